```python
import math
import jax, jax.numpy as jnp
from jax import lax
import numpy as np

D_MODEL = 2048
BATCH = 4
SEQ = 4096
DEPTH = 1
DEC_BATCH = 128
DEC_SEQ = 8
PAST_LEN = 16384
PAGE_SIZE = 128

HEAD_DIM_A = 64
N_HEADS_A = D_MODEL // 128
N_KV_A = N_HEADS_A // 4
GROUP_A = N_HEADS_A // N_KV_A
WINDOW = 128
ATTN_BLOCK = 128
N_HEADS_R = 8
DK_R = D_MODEL // (4 * N_HEADS_R)
DV_R = 2 * DK_R
RET_CHUNK = 128
PEER_HEADS = 8
N_KEYS = 128
N_EXPERTS = N_KEYS * N_KEYS
PEER_TOPK = 16
D_KEY = 256
D_KEY_HALF = D_KEY // 2
PEER_BLOCK = 128
PLE_DIM = 256
ROPE_THETA = 10000.0
EPS = 1e-6
NEG_INF = -1e30

IN_SIZES = (N_HEADS_A * HEAD_DIM_A, N_KV_A * HEAD_DIM_A, N_KV_A * HEAD_DIM_A,
            N_HEADS_R * DK_R, N_HEADS_R * DK_R, N_HEADS_R * DV_R, N_HEADS_R * DV_R,
            D_MODEL, D_MODEL)
IN_COLS = sum(IN_SIZES)

kernel_name = "hybrid_swa_retention_peer_step"


def _rms_norm(x, g):
    xf = x.astype(jnp.float32)
    y = xf * lax.rsqrt(jnp.mean(xf * xf, axis=-1, keepdims=True) + EPS)
    return (y * g.astype(jnp.float32)).astype(x.dtype)


def _rope(x, pos):
    d = x.shape[-1]
    inv = ROPE_THETA ** (-jnp.arange(0, d, 2, dtype=jnp.float32) / d)
    ang = pos.astype(jnp.float32)[:, None] * inv[None, :]
    cos = jnp.cos(ang)[:, None, :]
    sin = jnp.sin(ang)[:, None, :]
    xf = x.astype(jnp.float32)
    x1, x2 = xf[..., : d // 2], xf[..., d // 2:]
    return jnp.concatenate([x1 * cos - x2 * sin, x2 * cos + x1 * sin], axis=-1).astype(x.dtype)


def _sink_window_attention(q, k, v, q_pos, k_pos, sinks):
    s = jnp.einsum('bnqkgd,bnskd->bnkgqs', q, k).astype(jnp.float32) * (q.shape[-1] ** -0.5)
    diff = q_pos[:, :, None] - k_pos[:, None, :]
    ok = (diff >= 0) & (diff <= WINDOW) & (k_pos[:, None, :] >= 0)
    s = jnp.where(ok[None, :, None, None], s, NEG_INF)
    sink = sinks.astype(jnp.float32)[None, None, :, :, None, None]
    m = jnp.maximum(s.max(axis=-1, keepdims=True), sink)
    e = jnp.exp(s - m)
    p = e / (e.sum(axis=-1, keepdims=True) + jnp.exp(sink - m))
    return jnp.einsum('bnkgqs,bnskd->bnqkgd', p.astype(v.dtype), v)


def _retention(q, k, v, s0):
    Bn, T, H, _ = q.shape
    dv = v.shape[-1]
    C = math.gcd(T, RET_CHUNK)
    NC = T // C
    log_g = jnp.log1p(-jnp.exp2(-5.0 - jnp.arange(H, dtype=jnp.float32)))
    i = jnp.arange(C, dtype=jnp.float32)
    diff = i[:, None] - i[None, :]
    intra = jnp.where(diff[None] >= 0, jnp.exp(jnp.maximum(diff, 0.0)[None] * log_g[:, None, None]), 0.0)
    cross = jnp.exp((i + 1.0)[:, None] * log_g[None, :])
    kdec = jnp.exp((C - 1.0 - i)[:, None] * log_g[None, :])
    cdec = jnp.exp(C * log_g)

    def to_chunks(a):
        return a.astype(jnp.float32).reshape(Bn, NC, C, H, a.shape[-1]).transpose(1, 0, 2, 3, 4)

    def step(s, qkv):
        qc, kc, vc = qkv
        att = jnp.einsum('bihd,bjhd->bhij', qc, kc) * intra[None]
        o = (jnp.einsum('bhij,bjhv->bihv', att, vc)
             + jnp.einsum('bihd,bhdv->bihv', qc, s) * cross[None, :, :, None])
        s = s * cdec[None, :, None, None] + jnp.einsum('bjhd,bjhv->bhdv', kc * kdec[None, :, :, None], vc)
        return s, o

    s_fin, o = lax.scan(step, s0.astype(jnp.float32), (to_chunks(q), to_chunks(k), to_chunks(v)))
    o = o.transpose(1, 0, 2, 3, 4).reshape(Bn, T, H, dv)
    return o, s_fin.astype(s0.dtype)


def _peer(h, w_query, sub_keys, u_tab, v_tab):
    T = h.shape[0]
    qry = (h @ w_query).reshape(T, PEER_HEADS, 2, D_KEY_HALF)
    sc = jnp.einsum('thpc,pnc->thpn', qry, sub_keys).astype(jnp.float32)
    top_s, top_i = lax.top_k(sc, PEER_TOPK)
    comb = (top_s[:, :, 0, :, None] + top_s[:, :, 1, None, :]).reshape(T, PEER_HEADS, PEER_TOPK * PEER_TOPK)
    cidx = (top_i[:, :, 0, :, None] * N_KEYS + top_i[:, :, 1, None, :]).reshape(T, PEER_HEADS, PEER_TOPK * PEER_TOPK)
    best_s, best_j = lax.top_k(comb, PEER_TOPK)
    eidx = jnp.take_along_axis(cidx, best_j, axis=-1)
    gate = jax.nn.softmax(best_s, axis=-1)
    nblk = -(-T // PEER_BLOCK)
    pad = nblk * PEER_BLOCK - T
    hp = jnp.pad(h, ((0, pad), (0, 0))).reshape(nblk, PEER_BLOCK, D_MODEL)
    ip = jnp.pad(eidx, ((0, pad), (0, 0), (0, 0))).reshape(nblk, PEER_BLOCK, PEER_HEADS, PEER_TOPK)
    gp = jnp.pad(gate, ((0, pad), (0, 0), (0, 0))).reshape(nblk, PEER_BLOCK, PEER_HEADS, PEER_TOPK)

    def blk(args):
        hb, ib, gb = args
        u = u_tab[ib]
        a = jnp.einsum('bd,bhkd->bhk', hb, u).astype(jnp.float32)
        a = jax.nn.gelu(a, approximate=False) * gb
        return jnp.einsum('bhk,bhkd->bd', a.astype(v_tab.dtype), v_tab[ib])

    out = lax.map(blk, (hp, ip, gp)).reshape(nblk * PEER_BLOCK, D_MODEL)[:T]
    return out.astype(h.dtype)


def _layer(x, pe, pos0, k_buf, v_buf, s0, ln1, w_in, sinks, ln_ret, w_br_a, w_br_r, w_o,
           ln2, w_query, sub_keys, peer_u, peer_v, ln_ple, w_ple_gate, w_ple_proj):
    Bn, T, _ = x.shape
    pos = pos0 + jnp.arange(T, dtype=jnp.int32)
    h = _rms_norm(x, ln1)
    offs = [sum(IN_SIZES[:j]) for j in range(1, len(IN_SIZES))]
    q_a, k_a, v_a, q_r, k_r, v_r, g_r, gate_a, gate_r = jnp.split(h @ w_in, offs, axis=-1)

    q_a = _rope(q_a.reshape(Bn, T, N_HEADS_A, HEAD_DIM_A), pos).reshape(Bn, T, N_KV_A, GROUP_A, HEAD_DIM_A)
    k_a = _rope(k_a.reshape(Bn, T, N_KV_A, HEAD_DIM_A), pos)
    v_a = v_a.reshape(Bn, T, N_KV_A, HEAD_DIM_A)
    if k_buf is None:
        NB = T // ATTN_BLOCK
        qb = q_a.reshape(Bn, NB, ATTN_BLOCK, N_KV_A, GROUP_A, HEAD_DIM_A)
        kb = k_a.reshape(Bn, NB, ATTN_BLOCK, N_KV_A, HEAD_DIM_A)
        vb = v_a.reshape(Bn, NB, ATTN_BLOCK, N_KV_A, HEAD_DIM_A)
        k_ctx = jnp.concatenate([jnp.concatenate([jnp.zeros_like(kb[:, :1]), kb[:, :-1]], axis=1), kb], axis=2)
        v_ctx = jnp.concatenate([jnp.concatenate([jnp.zeros_like(vb[:, :1]), vb[:, :-1]], axis=1), vb], axis=2)
        q_pos = pos.reshape(NB, ATTN_BLOCK)
        k_pos = q_pos[:, :1] - ATTN_BLOCK + jnp.arange(2 * ATTN_BLOCK, dtype=jnp.int32)[None, :]
        new_k, new_v = k_a[:, -WINDOW:], v_a[:, -WINDOW:]
    else:
        k_all = jnp.concatenate([k_buf.astype(k_a.dtype), k_a], axis=1)
        v_all = jnp.concatenate([v_buf.astype(v_a.dtype), v_a], axis=1)
        qb, k_ctx, v_ctx = q_a[:, None], k_all[:, None], v_all[:, None]
        q_pos = pos[None, :]
        k_pos = (pos0 - WINDOW + jnp.arange(WINDOW + T, dtype=jnp.int32))[None, :]
        new_k, new_v = k_all[:, -WINDOW:], v_all[:, -WINDOW:]
    o_a = _sink_window_attention(qb, k_ctx, v_ctx, q_pos, k_pos, sinks.reshape(N_KV_A, GROUP_A))
    br_a = o_a.reshape(Bn, T, N_HEADS_A * HEAD_DIM_A) @ w_br_a

    q_r = _rope(q_r.reshape(Bn, T, N_HEADS_R, DK_R), pos)
    k_r = _rope(k_r.reshape(Bn, T, N_HEADS_R, DK_R), pos) * (DK_R ** -0.5)
    v_r = v_r.reshape(Bn, T, N_HEADS_R, DV_R)
    o_r, s_new = _retention(q_r, k_r, v_r, s0)
    mu = o_r.mean(axis=-1, keepdims=True)
    var = jnp.mean((o_r - mu) ** 2, axis=-1, keepdims=True)
    o_r = (o_r - mu) * lax.rsqrt(var + EPS) * ln_ret.astype(jnp.float32).reshape(N_HEADS_R, DV_R)
    o_r = o_r.reshape(Bn, T, N_HEADS_R * DV_R).astype(x.dtype) * jax.nn.silu(g_r)
    br_r = o_r @ w_br_r

    x = x + (jax.nn.sigmoid(gate_a) * br_a + jax.nn.sigmoid(gate_r) * br_r) @ w_o
    h2 = _rms_norm(x, ln2)
    x = x + _peer(h2.reshape(Bn * T, D_MODEL), w_query, sub_keys, peer_u, peer_v).reshape(Bn, T, D_MODEL)
    x = x + jax.nn.sigmoid(_rms_norm(x, ln_ple) @ w_ple_gate) * (pe @ w_ple_proj)
    return x, new_k, new_v, s_new


def setup_inputs(seed: int = 0) -> dict:
    key = jax.random.key(seed)
    ks = jax.random.split(key, 24)
    f = jnp.float32
    nrm = lambda k, shape, s=1.0: jax.random.normal(k, shape, f) * s
    gain = lambda k, shape: 1.0 + 0.02 * jax.random.normal(k, shape, f)
    return {
        "x_prompt": nrm(ks[0], (BATCH, SEQ, D_MODEL)),
        "x_sample": nrm(ks[1], (DEC_BATCH, DEC_SEQ, D_MODEL)),
        "cache_k_win": nrm(ks[2], (DEPTH, DEC_BATCH, WINDOW, N_KV_A, HEAD_DIM_A)),
        "cache_v_win": nrm(ks[3], (DEPTH, DEC_BATCH, WINDOW, N_KV_A, HEAD_DIM_A)),
        "state_ret": nrm(ks[4], (DEPTH, DEC_BATCH, N_HEADS_R, DK_R, DV_R)),
        "p_prompt": nrm(ks[5], (DEPTH, BATCH, SEQ, PLE_DIM)),
        "p_sample": nrm(ks[6], (DEPTH, DEC_BATCH, DEC_SEQ, PLE_DIM)),
        "ln1": gain(ks[7], (DEPTH, D_MODEL)),
        "w_in": nrm(ks[8], (DEPTH, D_MODEL, IN_COLS), D_MODEL ** -0.5),
        "attn_sinks": nrm(ks[9], (DEPTH, N_HEADS_A), 0.5),
        "ln_ret": gain(ks[10], (DEPTH, N_HEADS_R * DV_R)),
        "w_branch_attn": nrm(ks[11], (DEPTH, N_HEADS_A * HEAD_DIM_A, D_MODEL), (N_HEADS_A * HEAD_DIM_A) ** -0.5),
        "w_branch_ret": nrm(ks[12], (DEPTH, N_HEADS_R * DV_R, D_MODEL), (N_HEADS_R * DV_R) ** -0.5),
        "w_out": nrm(ks[13], (DEPTH, D_MODEL, D_MODEL), D_MODEL ** -0.5),
        "ln2": gain(ks[14], (DEPTH, D_MODEL)),
        "w_peer_query": nrm(ks[15], (DEPTH, D_MODEL, PEER_HEADS * D_KEY), D_MODEL ** -0.5),
        "peer_sub_keys": nrm(ks[16], (DEPTH, 2, N_KEYS, D_KEY_HALF), D_KEY_HALF ** -0.5),
        "peer_u": nrm(ks[17], (DEPTH, N_EXPERTS, D_MODEL), D_MODEL ** -0.5),
        "peer_v": nrm(ks[18], (DEPTH, N_EXPERTS, D_MODEL), PEER_HEADS ** -0.5),
        "ln_ple": gain(ks[19], (DEPTH, D_MODEL)),
        "w_ple_gate": nrm(ks[20], (DEPTH, D_MODEL, D_MODEL), D_MODEL ** -0.5),
        "w_ple_proj": nrm(ks[21], (DEPTH, PLE_DIM, D_MODEL), PLE_DIM ** -0.5),
        "ln_final": gain(ks[22], (D_MODEL,)),
    }


def reference(x_prompt, x_sample, cache_k_win, cache_v_win, state_ret, p_prompt, p_sample,
              ln1, w_in, attn_sinks, ln_ret, w_branch_attn, w_branch_ret, w_out, ln2,
              w_peer_query, peer_sub_keys, peer_u, peer_v, ln_ple, w_ple_gate, w_ple_proj, ln_final):
    xp, xs = x_prompt, x_sample
    kp_l, vp_l, sp_l, ks_l, vs_l, ss_l = [], [], [], [], [], []
    for i in range(DEPTH):
        lw = (ln1[i], w_in[i], attn_sinks[i], ln_ret[i], w_branch_attn[i], w_branch_ret[i], w_out[i],
              ln2[i], w_peer_query[i], peer_sub_keys[i], peer_u[i], peer_v[i],
              ln_ple[i], w_ple_gate[i], w_ple_proj[i])
        s0 = jnp.zeros((xp.shape[0], N_HEADS_R, DK_R, DV_R), xp.dtype)
        xp, kp, vp, sp = _layer(xp, p_prompt[i], 0, None, None, s0, *lw)
        xs, kn, vn, sn = _layer(xs, p_sample[i], PAST_LEN, cache_k_win[i], cache_v_win[i], state_ret[i], *lw)
        kp_l.append(kp); vp_l.append(vp); sp_l.append(sp)
        ks_l.append(kn); vs_l.append(vn); ss_l.append(sn)
    y_prompt = _rms_norm(xp, ln_final)
    y_sample = _rms_norm(xs, ln_final)
    return (y_prompt, y_sample, jnp.stack(kp_l), jnp.stack(vp_l), jnp.stack(sp_l),
            jnp.stack(ks_l), jnp.stack(vs_l), jnp.stack(ss_l))
```

```python
import functools
import math

import jax
import jax.numpy as jnp
from jax import lax
from jax.experimental import pallas as pl
from jax.experimental.pallas import tpu as pltpu

F32 = jnp.float32
BF16 = jnp.bfloat16

D_MODEL = 2048
PAST_LEN = 16384
HEAD_DIM_A = 64
N_HEADS_A = 16
N_KV_A = 4
GROUP_A = 4
WINDOW = 128
ATTN_BLOCK = 128
N_HEADS_R = 8
DK_R = 64
DV_R = 128
RET_CHUNK = 128
PEER_HEADS = 8
N_KEYS = 128
PEER_TOPK = 16
D_KEY_HALF = 128
N_SEL = PEER_HEADS * PEER_TOPK
ROPE_THETA = 10000.0
EPS = 1e-6
NEG_INF = -1e30

LANES = 128
SUBLANES = 8
VMEM_LIMIT_BYTES = 56 * 1024 * 1024

_REF_FIELDS = (("q_a", 1024), ("k_a", 256), ("v_a", 256), ("q_r", 512), ("k_r", 512),
               ("v_r", 1024), ("g_r", 1024), ("gate_a", 2048), ("gate_r", 2048))
_NEW_ORDER = ("gate_a", "gate_r", "q_a", "v_r", "g_r", "q_r", "k_r", "k_a", "v_a")
_ROPE_FIELDS = ("q_a", "q_r", "k_r", "k_a")
PROJ_TILE = 256


def _layout():
    ref_off, o = {}, 0
    for name, w in _REF_FIELDS:
        ref_off[name] = (o, w)
        o += w
    new_off, o = {}, 0
    for name in _NEW_ORDER:
        new_off[name] = (o, ref_off[name][1])
        o += ref_off[name][1]
    return new_off, ref_off, o


COL, _REF_COL, IN_COLS = _layout()


def _blk(name, width):
    off, w = COL[name]
    assert off % width == 0 and w % width == 0
    return off // width


def _params(sem, vmem=VMEM_LIMIT_BYTES):
    return pltpu.CompilerParams(dimension_semantics=sem, vmem_limit_bytes=vmem)


def _rms(x, g):
    return x * lax.rsqrt(jnp.mean(x * x, axis=-1, keepdims=True) + EPS) * g


def _sigmoid(x):
    return 1.0 / (1.0 + jnp.exp(-x))


def _dot(a, b):
    return jnp.dot(a, b, preferred_element_type=F32)


def _dot_nt(a, b):
    return lax.dot_general(a, b, (((1,), (1,)), ((), ())), preferred_element_type=F32)


def _dot_tn(a, b):
    return lax.dot_general(a, b, (((0,), (0,)), ((), ())), preferred_element_type=F32)


def _in_proj_kernel(x_ref, g_ref, w_ref, cos_ref, sin_ref, cs_ref, o_ref, h_ref, *, rope_ranges):
    j = pl.program_id(1)

    @pl.when(j == 0)
    def _():
        h_ref[...] = _rms(x_ref[...], g_ref[...]).astype(BF16)

    acc = _dot(h_ref[...], w_ref[...])
    is_rope = functools.reduce(jnp.logical_or, [(j >= a) & (j < b) for a, b in rope_ranges])

    @pl.when(is_rope)
    def _():
        tn = acc.shape[1]
        lane = lax.broadcasted_iota(jnp.int32, acc.shape, 1)
        first_half = (lane % HEAD_DIM_A) < (HEAD_DIM_A // 2)
        partner = jnp.where(first_half, pltpu.roll(acc, tn - HEAD_DIM_A // 2, 1), pltpu.roll(acc, HEAD_DIM_A // 2, 1))
        o_ref[...] = (acc * cos_ref[...] + partner * sin_ref[...]) * cs_ref[...]

    @pl.when(jnp.logical_not(is_rope))
    def _():
        o_ref[...] = acc


def _in_proj(x2d, ln, w, cos, sin, colscale, tm):
    n = x2d.shape[0]
    rt = cos.shape[0] // tm
    rope_ranges = tuple((COL[f][0] // PROJ_TILE, (COL[f][0] + COL[f][1]) // PROJ_TILE) for f in _ROPE_FIELDS)
    return pl.pallas_call(
        functools.partial(_in_proj_kernel, rope_ranges=rope_ranges),
        out_shape=jax.ShapeDtypeStruct((n, IN_COLS), F32),
        grid=(n // tm, IN_COLS // PROJ_TILE),
        in_specs=[
            pl.BlockSpec((tm, D_MODEL), lambda i, j: (i, 0)),
            pl.BlockSpec((1, D_MODEL), lambda i, j: (0, 0)),
            pl.BlockSpec((D_MODEL, PROJ_TILE), lambda i, j: (0, j)),
            pl.BlockSpec((tm, PROJ_TILE), lambda i, j: (i % rt, 0)),
            pl.BlockSpec((tm, PROJ_TILE), lambda i, j: (i % rt, 0)),
            pl.BlockSpec((1, PROJ_TILE), lambda i, j: (0, j)),
        ],
        out_specs=pl.BlockSpec((tm, PROJ_TILE), lambda i, j: (i, j)),
        scratch_shapes=[pltpu.VMEM((tm, D_MODEL), BF16)],
        compiler_params=_params(("arbitrary", "arbitrary")),
        name="in_proj",
    )(x2d, ln, w, cos, sin, colscale)


def _attn_kernel(sink_ref, q_ref, kc_ref, vc_ref, kx_ref, vx_ref, o_ref, *, first_ctx_invalid):
    bt, c, _ = q_ref.shape
    rows = GROUP_A * c
    qi = lax.broadcasted_iota(jnp.int32, (rows, WINDOW), 0) % c
    kj = lax.broadcasted_iota(jnp.int32, (rows, WINDOW), 1)
    ctx_ok = kj >= qi
    if first_ctx_invalid:
        ctx_ok = jnp.logical_and(ctx_ok, pl.program_id(1) > 0)
    qi_c = lax.broadcasted_iota(jnp.int32, (rows, c), 0) % c
    kj_c = lax.broadcasted_iota(jnp.int32, (rows, c), 1)
    cur_ok = kj_c <= qi_c
    grp = lax.broadcasted_iota(jnp.int32, (rows, 1), 0) // c
    scale = HEAD_DIM_A ** -0.5

    def one(b):
        q = q_ref[b]
        kc, vc, kx, vx = kc_ref[b], vc_ref[b], kx_ref[b], vx_ref[b]
        outs = [None] * N_HEADS_A
        for kh in range(N_KV_A):
            sl = slice(kh * HEAD_DIM_A, (kh + 1) * HEAD_DIM_A)
            q4 = jnp.concatenate(
                [q[:, (kh * GROUP_A + g) * HEAD_DIM_A:(kh * GROUP_A + g + 1) * HEAD_DIM_A] for g in range(GROUP_A)],
                axis=0).astype(BF16)
            s_x = jnp.where(ctx_ok, _dot_nt(q4, kx[:, sl].astype(BF16)) * scale, NEG_INF)
            s_c = jnp.where(cur_ok, _dot_nt(q4, kc[:, sl].astype(BF16)) * scale, NEG_INF)
            sink = jnp.zeros((rows, 1), F32)
            for g in range(GROUP_A):
                sink = jnp.where(grp == g, sink_ref[kh * GROUP_A + g], sink)
            m = jnp.maximum(jnp.maximum(jnp.max(s_x, axis=-1, keepdims=True), jnp.max(s_c, axis=-1, keepdims=True)), sink)
            e_x = jnp.exp(s_x - m)
            e_c = jnp.exp(s_c - m)
            den = jnp.sum(e_x, axis=-1, keepdims=True) + jnp.sum(e_c, axis=-1, keepdims=True) + jnp.exp(sink - m)
            o4 = _dot((e_x / den).astype(BF16), vx[:, sl].astype(BF16)) + _dot((e_c / den).astype(BF16), vc[:, sl].astype(BF16))
            for g in range(GROUP_A):
                outs[kh * GROUP_A + g] = o4[g * c:(g + 1) * c]
        o_ref[b] = jnp.concatenate(outs, axis=1).astype(o_ref.dtype)

    if bt == 1:
        one(0)
    else:
        def body(b, carry):
            one(b)
            return carry
        lax.fori_loop(0, bt, body, 0)


def _attention(sinks, proj3, ctx_k, ctx_v, bt, c, prompt):
    bn, t, _ = proj3.shape
    qb, kb, vb = _blk("q_a", 1024), _blk("k_a", 256), _blk("v_a", 256)
    if prompt:
        ctx_specs = [pl.BlockSpec((bt, WINDOW, 256), lambda b, n: (b, jnp.maximum(n - 1, 0), kb)),
                     pl.BlockSpec((bt, WINDOW, 256), lambda b, n: (b, jnp.maximum(n - 1, 0), vb))]
    else:
        ctx_specs = [pl.BlockSpec((bt, WINDOW, 256), lambda b, n: (b, 0, 0)),
                     pl.BlockSpec((bt, WINDOW, 256), lambda b, n: (b, 0, 0))]
    return pl.pallas_call(
        functools.partial(_attn_kernel, first_ctx_invalid=prompt),
        out_shape=jax.ShapeDtypeStruct((bn, t, N_HEADS_A * HEAD_DIM_A), BF16),
        grid=(bn // bt, t // c),
        in_specs=[
            pl.BlockSpec(memory_space=pltpu.SMEM),
            pl.BlockSpec((bt, c, 1024), lambda b, n: (b, n, qb)),
            pl.BlockSpec((bt, c, 256), lambda b, n: (b, n, kb)),
            pl.BlockSpec((bt, c, 256), lambda b, n: (b, n, vb)),
        ] + ctx_specs,
        out_specs=pl.BlockSpec((bt, c, 1024), lambda b, n: (b, n, 0)),
        compiler_params=_params(("arbitrary", "arbitrary")),
        name="attn_prompt" if prompt else "attn_sample",
    )(sinks, proj3, proj3, proj3, ctx_k, ctx_v)


def _ret_kernel(cdec_ref, q_ref, k_ref, v_ref, g_ref, intra_ref, cross_ref, kdec_ref, ln_ref, *rest, has_s0):
    if has_s0:
        s0_ref, o_ref, sout_ref, s_ref = rest
    else:
        o_ref, sout_ref, s_ref = rest
    bt = q_ref.shape[0]
    ci = pl.program_id(1)

    @pl.when(ci == 0)
    def _():
        s_ref[...] = s0_ref[...] if has_s0 else jnp.zeros(s_ref.shape, F32)

    def one(b):
        q, k, v, gt = q_ref[b], k_ref[b], v_ref[b], g_ref[b]
        outs = []
        for h in range(N_HEADS_R):
            qh = q[:, h * DK_R:(h + 1) * DK_R].astype(BF16)
            kh = k[:, h * DK_R:(h + 1) * DK_R]
            vh = v[:, h * DV_R:(h + 1) * DV_R].astype(BF16)
            s = s_ref[b, h]
            att = _dot_nt(qh, kh.astype(BF16)) * intra_ref[h]
            o = _dot(att.astype(BF16), vh) + _dot(qh, s.astype(BF16)) * cross_ref[h]
            s_ref[b, h] = s * cdec_ref[h] + _dot_tn((kh * kdec_ref[h]).astype(BF16), vh)
            mu = jnp.mean(o, axis=-1, keepdims=True)
            d = o - mu
            var = jnp.mean(d * d, axis=-1, keepdims=True)
            gh = gt[:, h * DV_R:(h + 1) * DV_R]
            outs.append(d * lax.rsqrt(var + EPS) * ln_ref[h:h + 1, :] * (gh * _sigmoid(gh)))
        o_ref[b] = jnp.concatenate(outs, axis=1).astype(o_ref.dtype)

    if bt == 1:
        one(0)
    else:
        def body(b, carry):
            one(b)
            return carry
        lax.fori_loop(0, bt, body, 0)

    @pl.when(ci == pl.num_programs(1) - 1)
    def _():
        sout_ref[...] = s_ref[...]


def _retention(proj3, s0, ln_ret, bt, c):
    bn, t, _ = proj3.shape
    log_g = jnp.log1p(-jnp.exp2(-5.0 - jnp.arange(N_HEADS_R, dtype=F32)))
    i = jnp.arange(c, dtype=F32)
    diff = i[:, None] - i[None, :]
    intra = jnp.where(diff[None] >= 0, jnp.exp(jnp.maximum(diff, 0.0)[None] * log_g[:, None, None]), 0.0)
    cross = jnp.exp((i + 1.0)[None, :] * log_g[:, None])
    kdec = jnp.exp((c - 1.0 - i)[None, :] * log_g[:, None])
    cdec = jnp.exp(c * log_g)
    cross_b = jnp.broadcast_to(cross[:, :, None], (N_HEADS_R, c, DV_R))
    kdec_b = jnp.broadcast_to(kdec[:, :, None], (N_HEADS_R, c, DK_R))
    has_s0 = s0 is not None
    const3 = lambda b, n: (0, 0, 0)
    state_spec = pl.BlockSpec((bt, N_HEADS_R, DK_R, DV_R), lambda b, n: (b, 0, 0, 0))
    in_specs = [
        pl.BlockSpec(memory_space=pltpu.SMEM),
        pl.BlockSpec((bt, c, 512), lambda b, n: (b, n, _blk("q_r", 512))),
        pl.BlockSpec((bt, c, 512), lambda b, n: (b, n, _blk("k_r", 512))),
        pl.BlockSpec((bt, c, 1024), lambda b, n: (b, n, _blk("v_r", 1024))),
        pl.BlockSpec((bt, c, 1024), lambda b, n: (b, n, _blk("g_r", 1024))),
        pl.BlockSpec((N_HEADS_R, c, c), const3),
        pl.BlockSpec((N_HEADS_R, c, DV_R), const3),
        pl.BlockSpec((N_HEADS_R, c, DK_R), const3),
        pl.BlockSpec((N_HEADS_R, DV_R), lambda b, n: (0, 0)),
    ]
    args = [cdec, proj3, proj3, proj3, proj3, intra, cross_b, kdec_b, ln_ret]
    if has_s0:
        in_specs.append(state_spec)
        args.append(s0)
    return pl.pallas_call(
        functools.partial(_ret_kernel, has_s0=has_s0),
        out_shape=(jax.ShapeDtypeStruct((bn, t, N_HEADS_R * DV_R), BF16),
                   jax.ShapeDtypeStruct((bn, N_HEADS_R, DK_R, DV_R), F32)),
        grid=(bn // bt, t // c),
        in_specs=in_specs,
        out_specs=(pl.BlockSpec((bt, c, 1024), lambda b, n: (b, n, 0)), state_spec),
        scratch_shapes=[pltpu.VMEM((bt, N_HEADS_R, DK_R, DV_R), F32)],
        compiler_params=_params(("arbitrary", "arbitrary")),
        name="ret_sample" if has_s0 else "ret_prompt",
    )(*args)


def _merge_kernel(oa_ref, or_ref, ga_ref, gr_ref, wa_ref, wr_ref, m_ref):
    br_a = _dot(oa_ref[...], wa_ref[...])
    br_r = _dot(or_ref[...], wr_ref[...])
    m_ref[...] = (_sigmoid(ga_ref[...]) * br_a + _sigmoid(gr_ref[...]) * br_r).astype(m_ref.dtype)


def _merge(o_a, o_r, proj, w_a, w_r, tm, tn=512):
    n = o_a.shape[0]
    ga, gr = _blk("gate_a", tn), _blk("gate_r", tn)
    return pl.pallas_call(
        _merge_kernel,
        out_shape=jax.ShapeDtypeStruct((n, D_MODEL), BF16),
        grid=(n // tm, D_MODEL // tn),
        in_specs=[
            pl.BlockSpec((tm, 1024), lambda i, j: (i, 0)),
            pl.BlockSpec((tm, 1024), lambda i, j: (i, 0)),
            pl.BlockSpec((tm, tn), lambda i, j: (i, ga + j)),
            pl.BlockSpec((tm, tn), lambda i, j: (i, gr + j)),
            pl.BlockSpec((1024, tn), lambda i, j: (0, j)),
            pl.BlockSpec((1024, tn), lambda i, j: (0, j)),
        ],
        out_specs=pl.BlockSpec((tm, tn), lambda i, j: (i, j)),
        compiler_params=_params(("arbitrary", "arbitrary")),
        name="merge",
    )(o_a, o_r, proj, proj, w_a, w_r)


def _out_proj_kernel(x_ref, m_ref, w_ref, o_ref):
    o_ref[...] = x_ref[...] + _dot(m_ref[...], w_ref[...])


def _out_proj(x2d, m, w_o, tm, tn=512):
    n = x2d.shape[0]
    return pl.pallas_call(
        _out_proj_kernel,
        out_shape=jax.ShapeDtypeStruct((n, D_MODEL), F32),
        grid=(n // tm, D_MODEL // tn),
        in_specs=[
            pl.BlockSpec((tm, tn), lambda i, j: (i, j)),
            pl.BlockSpec((tm, D_MODEL), lambda i, j: (i, 0)),
            pl.BlockSpec((D_MODEL, tn), lambda i, j: (0, j)),
        ],
        out_specs=pl.BlockSpec((tm, tn), lambda i, j: (i, j)),
        compiler_params=_params(("arbitrary", "arbitrary")),
        name="out_proj",
    )(x2d, m, w_o)


def _scores_kernel(x_ref, g_ref, w_ref, sk_ref, h_out_ref, s_ref, h_ref):
    j = pl.program_id(1)

    @pl.when(j == 0)
    def _():
        h = _rms(x_ref[...], g_ref[...])
        h_ref[...] = h.astype(BF16)
        h_out_ref[...] = h

    qry = _dot(h_ref[...], w_ref[...]).astype(BF16)
    for g in range(qry.shape[1] // D_KEY_HALF):
        s_ref[g] = _dot_nt(sk_ref[g % 2], qry[:, g * D_KEY_HALF:(g + 1) * D_KEY_HALF])


def _scores(x2d, ln, w_q, sub_keys, tm, tn=512):
    n = x2d.shape[0]
    ng = tn // D_KEY_HALF
    return pl.pallas_call(
        _scores_kernel,
        out_shape=(jax.ShapeDtypeStruct((n, D_MODEL), F32),
                   jax.ShapeDtypeStruct((2 * PEER_HEADS, N_KEYS, n), F32)),
        grid=(n // tm, D_MODEL // tn),
        in_specs=[
            pl.BlockSpec((tm, D_MODEL), lambda i, j: (i, 0)),
            pl.BlockSpec((1, D_MODEL), lambda i, j: (0, 0)),
            pl.BlockSpec((D_MODEL, tn), lambda i, j: (0, j)),
            pl.BlockSpec((2, N_KEYS, D_KEY_HALF), lambda i, j: (0, 0, 0)),
        ],
        out_specs=(pl.BlockSpec((tm, D_MODEL), lambda i, j: (i, 0)),
                   pl.BlockSpec((ng, N_KEYS, tm), lambda i, j: (j, 0, i))),
        scratch_shapes=[pltpu.VMEM((tm, D_MODEL), BF16)],
        compiler_params=_params(("arbitrary", "arbitrary")),
        name="peer_scores",
    )(x2d, ln, w_q, sub_keys)


def _take_top(vals, iota, count, fill, payload=None):
    n_rows = vals.shape[0]
    top_v, top_i = [], []
    for _ in range(count):
        m = jnp.max(vals, axis=0, keepdims=True)
        pos = jnp.min(jnp.where(vals == m, iota, n_rows), axis=0, keepdims=True)
        sel = iota == pos
        top_v.append(m)
        top_i.append(pos if payload is None else jnp.max(jnp.where(sel, payload, -1), axis=0, keepdims=True))
        vals = jnp.where(sel, fill, vals)
    return jnp.concatenate(top_v, axis=0), jnp.concatenate(top_i, axis=0)


def _topk_kernel(s_ref, e_ref, g_ref):
    tt = s_ref.shape[2]
    key_iota = lax.broadcasted_iota(jnp.int32, (N_KEYS, tt), 0)
    comb_iota = lax.broadcasted_iota(jnp.int32, (PEER_TOPK * PEER_TOPK, tt), 0)
    ninf = float("-inf")

    def head(h, carry):
        s0, i0 = _take_top(s_ref[2 * h], key_iota, PEER_TOPK, ninf)
        s1, i1 = _take_top(s_ref[2 * h + 1], key_iota, PEER_TOPK, ninf)
        comb = jnp.concatenate([s0[i:i + 1, :] + s1 for i in range(PEER_TOPK)], axis=0)
        cidx = jnp.concatenate([i0[i:i + 1, :] * N_KEYS + i1 for i in range(PEER_TOPK)], axis=0)
        best, eidx = _take_top(comb, comb_iota, PEER_TOPK, ninf, payload=cidx)
        ex = jnp.exp(best - jnp.max(best, axis=0, keepdims=True))
        g_ref[h] = ex / jnp.sum(ex, axis=0, keepdims=True)
        e_ref[h] = eidx
        return carry

    lax.fori_loop(0, PEER_HEADS, head, 0)


def _topk(scores_t, tt=256):
    n = scores_t.shape[2]
    out_spec = pl.BlockSpec((PEER_HEADS, PEER_TOPK, tt), lambda i: (0, 0, i))
    return pl.pallas_call(
        _topk_kernel,
        out_shape=(jax.ShapeDtypeStruct((PEER_HEADS, PEER_TOPK, n), jnp.int32),
                   jax.ShapeDtypeStruct((PEER_HEADS, PEER_TOPK, n), F32)),
        grid=(n // tt,),
        in_specs=[pl.BlockSpec((2 * PEER_HEADS, N_KEYS, tt), lambda i: (0, 0, i))],
        out_specs=(out_spec, out_spec),
        compiler_params=_params(("arbitrary",)),
        name="peer_topk",
    )(scores_t)


def _gelu(a):
    return 0.5 * a * (1.0 + lax.erf(a * (2.0 ** -0.5)))


def _peer_kernel(idx_hbm, h_ref, g_ref, x_ref, u_hbm, v_hbm, o_ref, idx_smem, ubuf, vbuf, sem_i, sem_u, sem_v):
    tt = h_ref.shape[0]
    step = pl.program_id(0)
    idx_copy = pltpu.make_async_copy(idx_hbm.at[pl.ds(step * tt, tt), :], idx_smem, sem_i)
    idx_copy.start()
    idx_copy.wait()

    def row_copies(t, k, slot):
        e = idx_smem[t, k]
        return (pltpu.make_async_copy(u_hbm.at[pl.ds(e, 1), :], ubuf.at[slot, pl.ds(k, 1), :], sem_u.at[slot]),
                pltpu.make_async_copy(v_hbm.at[pl.ds(e, 1), :], vbuf.at[slot, pl.ds(k, 1), :], sem_v.at[slot]))

    def gather_start(t, slot):
        for k in range(N_SEL):
            cu, cv = row_copies(t, k, slot)
            cu.start()
            cv.start()

    def gather_wait(slot):
        pltpu.make_async_copy(u_hbm.at[pl.ds(0, N_SEL), :], ubuf.at[slot], sem_u.at[slot]).wait()
        pltpu.make_async_copy(v_hbm.at[pl.ds(0, N_SEL), :], vbuf.at[slot], sem_v.at[slot]).wait()

    gather_start(0, 0)
    row_iota = lax.broadcasted_iota(jnp.int32, (SUBLANES, 1), 0)

    def group(gi, carry):
        r0 = pl.multiple_of(gi * SUBLANES, SUBLANES)
        h8 = h_ref[pl.ds(r0, SUBLANES), :]
        g8 = g_ref[pl.ds(r0, SUBLANES), :]

        def token(r, acc):
            t = r0 + r
            slot = t % 2

            @pl.when(t + 1 < tt)
            def _():
                gather_start(t + 1, 1 - slot)

            gather_wait(slot)
            hm = jnp.where(row_iota == r, h8, 0.0).astype(BF16)
            a = _dot_nt(hm, ubuf[slot].astype(BF16))
            w = (_gelu(a) * g8).astype(BF16)
            return acc + _dot(w, vbuf[slot].astype(BF16))

        acc = lax.fori_loop(0, SUBLANES, token, jnp.zeros((SUBLANES, D_MODEL), F32))
        o_ref[pl.ds(r0, SUBLANES), :] = x_ref[pl.ds(r0, SUBLANES), :] + acc
        return carry

    lax.fori_loop(0, tt // SUBLANES, group, 0)


def _peer(eidx, h2, gate, x2d, peer_u, peer_v, tt):
    n = x2d.shape[0]
    row = lambda i: (i, 0)
    return pl.pallas_call(
        _peer_kernel,
        out_shape=jax.ShapeDtypeStruct((n, D_MODEL), F32),
        grid=(n // tt,),
        in_specs=[
            pl.BlockSpec(memory_space=pl.ANY),
            pl.BlockSpec((tt, D_MODEL), row),
            pl.BlockSpec((tt, N_SEL), row),
            pl.BlockSpec((tt, D_MODEL), row),
            pl.BlockSpec(memory_space=pl.ANY),
            pl.BlockSpec(memory_space=pl.ANY),
        ],
        out_specs=pl.BlockSpec((tt, D_MODEL), row),
        scratch_shapes=[
            pltpu.SMEM((tt, N_SEL), jnp.int32),
            pltpu.VMEM((2, N_SEL, D_MODEL), F32),
            pltpu.VMEM((2, N_SEL, D_MODEL), F32),
            pltpu.SemaphoreType.DMA,
            pltpu.SemaphoreType.DMA((2,)),
            pltpu.SemaphoreType.DMA((2,)),
        ],
        compiler_params=_params(("arbitrary",)),
        name="peer_mix",
    )(eidx, h2, gate, x2d, peer_u, peer_v)


def _ple_kernel(x_ref, p_ref, lnp_ref, wg_ref, wp_ref, lnf_ref, y_ref):
    x = x_ref[...]
    gate = _sigmoid(_dot(_rms(x, lnp_ref[...]).astype(BF16), wg_ref[...]))
    x = x + gate * _dot(p_ref[...].astype(BF16), wp_ref[...])
    y_ref[...] = _rms(x, lnf_ref[...])


def _ple(x2d, p2d, ln_ple, w_gate, w_proj, ln_final, tm):
    n = x2d.shape[0]
    pd = p2d.shape[1]
    row = lambda i: (i, 0)
    const = lambda i: (0, 0)
    return pl.pallas_call(
        _ple_kernel,
        out_shape=jax.ShapeDtypeStruct((n, D_MODEL), F32),
        grid=(n // tm,),
        in_specs=[
            pl.BlockSpec((tm, D_MODEL), row),
            pl.BlockSpec((tm, pd), row),
            pl.BlockSpec((1, D_MODEL), const),
            pl.BlockSpec((D_MODEL, D_MODEL), const),
            pl.BlockSpec((pd, D_MODEL), const),
            pl.BlockSpec((1, D_MODEL), const),
        ],
        out_specs=pl.BlockSpec((tm, D_MODEL), row),
        compiler_params=_params(("arbitrary",)),
        name="ple_final",
    )(x2d, p2d, ln_ple, w_gate, w_proj, ln_final)


def _rope_tables(pos):
    inv = ROPE_THETA ** (-jnp.arange(0, HEAD_DIM_A, 2, dtype=F32) / HEAD_DIM_A)
    ang = pos.astype(F32)[:, None] * inv[None, :]
    cos, sin = jnp.cos(ang), jnp.sin(ang)
    reps = PROJ_TILE // HEAD_DIM_A
    return (jnp.tile(jnp.concatenate([cos, cos], axis=1), (1, reps)),
            jnp.tile(jnp.concatenate([-sin, sin], axis=1), (1, reps)))


def _pick(n, prefs):
    for c in prefs:
        if n % c == 0:
            return c
    raise ValueError(f"no tile of {prefs} divides {n}")


def _group(x, pe, pos0, cache_k, cache_v, s0, wts):
    bn, t, _ = x.shape
    n = bn * t
    prompt = cache_k is None
    x2d = x.reshape(n, D_MODEL)
    tm = _pick(n, (1024, 512, 256, 128))

    pos = pos0 + jnp.arange(t, dtype=jnp.int32)
    cos, sin = _rope_tables(pos)
    if t < tm:
        cos, sin = jnp.tile(cos, (tm // t, 1)), jnp.tile(sin, (tm // t, 1))
    proj = _in_proj(x2d, wts["ln1"], wts["w_in"], cos, sin, wts["colscale"], tm)
    proj3 = proj.reshape(bn, t, IN_COLS)

    if prompt:
        c_att, bt_att = ATTN_BLOCK, 1
        ctx_k = ctx_v = proj3
    else:
        c_att, bt_att = t, _pick(bn, (8, 4, 2, 1))
        ctx_k = cache_k.reshape(bn, WINDOW, N_KV_A * HEAD_DIM_A)
        ctx_v = cache_v.reshape(bn, WINDOW, N_KV_A * HEAD_DIM_A)
    o_a = _attention(wts["sinks"], proj3, ctx_k, ctx_v, bt_att, c_att, prompt)

    c_ret = math.gcd(t, RET_CHUNK)
    bt_ret = 1 if prompt else _pick(bn, (8, 4, 2, 1))
    o_r, s_new = _retention(proj3, s0, wts["ln_ret"], bt_ret, c_ret)

    m = _merge(o_a.reshape(n, -1), o_r.reshape(n, -1), proj, wts["w_br_a"], wts["w_br_r"], tm)
    x2 = _out_proj(x2d, m, wts["w_o"], tm)

    h2, scores_t = _scores(x2, wts["ln2"], wts["w_q"], wts["sub_keys"], _pick(n, (512, 256, 128)))
    eidx_t, gate_t = _topk(scores_t, _pick(n, (256, 128)))
    eidx = eidx_t.reshape(N_SEL, n).T
    gate = gate_t.reshape(N_SEL, n).T
    x3 = _peer(eidx, h2, gate, x2, wts["peer_u"], wts["peer_v"], _pick(n, (64, 32, 16, 8)))

    y = _ple(x3, pe.reshape(n, -1), wts["ln_ple"], wts["w_ple_gate"], wts["w_ple_proj"], wts["ln_final"],
             _pick(n, (256, 128)))

    k_off, v_off = COL["k_a"][0], COL["v_a"][0]
    kv_w = N_KV_A * HEAD_DIM_A
    k_new, v_new = proj3[:, :, k_off:k_off + kv_w], proj3[:, :, v_off:v_off + kv_w]
    if prompt:
        k_win, v_win = k_new[:, -WINDOW:], v_new[:, -WINDOW:]
    else:
        k_win = jnp.concatenate([ctx_k, k_new], axis=1)[:, -WINDOW:]
        v_win = jnp.concatenate([ctx_v, v_new], axis=1)[:, -WINDOW:]
    shp = (bn, WINDOW, N_KV_A, HEAD_DIM_A)
    return y.reshape(bn, t, D_MODEL), k_win.reshape(shp), v_win.reshape(shp), s_new


def kernel(x_prompt, x_sample, cache_k_win, cache_v_win, state_ret, p_prompt, p_sample, ln1, w_in, attn_sinks, ln_ret, w_branch_attn, w_branch_ret, w_out, ln2, w_peer_query, peer_sub_keys, peer_u, peer_v, ln_ple, w_ple_gate, w_ple_proj, ln_final):
    depth = ln1.shape[0]
    assert depth == 1, "single-layer step"
    i = 0
    colscale = jnp.ones((IN_COLS,), F32).at[COL["k_r"][0]:COL["k_r"][0] + COL["k_r"][1]].set(DK_R ** -0.5)
    wts = dict(
        ln1=ln1[i][None, :],
        w_in=jnp.concatenate([w_in[i][:, _REF_COL[f][0]:_REF_COL[f][0] + _REF_COL[f][1]] for f in _NEW_ORDER],
                             axis=1).astype(BF16),
        colscale=colscale[None, :],
        sinks=attn_sinks[i],
        ln_ret=ln_ret[i].reshape(N_HEADS_R, DV_R),
        w_br_a=w_branch_attn[i].astype(BF16),
        w_br_r=w_branch_ret[i].astype(BF16),
        w_o=w_out[i].astype(BF16),
        ln2=ln2[i][None, :],
        w_q=w_peer_query[i].astype(BF16),
        sub_keys=peer_sub_keys[i].astype(BF16),
        peer_u=peer_u[i],
        peer_v=peer_v[i],
        ln_ple=ln_ple[i][None, :],
        w_ple_gate=w_ple_gate[i].astype(BF16),
        w_ple_proj=w_ple_proj[i].astype(BF16),
        ln_final=ln_final[None, :],
    )
    yp, kp, vp, sp = _group(x_prompt, p_prompt[i], 0, None, None, None, wts)
    ys, ks, vs, ss = _group(x_sample, p_sample[i], PAST_LEN, cache_k_win[i], cache_v_win[i], state_ret[i], wts)
    return (yp, ys, kp[None], vp[None], sp[None], ks[None], vs[None], ss[None])
```

```python
import functools
import math

import jax
import jax.numpy as jnp
from jax import lax
from jax.experimental import pallas as pl
from jax.experimental.pallas import tpu as pltpu

F32 = jnp.float32
BF16 = jnp.bfloat16

D_MODEL = 2048
PAST_LEN = 16384
HEAD_DIM_A = 64
N_HEADS_A = 16
N_KV_A = 4
GROUP_A = 4
WINDOW = 128
ATTN_BLOCK = 128
N_HEADS_R = 8
DK_R = 64
DV_R = 128
RET_CHUNK = 128
PEER_HEADS = 8
N_KEYS = 128
PEER_TOPK = 16
D_KEY_HALF = 128
N_SEL = PEER_HEADS * PEER_TOPK
ROPE_THETA = 10000.0
EPS = 1e-6
NEG_INF = -1e30

LANES = 128
SUBLANES = 8
VMEM_LIMIT_BYTES = 56 * 1024 * 1024

_REF_FIELDS = (("q_a", 1024), ("k_a", 256), ("v_a", 256), ("q_r", 512), ("k_r", 512),
               ("v_r", 1024), ("g_r", 1024), ("gate_a", 2048), ("gate_r", 2048))
_NEW_ORDER = ("gate_a", "gate_r", "q_a", "v_r", "g_r", "q_r", "k_r", "k_a", "v_a")
_ROPE_FIELDS = ("q_a", "q_r", "k_r", "k_a")
PROJ_TILE = 256


def _layout():
    ref_off, o = {}, 0
    for name, w in _REF_FIELDS:
        ref_off[name] = (o, w)
        o += w
    new_off, o = {}, 0
    for name in _NEW_ORDER:
        new_off[name] = (o, ref_off[name][1])
        o += ref_off[name][1]
    return new_off, ref_off, o


COL, _REF_COL, IN_COLS = _layout()


def _blk(name, width):
    off, w = COL[name]
    assert off % width == 0 and w % width == 0
    return off // width


def _params(sem, vmem=VMEM_LIMIT_BYTES):
    return pltpu.CompilerParams(dimension_semantics=sem, vmem_limit_bytes=vmem)


def _rms(x, g):
    return x * lax.rsqrt(jnp.mean(x * x, axis=-1, keepdims=True) + EPS) * g


def _sigmoid(x):
    return 1.0 / (1.0 + jnp.exp(-x))


def _dot(a, b):
    return jnp.dot(a, b, preferred_element_type=F32)


def _dot_nt(a, b):
    return lax.dot_general(a, b, (((1,), (1,)), ((), ())), preferred_element_type=F32)


def _dot_tn(a, b):
    return lax.dot_general(a, b, (((0,), (0,)), ((), ())), preferred_element_type=F32)


def _in_proj_kernel(x_ref, g_ref, w_ref, cos_ref, sin_ref, cs_ref, o_ref, h_ref, *, rope_ranges):
    j = pl.program_id(1)

    @pl.when(j == 0)
    def _():
        h_ref[...] = _rms(x_ref[...], g_ref[...]).astype(BF16)

    acc = _dot(h_ref[...], w_ref[...])
    is_rope = functools.reduce(jnp.logical_or, [(j >= a) & (j < b) for a, b in rope_ranges])

    @pl.when(is_rope)
    def _():
        tn = acc.shape[1]
        lane = lax.broadcasted_iota(jnp.int32, acc.shape, 1)
        first_half = (lane % HEAD_DIM_A) < (HEAD_DIM_A // 2)
        partner = jnp.where(first_half, pltpu.roll(acc, tn - HEAD_DIM_A // 2, 1), pltpu.roll(acc, HEAD_DIM_A // 2, 1))
        o_ref[...] = (acc * cos_ref[...] + partner * sin_ref[...]) * cs_ref[...]

    @pl.when(jnp.logical_not(is_rope))
    def _():
        o_ref[...] = acc


def _in_proj(x2d, ln, w, cos, sin, colscale, tm):
    n = x2d.shape[0]
    rt = cos.shape[0] // tm
    rope_ranges = tuple((COL[f][0] // PROJ_TILE, (COL[f][0] + COL[f][1]) // PROJ_TILE) for f in _ROPE_FIELDS)
    return pl.pallas_call(
        functools.partial(_in_proj_kernel, rope_ranges=rope_ranges),
        out_shape=jax.ShapeDtypeStruct((n, IN_COLS), F32),
        grid=(n // tm, IN_COLS // PROJ_TILE),
        in_specs=[
            pl.BlockSpec((tm, D_MODEL), lambda i, j: (i, 0)),
            pl.BlockSpec((1, D_MODEL), lambda i, j: (0, 0)),
            pl.BlockSpec((D_MODEL, PROJ_TILE), lambda i, j: (0, j)),
            pl.BlockSpec((tm, PROJ_TILE), lambda i, j: (i % rt, 0)),
            pl.BlockSpec((tm, PROJ_TILE), lambda i, j: (i % rt, 0)),
            pl.BlockSpec((1, PROJ_TILE), lambda i, j: (0, j)),
        ],
        out_specs=pl.BlockSpec((tm, PROJ_TILE), lambda i, j: (i, j)),
        scratch_shapes=[pltpu.VMEM((tm, D_MODEL), BF16)],
        compiler_params=_params(("arbitrary", "arbitrary")),
        name="in_proj",
    )(x2d, ln, w, cos, sin, colscale)


def _attn_kernel(sink_ref, q_ref, kc_ref, vc_ref, kx_ref, vx_ref, o_ref, *, first_ctx_invalid):
    bt, c, _ = q_ref.shape
    rows = GROUP_A * c
    qi = lax.broadcasted_iota(jnp.int32, (rows, WINDOW), 0) % c
    kj = lax.broadcasted_iota(jnp.int32, (rows, WINDOW), 1)
    ctx_ok = kj >= qi
    if first_ctx_invalid:
        ctx_ok = jnp.logical_and(ctx_ok, pl.program_id(1) > 0)
    qi_c = lax.broadcasted_iota(jnp.int32, (rows, c), 0) % c
    kj_c = lax.broadcasted_iota(jnp.int32, (rows, c), 1)
    cur_ok = kj_c <= qi_c
    grp = lax.broadcasted_iota(jnp.int32, (rows, 1), 0) // c
    scale = HEAD_DIM_A ** -0.5

    def one(b):
        q = q_ref[b]
        kc, vc, kx, vx = kc_ref[b], vc_ref[b], kx_ref[b], vx_ref[b]
        outs = [None] * N_HEADS_A
        for kh in range(N_KV_A):
            sl = slice(kh * HEAD_DIM_A, (kh + 1) * HEAD_DIM_A)
            q4 = jnp.concatenate(
                [q[:, (kh * GROUP_A + g) * HEAD_DIM_A:(kh * GROUP_A + g + 1) * HEAD_DIM_A] for g in range(GROUP_A)],
                axis=0).astype(BF16)
            s_x = jnp.where(ctx_ok, _dot_nt(q4, kx[:, sl].astype(BF16)) * scale, NEG_INF)
            s_c = jnp.where(cur_ok, _dot_nt(q4, kc[:, sl].astype(BF16)) * scale, NEG_INF)
            sink = jnp.zeros((rows, 1), F32)
            for g in range(GROUP_A):
                sink = jnp.where(grp == g, sink_ref[kh * GROUP_A + g], sink)
            m = jnp.maximum(jnp.maximum(jnp.max(s_x, axis=-1, keepdims=True), jnp.max(s_c, axis=-1, keepdims=True)), sink)
            e_x = jnp.exp(s_x - m)
            e_c = jnp.exp(s_c - m)
            den = jnp.sum(e_x, axis=-1, keepdims=True) + jnp.sum(e_c, axis=-1, keepdims=True) + jnp.exp(sink - m)
            o4 = _dot((e_x / den).astype(BF16), vx[:, sl].astype(BF16)) + _dot((e_c / den).astype(BF16), vc[:, sl].astype(BF16))
            for g in range(GROUP_A):
                outs[kh * GROUP_A + g] = o4[g * c:(g + 1) * c]
        o_ref[b] = jnp.concatenate(outs, axis=1).astype(o_ref.dtype)

    if bt == 1:
        one(0)
    else:
        def body(b, carry):
            one(b)
            return carry
        lax.fori_loop(0, bt, body, 0)


def _attention(sinks, proj3, ctx_k, ctx_v, bt, c, prompt):
    bn, t, _ = proj3.shape
    qb, kb, vb = _blk("q_a", 1024), _blk("k_a", 256), _blk("v_a", 256)
    if prompt:
        ctx_specs = [pl.BlockSpec((bt, WINDOW, 256), lambda b, n: (b, jnp.maximum(n - 1, 0), kb)),
                     pl.BlockSpec((bt, WINDOW, 256), lambda b, n: (b, jnp.maximum(n - 1, 0), vb))]
    else:
        ctx_specs = [pl.BlockSpec((bt, WINDOW, 256), lambda b, n: (b, 0, 0)),
                     pl.BlockSpec((bt, WINDOW, 256), lambda b, n: (b, 0, 0))]
    return pl.pallas_call(
        functools.partial(_attn_kernel, first_ctx_invalid=prompt),
        out_shape=jax.ShapeDtypeStruct((bn, t, N_HEADS_A * HEAD_DIM_A), BF16),
        grid=(bn // bt, t // c),
        in_specs=[
            pl.BlockSpec(memory_space=pltpu.SMEM),
            pl.BlockSpec((bt, c, 1024), lambda b, n: (b, n, qb)),
            pl.BlockSpec((bt, c, 256), lambda b, n: (b, n, kb)),
            pl.BlockSpec((bt, c, 256), lambda b, n: (b, n, vb)),
        ] + ctx_specs,
        out_specs=pl.BlockSpec((bt, c, 1024), lambda b, n: (b, n, 0)),
        compiler_params=_params(("arbitrary", "arbitrary")),
        name="attn_prompt" if prompt else "attn_sample",
    )(sinks, proj3, proj3, proj3, ctx_k, ctx_v)


def _ret_kernel(cdec_ref, q_ref, k_ref, v_ref, g_ref, intra_ref, cross_ref, kdec_ref, ln_ref, *rest, has_s0):
    if has_s0:
        s0_ref, o_ref, sout_ref, s_ref = rest
    else:
        o_ref, sout_ref, s_ref = rest
    bt = q_ref.shape[0]
    ci = pl.program_id(1)

    @pl.when(ci == 0)
    def _():
        s_ref[...] = s0_ref[...] if has_s0 else jnp.zeros(s_ref.shape, F32)

    def one(b):
        q, k, v, gt = q_ref[b], k_ref[b], v_ref[b], g_ref[b]
        outs = []
        for h in range(N_HEADS_R):
            qh = q[:, h * DK_R:(h + 1) * DK_R].astype(BF16)
            kh = k[:, h * DK_R:(h + 1) * DK_R]
            vh = v[:, h * DV_R:(h + 1) * DV_R].astype(BF16)
            s = s_ref[b, h]
            att = _dot_nt(qh, kh.astype(BF16)) * intra_ref[h]
            o = _dot(att.astype(BF16), vh) + _dot(qh, s.astype(BF16)) * cross_ref[h]
            s_ref[b, h] = s * cdec_ref[h] + _dot_tn((kh * kdec_ref[h]).astype(BF16), vh)
            mu = jnp.mean(o, axis=-1, keepdims=True)
            d = o - mu
            var = jnp.mean(d * d, axis=-1, keepdims=True)
            gh = gt[:, h * DV_R:(h + 1) * DV_R]
            outs.append(d * lax.rsqrt(var + EPS) * ln_ref[h:h + 1, :] * (gh * _sigmoid(gh)))
        o_ref[b] = jnp.concatenate(outs, axis=1).astype(o_ref.dtype)

    if bt == 1:
        one(0)
    else:
        def body(b, carry):
            one(b)
            return carry
        lax.fori_loop(0, bt, body, 0)

    @pl.when(ci == pl.num_programs(1) - 1)
    def _():
        sout_ref[...] = s_ref[...]


def _retention(proj3, s0, ln_ret, bt, c):
    bn, t, _ = proj3.shape
    log_g = jnp.log1p(-jnp.exp2(-5.0 - jnp.arange(N_HEADS_R, dtype=F32)))
    i = jnp.arange(c, dtype=F32)
    diff = i[:, None] - i[None, :]
    intra = jnp.where(diff[None] >= 0, jnp.exp(jnp.maximum(diff, 0.0)[None] * log_g[:, None, None]), 0.0)
    cross = jnp.exp((i + 1.0)[None, :] * log_g[:, None])
    kdec = jnp.exp((c - 1.0 - i)[None, :] * log_g[:, None])
    cdec = jnp.exp(c * log_g)
    cross_b = jnp.broadcast_to(cross[:, :, None], (N_HEADS_R, c, DV_R))
    kdec_b = jnp.broadcast_to(kdec[:, :, None], (N_HEADS_R, c, DK_R))
    has_s0 = s0 is not None
    const3 = lambda b, n: (0, 0, 0)
    state_spec = pl.BlockSpec((bt, N_HEADS_R, DK_R, DV_R), lambda b, n: (b, 0, 0, 0))
    in_specs = [
        pl.BlockSpec(memory_space=pltpu.SMEM),
        pl.BlockSpec((bt, c, 512), lambda b, n: (b, n, _blk("q_r", 512))),
        pl.BlockSpec((bt, c, 512), lambda b, n: (b, n, _blk("k_r", 512))),
        pl.BlockSpec((bt, c, 1024), lambda b, n: (b, n, _blk("v_r", 1024))),
        pl.BlockSpec((bt, c, 1024), lambda b, n: (b, n, _blk("g_r", 1024))),
        pl.BlockSpec((N_HEADS_R, c, c), const3),
        pl.BlockSpec((N_HEADS_R, c, DV_R), const3),
        pl.BlockSpec((N_HEADS_R, c, DK_R), const3),
        pl.BlockSpec((N_HEADS_R, DV_R), lambda b, n: (0, 0)),
    ]
    args = [cdec, proj3, proj3, proj3, proj3, intra, cross_b, kdec_b, ln_ret]
    if has_s0:
        in_specs.append(state_spec)
        args.append(s0)
    return pl.pallas_call(
        functools.partial(_ret_kernel, has_s0=has_s0),
        out_shape=(jax.ShapeDtypeStruct((bn, t, N_HEADS_R * DV_R), BF16),
                   jax.ShapeDtypeStruct((bn, N_HEADS_R, DK_R, DV_R), F32)),
        grid=(bn // bt, t // c),
        in_specs=in_specs,
        out_specs=(pl.BlockSpec((bt, c, 1024), lambda b, n: (b, n, 0)), state_spec),
        scratch_shapes=[pltpu.VMEM((bt, N_HEADS_R, DK_R, DV_R), F32)],
        compiler_params=_params(("arbitrary", "arbitrary")),
        name="ret_sample" if has_s0 else "ret_prompt",
    )(*args)


def _merge_kernel(oa_ref, or_ref, ga_ref, gr_ref, wa_ref, wr_ref, m_ref):
    br_a = _dot(oa_ref[...], wa_ref[...])
    br_r = _dot(or_ref[...], wr_ref[...])
    m_ref[...] = (_sigmoid(ga_ref[...]) * br_a + _sigmoid(gr_ref[...]) * br_r).astype(m_ref.dtype)


def _merge(o_a, o_r, proj, w_a, w_r, tm, tn=512):
    n = o_a.shape[0]
    ga, gr = _blk("gate_a", tn), _blk("gate_r", tn)
    return pl.pallas_call(
        _merge_kernel,
        out_shape=jax.ShapeDtypeStruct((n, D_MODEL), BF16),
        grid=(n // tm, D_MODEL // tn),
        in_specs=[
            pl.BlockSpec((tm, 1024), lambda i, j: (i, 0)),
            pl.BlockSpec((tm, 1024), lambda i, j: (i, 0)),
            pl.BlockSpec((tm, tn), lambda i, j: (i, ga + j)),
            pl.BlockSpec((tm, tn), lambda i, j: (i, gr + j)),
            pl.BlockSpec((1024, tn), lambda i, j: (0, j)),
            pl.BlockSpec((1024, tn), lambda i, j: (0, j)),
        ],
        out_specs=pl.BlockSpec((tm, tn), lambda i, j: (i, j)),
        compiler_params=_params(("arbitrary", "arbitrary")),
        name="merge",
    )(o_a, o_r, proj, proj, w_a, w_r)


def _out_proj_kernel(x_ref, m_ref, w_ref, o_ref):
    o_ref[...] = x_ref[...] + _dot(m_ref[...], w_ref[...])


def _out_proj(x2d, m, w_o, tm, tn=512):
    n = x2d.shape[0]
    return pl.pallas_call(
        _out_proj_kernel,
        out_shape=jax.ShapeDtypeStruct((n, D_MODEL), F32),
        grid=(n // tm, D_MODEL // tn),
        in_specs=[
            pl.BlockSpec((tm, tn), lambda i, j: (i, j)),
            pl.BlockSpec((tm, D_MODEL), lambda i, j: (i, 0)),
            pl.BlockSpec((D_MODEL, tn), lambda i, j: (0, j)),
        ],
        out_specs=pl.BlockSpec((tm, tn), lambda i, j: (i, j)),
        compiler_params=_params(("arbitrary", "arbitrary")),
        name="out_proj",
    )(x2d, m, w_o)


def _scores_kernel(x_ref, g_ref, w_ref, sk_ref, h_out_ref, s_ref, h_ref):
    j = pl.program_id(1)

    @pl.when(j == 0)
    def _():
        h = _rms(x_ref[...], g_ref[...])
        h_ref[...] = h.astype(BF16)
        h_out_ref[...] = h

    qry = _dot(h_ref[...], w_ref[...]).astype(BF16)
    for g in range(qry.shape[1] // D_KEY_HALF):
        s_ref[g] = _dot_nt(sk_ref[g % 2], qry[:, g * D_KEY_HALF:(g + 1) * D_KEY_HALF])


def _scores(x2d, ln, w_q, sub_keys, tm, tn=512):
    n = x2d.shape[0]
    ng = tn // D_KEY_HALF
    return pl.pallas_call(
        _scores_kernel,
        out_shape=(jax.ShapeDtypeStruct((n, D_MODEL), F32),
                   jax.ShapeDtypeStruct((2 * PEER_HEADS, N_KEYS, n), F32)),
        grid=(n // tm, D_MODEL // tn),
        in_specs=[
            pl.BlockSpec((tm, D_MODEL), lambda i, j: (i, 0)),
            pl.BlockSpec((1, D_MODEL), lambda i, j: (0, 0)),
            pl.BlockSpec((D_MODEL, tn), lambda i, j: (0, j)),
            pl.BlockSpec((2, N_KEYS, D_KEY_HALF), lambda i, j: (0, 0, 0)),
        ],
        out_specs=(pl.BlockSpec((tm, D_MODEL), lambda i, j: (i, 0)),
                   pl.BlockSpec((ng, N_KEYS, tm), lambda i, j: (j, 0, i))),
        scratch_shapes=[pltpu.VMEM((tm, D_MODEL), BF16)],
        compiler_params=_params(("arbitrary", "arbitrary")),
        name="peer_scores",
    )(x2d, ln, w_q, sub_keys)


def _take_top(vals, iota, count, fill, payload=None):
    n_rows = vals.shape[0]
    top_v, top_i = [], []
    for _ in range(count):
        m = jnp.max(vals, axis=0, keepdims=True)
        pos = jnp.min(jnp.where(vals == m, iota, n_rows), axis=0, keepdims=True)
        sel = iota == pos
        top_v.append(m)
        top_i.append(pos if payload is None else jnp.max(jnp.where(sel, payload, -1), axis=0, keepdims=True))
        vals = jnp.where(sel, fill, vals)
    return jnp.concatenate(top_v, axis=0), jnp.concatenate(top_i, axis=0)


def _topk_kernel(s_ref, e_ref, g_ref):
    tt = s_ref.shape[2]
    key_iota = lax.broadcasted_iota(jnp.int32, (N_KEYS, tt), 0)
    comb_iota = lax.broadcasted_iota(jnp.int32, (PEER_TOPK * PEER_TOPK, tt), 0)
    ninf = float("-inf")

    def head(h, carry):
        s0, i0 = _take_top(s_ref[2 * h], key_iota, PEER_TOPK, ninf)
        s1, i1 = _take_top(s_ref[2 * h + 1], key_iota, PEER_TOPK, ninf)
        comb = jnp.concatenate([s0[i:i + 1, :] + s1 for i in range(PEER_TOPK)], axis=0)
        cidx = jnp.concatenate([i0[i:i + 1, :] * N_KEYS + i1 for i in range(PEER_TOPK)], axis=0)
        best, eidx = _take_top(comb, comb_iota, PEER_TOPK, ninf, payload=cidx)
        ex = jnp.exp(best - jnp.max(best, axis=0, keepdims=True))
        g_ref[h] = ex / jnp.sum(ex, axis=0, keepdims=True)
        e_ref[h] = eidx
        return carry

    lax.fori_loop(0, PEER_HEADS, head, 0)


def _topk(scores_t, tt=256):
    n = scores_t.shape[2]
    out_spec = pl.BlockSpec((PEER_HEADS, PEER_TOPK, tt), lambda i: (0, 0, i))
    return pl.pallas_call(
        _topk_kernel,
        out_shape=(jax.ShapeDtypeStruct((PEER_HEADS, PEER_TOPK, n), jnp.int32),
                   jax.ShapeDtypeStruct((PEER_HEADS, PEER_TOPK, n), F32)),
        grid=(n // tt,),
        in_specs=[pl.BlockSpec((2 * PEER_HEADS, N_KEYS, tt), lambda i: (0, 0, i))],
        out_specs=(out_spec, out_spec),
        compiler_params=_params(("arbitrary",)),
        name="peer_topk",
    )(scores_t)


def _gelu(a):
    return 0.5 * a * (1.0 + lax.erf(a * (2.0 ** -0.5)))


def _peer_kernel(idx_hbm, h_ref, g2_ref, x_ref, uv_hbm, o_ref, idx_smem, buf, sem_i, sem):
    tt = h_ref.shape[0]
    half = D_MODEL // 2
    step = pl.program_id(0)
    more = step + 1 < pl.num_programs(0)
    cur = step % 2

    def idx_copy(s, islot):
        return pltpu.make_async_copy(idx_hbm.at[pl.ds(s * tt, tt), :], idx_smem.at[islot], sem_i.at[islot])

    def gather_start(islot, t, slot):
        for k in range(N_SEL):
            e = idx_smem[islot, t, k]
            pltpu.make_async_copy(uv_hbm.at[pl.ds(e, 1), :], buf.at[slot, pl.ds(k, 1), :],
                                  sem.at[slot]).start(priority=k % 2)

    def gather_wait(slot):
        pltpu.make_async_copy(uv_hbm.at[pl.ds(0, N_SEL), :], buf.at[slot], sem.at[slot]).wait()

    @pl.when(step == 0)
    def _():
        first = idx_copy(0, 0)
        first.start()
        first.wait()
        gather_start(0, 0, 0)

    @pl.when(more)
    def _():
        idx_copy(step + 1, 1 - cur).start()

    row_iota = lax.broadcasted_iota(jnp.int32, (SUBLANES, 1), 0)
    even = lax.broadcasted_iota(jnp.int32, (SUBLANES, 2 * N_SEL), 1) % 2 == 0

    def group(gi, carry):
        r0 = pl.multiple_of(gi * SUBLANES, SUBLANES)
        h8 = h_ref[pl.ds(r0, SUBLANES), :]
        g2 = g2_ref[pl.ds(r0, SUBLANES), :]

        def token(r, acc):
            t = r0 + r
            slot = t % 2
            wrap = t + 1 == tt

            @pl.when(jnp.logical_and(wrap, more))
            def _():
                idx_copy(step + 1, 1 - cur).wait()

            @pl.when(jnp.logical_or(jnp.logical_not(wrap), more))
            def _():
                gather_start(jnp.where(wrap, 1 - cur, cur), jnp.where(wrap, 0, t + 1), 1 - slot)

            gather_wait(slot)
            sel = row_iota == r
            hm = jnp.concatenate([jnp.where(sel, h8[:, :half], 0.0), jnp.where(sel, h8[:, half:], 0.0)],
                                 axis=0).astype(BF16)
            rows = buf[slot]
            p = _dot_nt(hm, pltpu.bitcast(rows[:, :half], BF16))
            part = jnp.where(even, p[:SUBLANES], p[SUBLANES:])
            a2 = part + jnp.where(even, pltpu.roll(part, 2 * N_SEL - 1, 1), pltpu.roll(part, 1, 1))
            w2 = _gelu(a2) * g2
            wm = jnp.concatenate([jnp.where(even, w2, 0.0), jnp.where(even, 0.0, w2)], axis=0).astype(BF16)
            return acc + _dot(wm, pltpu.bitcast(rows[:, half:], BF16))

        acc = lax.fori_loop(0, SUBLANES, token, jnp.zeros((2 * SUBLANES, half), F32))
        o_ref[pl.ds(r0, SUBLANES), :] = (x_ref[pl.ds(r0, SUBLANES), :]
                                         + jnp.concatenate([acc[:SUBLANES], acc[SUBLANES:]], axis=1))
        return carry

    lax.fori_loop(0, tt // SUBLANES, group, 0)


def _pack_rows(tab):
    bits = lax.bitcast_convert_type(tab.astype(BF16), jnp.uint16).astype(jnp.uint32)
    half = tab.shape[1] // 2
    return bits[:, :half] | (bits[:, half:] << 16)


def _peer(eidx, h2, gate2, x2d, uv, tt):
    n = x2d.shape[0]
    assert tt % 2 == 0 and tt % SUBLANES == 0
    row = lambda i: (i, 0)
    return pl.pallas_call(
        _peer_kernel,
        out_shape=jax.ShapeDtypeStruct((n, D_MODEL), F32),
        grid=(n // tt,),
        in_specs=[
            pl.BlockSpec(memory_space=pl.ANY),
            pl.BlockSpec((tt, D_MODEL), row),
            pl.BlockSpec((tt, 2 * N_SEL), row),
            pl.BlockSpec((tt, D_MODEL), row),
            pl.BlockSpec(memory_space=pl.ANY),
        ],
        out_specs=pl.BlockSpec((tt, D_MODEL), row),
        scratch_shapes=[
            pltpu.SMEM((2, tt, N_SEL), jnp.int32),
            pltpu.VMEM((2, N_SEL, D_MODEL), jnp.uint32),
            pltpu.SemaphoreType.DMA((2,)),
            pltpu.SemaphoreType.DMA((2,)),
        ],
        compiler_params=_params(("arbitrary",)),
        name="peer_mix",
    )(eidx, h2, gate2, x2d, uv)


def _ple_kernel(x_ref, p_ref, lnp_ref, wg_ref, wp_ref, lnf_ref, y_ref):
    x = x_ref[...]
    gate = _sigmoid(_dot(_rms(x, lnp_ref[...]).astype(BF16), wg_ref[...]))
    x = x + gate * _dot(p_ref[...].astype(BF16), wp_ref[...])
    y_ref[...] = _rms(x, lnf_ref[...])


def _ple(x2d, p2d, ln_ple, w_gate, w_proj, ln_final, tm):
    n = x2d.shape[0]
    pd = p2d.shape[1]
    row = lambda i: (i, 0)
    const = lambda i: (0, 0)
    return pl.pallas_call(
        _ple_kernel,
        out_shape=jax.ShapeDtypeStruct((n, D_MODEL), F32),
        grid=(n // tm,),
        in_specs=[
            pl.BlockSpec((tm, D_MODEL), row),
            pl.BlockSpec((tm, pd), row),
            pl.BlockSpec((1, D_MODEL), const),
            pl.BlockSpec((D_MODEL, D_MODEL), const),
            pl.BlockSpec((pd, D_MODEL), const),
            pl.BlockSpec((1, D_MODEL), const),
        ],
        out_specs=pl.BlockSpec((tm, D_MODEL), row),
        compiler_params=_params(("arbitrary",)),
        name="ple_final",
    )(x2d, p2d, ln_ple, w_gate, w_proj, ln_final)


def _rope_tables(pos):
    inv = ROPE_THETA ** (-jnp.arange(0, HEAD_DIM_A, 2, dtype=F32) / HEAD_DIM_A)
    ang = pos.astype(F32)[:, None] * inv[None, :]
    cos, sin = jnp.cos(ang), jnp.sin(ang)
    reps = PROJ_TILE // HEAD_DIM_A
    return (jnp.tile(jnp.concatenate([cos, cos], axis=1), (1, reps)),
            jnp.tile(jnp.concatenate([-sin, sin], axis=1), (1, reps)))


def _pick(n, prefs):
    for c in prefs:
        if n % c == 0:
            return c
    raise ValueError(f"no tile of {prefs} divides {n}")


def _group(x, pe, pos0, cache_k, cache_v, s0, wts):
    bn, t, _ = x.shape
    n = bn * t
    prompt = cache_k is None
    x2d = x.reshape(n, D_MODEL)
    tm = _pick(n, (1024, 512, 256, 128))

    pos = pos0 + jnp.arange(t, dtype=jnp.int32)
    cos, sin = _rope_tables(pos)
    if t < tm:
        cos, sin = jnp.tile(cos, (tm // t, 1)), jnp.tile(sin, (tm // t, 1))
    proj = _in_proj(x2d, wts["ln1"], wts["w_in"], cos, sin, wts["colscale"], tm)
    proj3 = proj.reshape(bn, t, IN_COLS)

    if prompt:
        c_att, bt_att = ATTN_BLOCK, 1
        ctx_k = ctx_v = proj3
    else:
        c_att, bt_att = t, _pick(bn, (8, 4, 2, 1))
        ctx_k = cache_k.reshape(bn, WINDOW, N_KV_A * HEAD_DIM_A)
        ctx_v = cache_v.reshape(bn, WINDOW, N_KV_A * HEAD_DIM_A)
    o_a = _attention(wts["sinks"], proj3, ctx_k, ctx_v, bt_att, c_att, prompt)

    c_ret = math.gcd(t, RET_CHUNK)
    bt_ret = 1 if prompt else _pick(bn, (8, 4, 2, 1))
    o_r, s_new = _retention(proj3, s0, wts["ln_ret"], bt_ret, c_ret)

    m = _merge(o_a.reshape(n, -1), o_r.reshape(n, -1), proj, wts["w_br_a"], wts["w_br_r"], tm)
    x2 = _out_proj(x2d, m, wts["w_o"], tm)

    h2, scores_t = _scores(x2, wts["ln2"], wts["w_q"], wts["sub_keys"], _pick(n, (512, 256, 128)))
    eidx_t, gate_t = _topk(scores_t, _pick(n, (256, 128)))
    eidx = eidx_t.reshape(N_SEL, n).T
    gate2 = jnp.repeat(gate_t.reshape(N_SEL, n).T, 2, axis=1)
    x3 = _peer(eidx, h2, gate2, x2, wts["peer_uv"], _pick(n, (64, 32, 16, 8)))

    y = _ple(x3, pe.reshape(n, -1), wts["ln_ple"], wts["w_ple_gate"], wts["w_ple_proj"], wts["ln_final"],
             _pick(n, (256, 128)))

    k_off, v_off = COL["k_a"][0], COL["v_a"][0]
    kv_w = N_KV_A * HEAD_DIM_A
    k_new, v_new = proj3[:, :, k_off:k_off + kv_w], proj3[:, :, v_off:v_off + kv_w]
    if prompt:
        k_win, v_win = k_new[:, -WINDOW:], v_new[:, -WINDOW:]
    else:
        k_win = jnp.concatenate([ctx_k, k_new], axis=1)[:, -WINDOW:]
        v_win = jnp.concatenate([ctx_v, v_new], axis=1)[:, -WINDOW:]
    shp = (bn, WINDOW, N_KV_A, HEAD_DIM_A)
    return y.reshape(bn, t, D_MODEL), k_win.reshape(shp), v_win.reshape(shp), s_new


def kernel(x_prompt, x_sample, cache_k_win, cache_v_win, state_ret, p_prompt, p_sample, ln1, w_in, attn_sinks, ln_ret, w_branch_attn, w_branch_ret, w_out, ln2, w_peer_query, peer_sub_keys, peer_u, peer_v, ln_ple, w_ple_gate, w_ple_proj, ln_final):
    depth = ln1.shape[0]
    assert depth == 1, "single-layer step"
    i = 0
    colscale = jnp.ones((IN_COLS,), F32).at[COL["k_r"][0]:COL["k_r"][0] + COL["k_r"][1]].set(DK_R ** -0.5)
    wts = dict(
        ln1=ln1[i][None, :],
        w_in=jnp.concatenate([w_in[i][:, _REF_COL[f][0]:_REF_COL[f][0] + _REF_COL[f][1]] for f in _NEW_ORDER],
                             axis=1).astype(BF16),
        colscale=colscale[None, :],
        sinks=attn_sinks[i],
        ln_ret=ln_ret[i].reshape(N_HEADS_R, DV_R),
        w_br_a=w_branch_attn[i].astype(BF16),
        w_br_r=w_branch_ret[i].astype(BF16),
        w_o=w_out[i].astype(BF16),
        ln2=ln2[i][None, :],
        w_q=w_peer_query[i].astype(BF16),
        sub_keys=peer_sub_keys[i].astype(BF16),
        peer_uv=jnp.concatenate([_pack_rows(peer_u[i]), _pack_rows(peer_v[i])], axis=1),
        ln_ple=ln_ple[i][None, :],
        w_ple_gate=w_ple_gate[i].astype(BF16),
        w_ple_proj=w_ple_proj[i].astype(BF16),
        ln_final=ln_final[None, :],
    )
    yp, kp, vp, sp = _group(x_prompt, p_prompt[i], 0, None, None, None, wts)
    ys, ks, vs, ss = _group(x_sample, p_sample[i], PAST_LEN, cache_k_win[i], cache_v_win[i], state_ret[i], wts)
    return (yp, ys, kp[None], vp[None], sp[None], ks[None], vs[None], ss[None])
```

```python
import functools
import math

import jax
import jax.numpy as jnp
from jax import lax
from jax.experimental import pallas as pl
from jax.experimental.pallas import tpu as pltpu

F32 = jnp.float32
BF16 = jnp.bfloat16

D_MODEL = 2048
PAST_LEN = 16384
HEAD_DIM_A = 64
N_HEADS_A = 16
N_KV_A = 4
GROUP_A = 4
WINDOW = 128
ATTN_BLOCK = 128
N_HEADS_R = 8
DK_R = 64
DV_R = 128
RET_CHUNK = 128
PEER_HEADS = 8
N_KEYS = 128
PEER_TOPK = 16
D_KEY_HALF = 128
N_SEL = PEER_HEADS * PEER_TOPK
PEER_RING = 4
PEER_CHUNKS = 4
ROPE_THETA = 10000.0
EPS = 1e-6
NEG_INF = -1e30

LANES = 128
SUBLANES = 8
VMEM_LIMIT_BYTES = 56 * 1024 * 1024

_REF_FIELDS = (("q_a", 1024), ("k_a", 256), ("v_a", 256), ("q_r", 512), ("k_r", 512),
               ("v_r", 1024), ("g_r", 1024), ("gate_a", 2048), ("gate_r", 2048))
_NEW_ORDER = ("gate_a", "gate_r", "q_a", "v_r", "g_r", "q_r", "k_r", "k_a", "v_a")
_ROPE_FIELDS = ("q_a", "q_r", "k_r", "k_a")
PROJ_TILE = 256


def _layout():
    ref_off, o = {}, 0
    for name, w in _REF_FIELDS:
        ref_off[name] = (o, w)
        o += w
    new_off, o = {}, 0
    for name in _NEW_ORDER:
        new_off[name] = (o, ref_off[name][1])
        o += ref_off[name][1]
    return new_off, ref_off, o


COL, _REF_COL, IN_COLS = _layout()


def _blk(name, width):
    off, w = COL[name]
    assert off % width == 0 and w % width == 0
    return off // width


def _params(sem, vmem=VMEM_LIMIT_BYTES):
    return pltpu.CompilerParams(dimension_semantics=sem, vmem_limit_bytes=vmem)


def _rms(x, g):
    return x * lax.rsqrt(jnp.mean(x * x, axis=-1, keepdims=True) + EPS) * g


def _sigmoid(x):
    return 1.0 / (1.0 + jnp.exp(-x))


def _dot(a, b):
    return jnp.dot(a, b, preferred_element_type=F32)


def _dot_nt(a, b):
    return lax.dot_general(a, b, (((1,), (1,)), ((), ())), preferred_element_type=F32)


def _dot_tn(a, b):
    return lax.dot_general(a, b, (((0,), (0,)), ((), ())), preferred_element_type=F32)


def _in_proj_kernel(x_ref, g_ref, w_ref, cos_ref, sin_ref, cs_ref, o_ref, h_ref, *, rope_ranges):
    j = pl.program_id(1)

    @pl.when(j == 0)
    def _():
        h_ref[...] = _rms(x_ref[...], g_ref[...]).astype(BF16)

    acc = _dot(h_ref[...], w_ref[...])
    is_rope = functools.reduce(jnp.logical_or, [(j >= a) & (j < b) for a, b in rope_ranges])

    @pl.when(is_rope)
    def _():
        tn = acc.shape[1]
        lane = lax.broadcasted_iota(jnp.int32, acc.shape, 1)
        first_half = (lane % HEAD_DIM_A) < (HEAD_DIM_A // 2)
        partner = jnp.where(first_half, pltpu.roll(acc, tn - HEAD_DIM_A // 2, 1), pltpu.roll(acc, HEAD_DIM_A // 2, 1))
        o_ref[...] = (acc * cos_ref[...] + partner * sin_ref[...]) * cs_ref[...]

    @pl.when(jnp.logical_not(is_rope))
    def _():
        o_ref[...] = acc


def _in_proj(x2d, ln, w, cos, sin, colscale, tm):
    n = x2d.shape[0]
    rt = cos.shape[0] // tm
    rope_ranges = tuple((COL[f][0] // PROJ_TILE, (COL[f][0] + COL[f][1]) // PROJ_TILE) for f in _ROPE_FIELDS)
    return pl.pallas_call(
        functools.partial(_in_proj_kernel, rope_ranges=rope_ranges),
        out_shape=jax.ShapeDtypeStruct((n, IN_COLS), F32),
        grid=(n // tm, IN_COLS // PROJ_TILE),
        in_specs=[
            pl.BlockSpec((tm, D_MODEL), lambda i, j: (i, 0)),
            pl.BlockSpec((1, D_MODEL), lambda i, j: (0, 0)),
            pl.BlockSpec((D_MODEL, PROJ_TILE), lambda i, j: (0, j)),
            pl.BlockSpec((tm, PROJ_TILE), lambda i, j: (i % rt, 0)),
            pl.BlockSpec((tm, PROJ_TILE), lambda i, j: (i % rt, 0)),
            pl.BlockSpec((1, PROJ_TILE), lambda i, j: (0, j)),
        ],
        out_specs=pl.BlockSpec((tm, PROJ_TILE), lambda i, j: (i, j)),
        scratch_shapes=[pltpu.VMEM((tm, D_MODEL), BF16)],
        compiler_params=_params(("arbitrary", "arbitrary")),
        name="in_proj",
    )(x2d, ln, w, cos, sin, colscale)


def _attn_kernel(sink_ref, q_ref, kc_ref, vc_ref, kx_ref, vx_ref, o_ref, *, first_ctx_invalid):
    bt, c, _ = q_ref.shape
    rows = GROUP_A * c
    qi = lax.broadcasted_iota(jnp.int32, (rows, WINDOW), 0) % c
    kj = lax.broadcasted_iota(jnp.int32, (rows, WINDOW), 1)
    ctx_ok = kj >= qi
    if first_ctx_invalid:
        ctx_ok = jnp.logical_and(ctx_ok, pl.program_id(1) > 0)
    qi_c = lax.broadcasted_iota(jnp.int32, (rows, c), 0) % c
    kj_c = lax.broadcasted_iota(jnp.int32, (rows, c), 1)
    cur_ok = kj_c <= qi_c
    grp = lax.broadcasted_iota(jnp.int32, (rows, 1), 0) // c
    scale = HEAD_DIM_A ** -0.5

    def one(b):
        q = q_ref[b]
        kc, vc, kx, vx = kc_ref[b], vc_ref[b], kx_ref[b], vx_ref[b]
        outs = [None] * N_HEADS_A
        for kh in range(N_KV_A):
            sl = slice(kh * HEAD_DIM_A, (kh + 1) * HEAD_DIM_A)
            q4 = jnp.concatenate(
                [q[:, (kh * GROUP_A + g) * HEAD_DIM_A:(kh * GROUP_A + g + 1) * HEAD_DIM_A] for g in range(GROUP_A)],
                axis=0).astype(BF16)
            s_x = jnp.where(ctx_ok, _dot_nt(q4, kx[:, sl].astype(BF16)) * scale, NEG_INF)
            s_c = jnp.where(cur_ok, _dot_nt(q4, kc[:, sl].astype(BF16)) * scale, NEG_INF)
            sink = jnp.zeros((rows, 1), F32)
            for g in range(GROUP_A):
                sink = jnp.where(grp == g, sink_ref[kh * GROUP_A + g], sink)
            m = jnp.maximum(jnp.maximum(jnp.max(s_x, axis=-1, keepdims=True), jnp.max(s_c, axis=-1, keepdims=True)), sink)
            e_x = jnp.exp(s_x - m)
            e_c = jnp.exp(s_c - m)
            den = jnp.sum(e_x, axis=-1, keepdims=True) + jnp.sum(e_c, axis=-1, keepdims=True) + jnp.exp(sink - m)
            o4 = _dot((e_x / den).astype(BF16), vx[:, sl].astype(BF16)) + _dot((e_c / den).astype(BF16), vc[:, sl].astype(BF16))
            for g in range(GROUP_A):
                outs[kh * GROUP_A + g] = o4[g * c:(g + 1) * c]
        o_ref[b] = jnp.concatenate(outs, axis=1).astype(o_ref.dtype)

    if bt == 1:
        one(0)
    else:
        def body(b, carry):
            one(b)
            return carry
        lax.fori_loop(0, bt, body, 0)


def _attention(sinks, proj3, ctx_k, ctx_v, bt, c, prompt):
    bn, t, _ = proj3.shape
    qb, kb, vb = _blk("q_a", 1024), _blk("k_a", 256), _blk("v_a", 256)
    if prompt:
        ctx_specs = [pl.BlockSpec((bt, WINDOW, 256), lambda b, n: (b, jnp.maximum(n - 1, 0), kb)),
                     pl.BlockSpec((bt, WINDOW, 256), lambda b, n: (b, jnp.maximum(n - 1, 0), vb))]
    else:
        ctx_specs = [pl.BlockSpec((bt, WINDOW, 256), lambda b, n: (b, 0, 0)),
                     pl.BlockSpec((bt, WINDOW, 256), lambda b, n: (b, 0, 0))]
    return pl.pallas_call(
        functools.partial(_attn_kernel, first_ctx_invalid=prompt),
        out_shape=jax.ShapeDtypeStruct((bn, t, N_HEADS_A * HEAD_DIM_A), BF16),
        grid=(bn // bt, t // c),
        in_specs=[
            pl.BlockSpec(memory_space=pltpu.SMEM),
            pl.BlockSpec((bt, c, 1024), lambda b, n: (b, n, qb)),
            pl.BlockSpec((bt, c, 256), lambda b, n: (b, n, kb)),
            pl.BlockSpec((bt, c, 256), lambda b, n: (b, n, vb)),
        ] + ctx_specs,
        out_specs=pl.BlockSpec((bt, c, 1024), lambda b, n: (b, n, 0)),
        compiler_params=_params(("arbitrary", "arbitrary")),
        name="attn_prompt" if prompt else "attn_sample",
    )(sinks, proj3, proj3, proj3, ctx_k, ctx_v)


def _ret_kernel(cdec_ref, q_ref, k_ref, v_ref, g_ref, intra_ref, cross_ref, kdec_ref, ln_ref, *rest, has_s0):
    if has_s0:
        s0_ref, o_ref, sout_ref, s_ref = rest
    else:
        o_ref, sout_ref, s_ref = rest
    bt = q_ref.shape[0]
    ci = pl.program_id(1)

    @pl.when(ci == 0)
    def _():
        s_ref[...] = s0_ref[...] if has_s0 else jnp.zeros(s_ref.shape, F32)

    def one(b):
        q, k, v, gt = q_ref[b], k_ref[b], v_ref[b], g_ref[b]
        outs = []
        for h in range(N_HEADS_R):
            qh = q[:, h * DK_R:(h + 1) * DK_R].astype(BF16)
            kh = k[:, h * DK_R:(h + 1) * DK_R]
            vh = v[:, h * DV_R:(h + 1) * DV_R].astype(BF16)
            s = s_ref[b, h]
            att = _dot_nt(qh, kh.astype(BF16)) * intra_ref[h]
            o = _dot(att.astype(BF16), vh) + _dot(qh, s.astype(BF16)) * cross_ref[h]
            s_ref[b, h] = s * cdec_ref[h] + _dot_tn((kh * kdec_ref[h]).astype(BF16), vh)
            mu = jnp.mean(o, axis=-1, keepdims=True)
            d = o - mu
            var = jnp.mean(d * d, axis=-1, keepdims=True)
            gh = gt[:, h * DV_R:(h + 1) * DV_R]
            outs.append(d * lax.rsqrt(var + EPS) * ln_ref[h:h + 1, :] * (gh * _sigmoid(gh)))
        o_ref[b] = jnp.concatenate(outs, axis=1).astype(o_ref.dtype)

    if bt == 1:
        one(0)
    else:
        def body(b, carry):
            one(b)
            return carry
        lax.fori_loop(0, bt, body, 0)

    @pl.when(ci == pl.num_programs(1) - 1)
    def _():
        sout_ref[...] = s_ref[...]


def _retention(proj3, s0, ln_ret, bt, c):
    bn, t, _ = proj3.shape
    log_g = jnp.log1p(-jnp.exp2(-5.0 - jnp.arange(N_HEADS_R, dtype=F32)))
    i = jnp.arange(c, dtype=F32)
    diff = i[:, None] - i[None, :]
    intra = jnp.where(diff[None] >= 0, jnp.exp(jnp.maximum(diff, 0.0)[None] * log_g[:, None, None]), 0.0)
    cross = jnp.exp((i + 1.0)[None, :] * log_g[:, None])
    kdec = jnp.exp((c - 1.0 - i)[None, :] * log_g[:, None])
    cdec = jnp.exp(c * log_g)
    cross_b = jnp.broadcast_to(cross[:, :, None], (N_HEADS_R, c, DV_R))
    kdec_b = jnp.broadcast_to(kdec[:, :, None], (N_HEADS_R, c, DK_R))
    has_s0 = s0 is not None
    const3 = lambda b, n: (0, 0, 0)
    state_spec = pl.BlockSpec((bt, N_HEADS_R, DK_R, DV_R), lambda b, n: (b, 0, 0, 0))
    in_specs = [
        pl.BlockSpec(memory_space=pltpu.SMEM),
        pl.BlockSpec((bt, c, 512), lambda b, n: (b, n, _blk("q_r", 512))),
        pl.BlockSpec((bt, c, 512), lambda b, n: (b, n, _blk("k_r", 512))),
        pl.BlockSpec((bt, c, 1024), lambda b, n: (b, n, _blk("v_r", 1024))),
        pl.BlockSpec((bt, c, 1024), lambda b, n: (b, n, _blk("g_r", 1024))),
        pl.BlockSpec((N_HEADS_R, c, c), const3),
        pl.BlockSpec((N_HEADS_R, c, DV_R), const3),
        pl.BlockSpec((N_HEADS_R, c, DK_R), const3),
        pl.BlockSpec((N_HEADS_R, DV_R), lambda b, n: (0, 0)),
    ]
    args = [cdec, proj3, proj3, proj3, proj3, intra, cross_b, kdec_b, ln_ret]
    if has_s0:
        in_specs.append(state_spec)
        args.append(s0)
    return pl.pallas_call(
        functools.partial(_ret_kernel, has_s0=has_s0),
        out_shape=(jax.ShapeDtypeStruct((bn, t, N_HEADS_R * DV_R), BF16),
                   jax.ShapeDtypeStruct((bn, N_HEADS_R, DK_R, DV_R), F32)),
        grid=(bn // bt, t // c),
        in_specs=in_specs,
        out_specs=(pl.BlockSpec((bt, c, 1024), lambda b, n: (b, n, 0)), state_spec),
        scratch_shapes=[pltpu.VMEM((bt, N_HEADS_R, DK_R, DV_R), F32)],
        compiler_params=_params(("arbitrary", "arbitrary")),
        name="ret_sample" if has_s0 else "ret_prompt",
    )(*args)


def _merge_kernel(oa_ref, or_ref, ga_ref, gr_ref, wa_ref, wr_ref, m_ref):
    br_a = _dot(oa_ref[...], wa_ref[...])
    br_r = _dot(or_ref[...], wr_ref[...])
    m_ref[...] = (_sigmoid(ga_ref[...]) * br_a + _sigmoid(gr_ref[...]) * br_r).astype(m_ref.dtype)


def _merge(o_a, o_r, proj, w_a, w_r, tm, tn=512):
    n = o_a.shape[0]
    ga, gr = _blk("gate_a", tn), _blk("gate_r", tn)
    return pl.pallas_call(
        _merge_kernel,
        out_shape=jax.ShapeDtypeStruct((n, D_MODEL), BF16),
        grid=(n // tm, D_MODEL // tn),
        in_specs=[
            pl.BlockSpec((tm, 1024), lambda i, j: (i, 0)),
            pl.BlockSpec((tm, 1024), lambda i, j: (i, 0)),
            pl.BlockSpec((tm, tn), lambda i, j: (i, ga + j)),
            pl.BlockSpec((tm, tn), lambda i, j: (i, gr + j)),
            pl.BlockSpec((1024, tn), lambda i, j: (0, j)),
            pl.BlockSpec((1024, tn), lambda i, j: (0, j)),
        ],
        out_specs=pl.BlockSpec((tm, tn), lambda i, j: (i, j)),
        compiler_params=_params(("arbitrary", "arbitrary")),
        name="merge",
    )(o_a, o_r, proj, proj, w_a, w_r)


def _out_proj_kernel(x_ref, m_ref, w_ref, o_ref):
    o_ref[...] = x_ref[...] + _dot(m_ref[...], w_ref[...])


def _out_proj(x2d, m, w_o, tm, tn=512):
    n = x2d.shape[0]
    return pl.pallas_call(
        _out_proj_kernel,
        out_shape=jax.ShapeDtypeStruct((n, D_MODEL), F32),
        grid=(n // tm, D_MODEL // tn),
        in_specs=[
            pl.BlockSpec((tm, tn), lambda i, j: (i, j)),
            pl.BlockSpec((tm, D_MODEL), lambda i, j: (i, 0)),
            pl.BlockSpec((D_MODEL, tn), lambda i, j: (0, j)),
        ],
        out_specs=pl.BlockSpec((tm, tn), lambda i, j: (i, j)),
        compiler_params=_params(("arbitrary", "arbitrary")),
        name="out_proj",
    )(x2d, m, w_o)


def _scores_kernel(x_ref, g_ref, w_ref, sk_ref, h_out_ref, s_ref, h_ref):
    j = pl.program_id(1)

    @pl.when(j == 0)
    def _():
        h = _rms(x_ref[...], g_ref[...])
        h_ref[...] = h.astype(BF16)
        h_out_ref[...] = h

    qry = _dot(h_ref[...], w_ref[...]).astype(BF16)
    for g in range(qry.shape[1] // D_KEY_HALF):
        s_ref[g] = _dot_nt(sk_ref[g % 2], qry[:, g * D_KEY_HALF:(g + 1) * D_KEY_HALF])


def _scores(x2d, ln, w_q, sub_keys, tm, tn=512):
    n = x2d.shape[0]
    ng = tn // D_KEY_HALF
    return pl.pallas_call(
        _scores_kernel,
        out_shape=(jax.ShapeDtypeStruct((n, D_MODEL), F32),
                   jax.ShapeDtypeStruct((2 * PEER_HEADS, N_KEYS, n), F32)),
        grid=(n // tm, D_MODEL // tn),
        in_specs=[
            pl.BlockSpec((tm, D_MODEL), lambda i, j: (i, 0)),
            pl.BlockSpec((1, D_MODEL), lambda i, j: (0, 0)),
            pl.BlockSpec((D_MODEL, tn), lambda i, j: (0, j)),
            pl.BlockSpec((2, N_KEYS, D_KEY_HALF), lambda i, j: (0, 0, 0)),
        ],
        out_specs=(pl.BlockSpec((tm, D_MODEL), lambda i, j: (i, 0)),
                   pl.BlockSpec((ng, N_KEYS, tm), lambda i, j: (j, 0, i))),
        scratch_shapes=[pltpu.VMEM((tm, D_MODEL), BF16)],
        compiler_params=_params(("arbitrary", "arbitrary")),
        name="peer_scores",
    )(x2d, ln, w_q, sub_keys)


def _take_top(vals, iota, count, fill, payload=None):
    n_rows = vals.shape[0]
    top_v, top_i = [], []
    for _ in range(count):
        m = jnp.max(vals, axis=0, keepdims=True)
        pos = jnp.min(jnp.where(vals == m, iota, n_rows), axis=0, keepdims=True)
        sel = iota == pos
        top_v.append(m)
        top_i.append(pos if payload is None else jnp.max(jnp.where(sel, payload, -1), axis=0, keepdims=True))
        vals = jnp.where(sel, fill, vals)
    return jnp.concatenate(top_v, axis=0), jnp.concatenate(top_i, axis=0)


_COMB_PIECES = tuple((i, PEER_TOPK // (i + 1)) for i in range(SUBLANES))
_COMB_ROWS = sum(max(nj, SUBLANES) for _, nj in _COMB_PIECES) + SUBLANES


def _topk_kernel(s_ref, e_ref, g_ref):
    tt = s_ref.shape[2]
    key_iota = lax.broadcasted_iota(jnp.int32, (N_KEYS, tt), 0)
    comb_iota = lax.broadcasted_iota(jnp.int32, (_COMB_ROWS, tt), 0)
    ninf = float("-inf")

    def head(h, carry):
        s0, i0 = _take_top(s_ref[2 * h], key_iota, PEER_TOPK, ninf)
        s1, i1 = _take_top(s_ref[2 * h + 1], key_iota, PEER_TOPK, ninf)
        comb, cidx = [], []
        for i, nj in _COMB_PIECES:
            rows = max(nj, SUBLANES)
            sv = s0[i:i + 1, :] + s1[:rows]
            iv = i0[i:i + 1, :] * N_KEYS + i1[:rows]
            if nj < rows:
                keep = lax.broadcasted_iota(jnp.int32, (rows, tt), 0) < nj
                sv, iv = jnp.where(keep, sv, ninf), jnp.where(keep, iv, -1)
            comb.append(sv)
            cidx.append(iv)
        comb.append(s0[SUBLANES:] + s1[0:1, :])
        cidx.append(i0[SUBLANES:] * N_KEYS + i1[0:1, :])
        comb, cidx = jnp.concatenate(comb, axis=0), jnp.concatenate(cidx, axis=0)
        best, eidx = _take_top(comb, comb_iota, PEER_TOPK, ninf, payload=cidx)
        ex = jnp.exp(best - jnp.max(best, axis=0, keepdims=True))
        g_ref[h] = ex / jnp.sum(ex, axis=0, keepdims=True)
        e_ref[h] = eidx
        return carry

    lax.fori_loop(0, PEER_HEADS, head, 0)


def _topk(scores_t, tt=256):
    n = scores_t.shape[2]
    out_spec = pl.BlockSpec((PEER_HEADS, PEER_TOPK, tt), lambda i: (0, 0, i))
    return pl.pallas_call(
        _topk_kernel,
        out_shape=(jax.ShapeDtypeStruct((PEER_HEADS, PEER_TOPK, n), jnp.int32),
                   jax.ShapeDtypeStruct((PEER_HEADS, PEER_TOPK, n), F32)),
        grid=(n // tt,),
        in_specs=[pl.BlockSpec((2 * PEER_HEADS, N_KEYS, tt), lambda i: (0, 0, i))],
        out_specs=(out_spec, out_spec),
        compiler_params=_params(("arbitrary",)),
        name="peer_topk",
    )(scores_t)


def _gelu(a):
    return 0.5 * a * (1.0 + lax.erf(a * (2.0 ** -0.5)))


def _peer_kernel(idx_hbm, h_ref, g2_ref, x_ref, uv_hbm, o_ref, idx_smem, buf, sem_i, sem):
    tt = h_ref.shape[0]
    half = D_MODEL // 2
    step = pl.program_id(0)
    more = step + 1 < pl.num_programs(0)
    cur = step % 2

    def idx_copy(s, islot):
        return pltpu.make_async_copy(idx_hbm.at[pl.ds(s * tt, tt), :], idx_smem.at[islot], sem_i.at[islot])

    ahead = PEER_RING - 1
    n_groups = tt // SUBLANES
    n_parts = 2 * PEER_CHUNKS
    cw = half // PEER_CHUNKS

    def gather_start(islot, t, slot, part=None):
        ks = range(N_SEL) if part is None else range(part * N_SEL // n_parts, (part + 1) * N_SEL // n_parts)
        for k in ks:
            e = idx_smem[islot, t, k]
            pltpu.make_async_copy(uv_hbm.at[pl.ds(e, 1), :], buf.at[slot, pl.ds(k, 1), :],
                                  sem.at[slot]).start()

    def gather_wait(slot):
        pltpu.make_async_copy(uv_hbm.at[pl.ds(0, N_SEL), :], buf.at[slot], sem.at[slot]).wait()

    @pl.when(step == 0)
    def _():
        first = idx_copy(0, 0)
        first.start()
        first.wait()
        for j in range(ahead):
            gather_start(0, j, j)

    @pl.when(more)
    def _():
        idx_copy(step + 1, 1 - cur).start()

    row_iota = lax.broadcasted_iota(jnp.int32, (SUBLANES, 1), 0)
    even = lax.broadcasted_iota(jnp.int32, (SUBLANES, 2 * N_SEL), 1) % 2 == 0

    def group(gi, carry):
        r0 = pl.multiple_of(gi * SUBLANES, SUBLANES)
        last = gi == n_groups - 1
        h8 = h_ref[pl.ds(r0, SUBLANES), :]
        g2 = g2_ref[pl.ds(r0, SUBLANES), :]
        acc = [jnp.zeros((2 * SUBLANES, cw), F32) for _ in range(PEER_CHUNKS)]
        for r in range(SUBLANES):
            slot, nslot = r % PEER_RING, (r + ahead) % PEER_RING
            j = r + ahead - SUBLANES
            if j < 0:
                n_islot, n_t = cur, r0 + r + ahead
            else:
                if j == 0:
                    @pl.when(jnp.logical_and(last, more))
                    def _():
                        idx_copy(step + 1, 1 - cur).wait()
                n_islot = jnp.where(jnp.logical_and(last, more), 1 - cur, cur)
                n_t = jnp.where(last, j, r0 + SUBLANES + j)

            gather_wait(slot)
            sel = row_iota == r
            hm = jnp.concatenate([jnp.where(sel, h8[:, :half], 0.0), jnp.where(sel, h8[:, half:], 0.0)],
                                 axis=0).astype(BF16)
            p = jnp.zeros((2 * SUBLANES, 2 * N_SEL), F32)
            for c in range(PEER_CHUNKS):
                gather_start(n_islot, n_t, nslot, part=c)
                ub = pltpu.bitcast(buf[slot, :, c * cw:(c + 1) * cw], BF16)
                p = p + _dot_nt(hm[:, c * cw:(c + 1) * cw], ub)
            part = jnp.where(even, p[:SUBLANES], p[SUBLANES:])
            a2 = part + jnp.where(even, pltpu.roll(part, 2 * N_SEL - 1, 1), pltpu.roll(part, 1, 1))
            w2 = _gelu(a2) * g2
            wm = jnp.concatenate([jnp.where(even, w2, 0.0), jnp.where(even, 0.0, w2)], axis=0).astype(BF16)
            for c in range(PEER_CHUNKS):
                gather_start(n_islot, n_t, nslot, part=PEER_CHUNKS + c)
                vb = pltpu.bitcast(buf[slot, :, half + c * cw:half + (c + 1) * cw], BF16)
                acc[c] = acc[c] + _dot(wm, vb)
        peer = jnp.concatenate([a[:SUBLANES] for a in acc] + [a[SUBLANES:] for a in acc], axis=1)
        o_ref[pl.ds(r0, SUBLANES), :] = x_ref[pl.ds(r0, SUBLANES), :] + peer
        return carry

    lax.fori_loop(0, n_groups, group, 0)

    @pl.when(jnp.logical_not(more))
    def _():
        for j in range(ahead):
            gather_wait((SUBLANES + j) % PEER_RING)


def _pack_rows(tab):
    bits = lax.bitcast_convert_type(tab.astype(BF16), jnp.uint16).astype(jnp.uint32)
    half = tab.shape[1] // 2
    return bits[:, :half] | (bits[:, half:] << 16)


def _peer(eidx, h2, gate2, x2d, uv, tt):
    n = x2d.shape[0]
    assert tt % SUBLANES == 0 and SUBLANES % PEER_RING == 0
    row = lambda i: (i, 0)
    return pl.pallas_call(
        _peer_kernel,
        out_shape=jax.ShapeDtypeStruct((n, D_MODEL), F32),
        grid=(n // tt,),
        in_specs=[
            pl.BlockSpec(memory_space=pl.ANY),
            pl.BlockSpec((tt, D_MODEL), row),
            pl.BlockSpec((tt, 2 * N_SEL), row),
            pl.BlockSpec((tt, D_MODEL), row),
            pl.BlockSpec(memory_space=pl.ANY),
        ],
        out_specs=pl.BlockSpec((tt, D_MODEL), row),
        scratch_shapes=[
            pltpu.SMEM((2, tt, N_SEL), jnp.int32),
            pltpu.VMEM((PEER_RING, N_SEL, D_MODEL), jnp.uint32),
            pltpu.SemaphoreType.DMA((2,)),
            pltpu.SemaphoreType.DMA((PEER_RING,)),
        ],
        compiler_params=_params(("arbitrary",)),
        name="peer_mix",
    )(eidx, h2, gate2, x2d, uv)


def _ple_kernel(x_ref, p_ref, lnp_ref, wg_ref, wp_ref, lnf_ref, y_ref):
    x = x_ref[...]
    gate = _sigmoid(_dot(_rms(x, lnp_ref[...]).astype(BF16), wg_ref[...]))
    x = x + gate * _dot(p_ref[...].astype(BF16), wp_ref[...])
    y_ref[...] = _rms(x, lnf_ref[...])


def _ple(x2d, p2d, ln_ple, w_gate, w_proj, ln_final, tm):
    n = x2d.shape[0]
    pd = p2d.shape[1]
    row = lambda i: (i, 0)
    const = lambda i: (0, 0)
    return pl.pallas_call(
        _ple_kernel,
        out_shape=jax.ShapeDtypeStruct((n, D_MODEL), F32),
        grid=(n // tm,),
        in_specs=[
            pl.BlockSpec((tm, D_MODEL), row),
            pl.BlockSpec((tm, pd), row),
            pl.BlockSpec((1, D_MODEL), const),
            pl.BlockSpec((D_MODEL, D_MODEL), const),
            pl.BlockSpec((pd, D_MODEL), const),
            pl.BlockSpec((1, D_MODEL), const),
        ],
        out_specs=pl.BlockSpec((tm, D_MODEL), row),
        compiler_params=_params(("arbitrary",)),
        name="ple_final",
    )(x2d, p2d, ln_ple, w_gate, w_proj, ln_final)


def _rope_tables(pos):
    inv = ROPE_THETA ** (-jnp.arange(0, HEAD_DIM_A, 2, dtype=F32) / HEAD_DIM_A)
    ang = pos.astype(F32)[:, None] * inv[None, :]
    cos, sin = jnp.cos(ang), jnp.sin(ang)
    reps = PROJ_TILE // HEAD_DIM_A
    return (jnp.tile(jnp.concatenate([cos, cos], axis=1), (1, reps)),
            jnp.tile(jnp.concatenate([-sin, sin], axis=1), (1, reps)))


def _pick(n, prefs):
    for c in prefs:
        if n % c == 0:
            return c
    raise ValueError(f"no tile of {prefs} divides {n}")


def _group(x, pe, pos0, cache_k, cache_v, s0, wts):
    bn, t, _ = x.shape
    n = bn * t
    prompt = cache_k is None
    x2d = x.reshape(n, D_MODEL)
    tm = _pick(n, (1024, 512, 256, 128))

    pos = pos0 + jnp.arange(t, dtype=jnp.int32)
    cos, sin = _rope_tables(pos)
    if t < tm:
        cos, sin = jnp.tile(cos, (tm // t, 1)), jnp.tile(sin, (tm // t, 1))
    proj = _in_proj(x2d, wts["ln1"], wts["w_in"], cos, sin, wts["colscale"], tm)
    proj3 = proj.reshape(bn, t, IN_COLS)

    if prompt:
        c_att, bt_att = ATTN_BLOCK, 1
        ctx_k = ctx_v = proj3
    else:
        c_att, bt_att = t, _pick(bn, (8, 4, 2, 1))
        ctx_k = cache_k.reshape(bn, WINDOW, N_KV_A * HEAD_DIM_A)
        ctx_v = cache_v.reshape(bn, WINDOW, N_KV_A * HEAD_DIM_A)
    o_a = _attention(wts["sinks"], proj3, ctx_k, ctx_v, bt_att, c_att, prompt)

    c_ret = math.gcd(t, RET_CHUNK)
    bt_ret = 1 if prompt else _pick(bn, (8, 4, 2, 1))
    o_r, s_new = _retention(proj3, s0, wts["ln_ret"], bt_ret, c_ret)

    m = _merge(o_a.reshape(n, -1), o_r.reshape(n, -1), proj, wts["w_br_a"], wts["w_br_r"], tm)
    x2 = _out_proj(x2d, m, wts["w_o"], tm)

    h2, scores_t = _scores(x2, wts["ln2"], wts["w_q"], wts["sub_keys"], _pick(n, (512, 256, 128)))
    eidx_t, gate_t = _topk(scores_t, _pick(n, (256, 128)))
    eidx = eidx_t.reshape(N_SEL, n).T
    gate2 = jnp.repeat(gate_t.reshape(N_SEL, n).T, 2, axis=1)
    x3 = _peer(eidx, h2, gate2, x2, wts["peer_uv"], _pick(n, (64, 32, 16, 8)))

    y = _ple(x3, pe.reshape(n, -1), wts["ln_ple"], wts["w_ple_gate"], wts["w_ple_proj"], wts["ln_final"],
             _pick(n, (256, 128)))

    k_off, v_off = COL["k_a"][0], COL["v_a"][0]
    kv_w = N_KV_A * HEAD_DIM_A
    k_new, v_new = proj3[:, :, k_off:k_off + kv_w], proj3[:, :, v_off:v_off + kv_w]
    if prompt:
        k_win, v_win = k_new[:, -WINDOW:], v_new[:, -WINDOW:]
    else:
        k_win = jnp.concatenate([ctx_k, k_new], axis=1)[:, -WINDOW:]
        v_win = jnp.concatenate([ctx_v, v_new], axis=1)[:, -WINDOW:]
    shp = (bn, WINDOW, N_KV_A, HEAD_DIM_A)
    return y.reshape(bn, t, D_MODEL), k_win.reshape(shp), v_win.reshape(shp), s_new


def kernel(x_prompt, x_sample, cache_k_win, cache_v_win, state_ret, p_prompt, p_sample, ln1, w_in, attn_sinks, ln_ret, w_branch_attn, w_branch_ret, w_out, ln2, w_peer_query, peer_sub_keys, peer_u, peer_v, ln_ple, w_ple_gate, w_ple_proj, ln_final):
    depth = ln1.shape[0]
    assert depth == 1, "single-layer step"
    i = 0
    colscale = jnp.ones((IN_COLS,), F32).at[COL["k_r"][0]:COL["k_r"][0] + COL["k_r"][1]].set(DK_R ** -0.5)
    wts = dict(
        ln1=ln1[i][None, :],
        w_in=jnp.concatenate([w_in[i][:, _REF_COL[f][0]:_REF_COL[f][0] + _REF_COL[f][1]] for f in _NEW_ORDER],
                             axis=1).astype(BF16),
        colscale=colscale[None, :],
        sinks=attn_sinks[i],
        ln_ret=ln_ret[i].reshape(N_HEADS_R, DV_R),
        w_br_a=w_branch_attn[i].astype(BF16),
        w_br_r=w_branch_ret[i].astype(BF16),
        w_o=w_out[i].astype(BF16),
        ln2=ln2[i][None, :],
        w_q=w_peer_query[i].astype(BF16),
        sub_keys=peer_sub_keys[i].astype(BF16),
        peer_uv=jnp.concatenate([_pack_rows(peer_u[i]), _pack_rows(peer_v[i])], axis=1),
        ln_ple=ln_ple[i][None, :],
        w_ple_gate=w_ple_gate[i].astype(BF16),
        w_ple_proj=w_ple_proj[i].astype(BF16),
        ln_final=ln_final[None, :],
    )
    yp, kp, vp, sp = _group(x_prompt, p_prompt[i], 0, None, None, None, wts)
    ys, ks, vs, ss = _group(x_sample, p_sample[i], PAST_LEN, cache_k_win[i], cache_v_win[i], state_ret[i], wts)
    return (yp, ys, kp[None], vp[None], sp[None], ks[None], vs[None], ss[None])
```

```python
import functools
import math

import jax
import jax.numpy as jnp
from jax import lax
from jax.experimental import pallas as pl
from jax.experimental.pallas import tpu as pltpu

F32 = jnp.float32
BF16 = jnp.bfloat16

D_MODEL = 2048
PAST_LEN = 16384
HEAD_DIM_A = 64
N_HEADS_A = 16
N_KV_A = 4
GROUP_A = 4
WINDOW = 128
ATTN_BLOCK = 128
N_HEADS_R = 8
DK_R = 64
DV_R = 128
RET_CHUNK = 128
PEER_HEADS = 8
N_KEYS = 128
PEER_TOPK = 16
D_KEY_HALF = 128
N_SEL = PEER_HEADS * PEER_TOPK
PEER_CHUNKS = 4
ROPE_THETA = 10000.0
EPS = 1e-6
NEG_INF = -1e30

LANES = 128
SUBLANES = 8
VMEM_LIMIT_BYTES = 56 * 1024 * 1024

_REF_FIELDS = (("q_a", 1024), ("k_a", 256), ("v_a", 256), ("q_r", 512), ("k_r", 512),
               ("v_r", 1024), ("g_r", 1024), ("gate_a", 2048), ("gate_r", 2048))
_NEW_ORDER = ("gate_a", "gate_r", "q_a", "v_r", "g_r", "q_r", "k_r", "k_a", "v_a")
_ROPE_FIELDS = ("q_a", "q_r", "k_r", "k_a")
PROJ_TILE = 256


def _layout():
    ref_off, o = {}, 0
    for name, w in _REF_FIELDS:
        ref_off[name] = (o, w)
        o += w
    new_off, o = {}, 0
    for name in _NEW_ORDER:
        new_off[name] = (o, ref_off[name][1])
        o += ref_off[name][1]
    return new_off, ref_off, o


COL, _REF_COL, IN_COLS = _layout()


def _blk(name, width):
    off, w = COL[name]
    assert off % width == 0 and w % width == 0
    return off // width


def _params(sem, vmem=VMEM_LIMIT_BYTES):
    return pltpu.CompilerParams(dimension_semantics=sem, vmem_limit_bytes=vmem)


def _rms(x, g):
    return x * lax.rsqrt(jnp.mean(x * x, axis=-1, keepdims=True) + EPS) * g


def _sigmoid(x):
    return 1.0 / (1.0 + jnp.exp(-x))


def _dot(a, b):
    return jnp.dot(a, b, preferred_element_type=F32)


def _dot_nt(a, b):
    return lax.dot_general(a, b, (((1,), (1,)), ((), ())), preferred_element_type=F32)


def _dot_tn(a, b):
    return lax.dot_general(a, b, (((0,), (0,)), ((), ())), preferred_element_type=F32)


def _in_proj_kernel(x_ref, g_ref, w_ref, cos_ref, sin_ref, cs_ref, o_ref, h_ref, *, rope_ranges):
    j = pl.program_id(1)

    @pl.when(j == 0)
    def _():
        h_ref[...] = _rms(x_ref[...], g_ref[...]).astype(BF16)

    acc = _dot(h_ref[...], w_ref[...])
    is_rope = functools.reduce(jnp.logical_or, [(j >= a) & (j < b) for a, b in rope_ranges])

    @pl.when(is_rope)
    def _():
        tn = acc.shape[1]
        lane = lax.broadcasted_iota(jnp.int32, acc.shape, 1)
        first_half = (lane % HEAD_DIM_A) < (HEAD_DIM_A // 2)
        partner = jnp.where(first_half, pltpu.roll(acc, tn - HEAD_DIM_A // 2, 1), pltpu.roll(acc, HEAD_DIM_A // 2, 1))
        o_ref[...] = (acc * cos_ref[...] + partner * sin_ref[...]) * cs_ref[...]

    @pl.when(jnp.logical_not(is_rope))
    def _():
        o_ref[...] = acc


def _in_proj(x2d, ln, w, cos, sin, colscale, tm):
    n = x2d.shape[0]
    rt = cos.shape[0] // tm
    rope_ranges = tuple((COL[f][0] // PROJ_TILE, (COL[f][0] + COL[f][1]) // PROJ_TILE) for f in _ROPE_FIELDS)
    return pl.pallas_call(
        functools.partial(_in_proj_kernel, rope_ranges=rope_ranges),
        out_shape=jax.ShapeDtypeStruct((n, IN_COLS), F32),
        grid=(n // tm, IN_COLS // PROJ_TILE),
        in_specs=[
            pl.BlockSpec((tm, D_MODEL), lambda i, j: (i, 0)),
            pl.BlockSpec((1, D_MODEL), lambda i, j: (0, 0)),
            pl.BlockSpec((D_MODEL, PROJ_TILE), lambda i, j: (0, j)),
            pl.BlockSpec((tm, PROJ_TILE), lambda i, j: (i % rt, 0)),
            pl.BlockSpec((tm, PROJ_TILE), lambda i, j: (i % rt, 0)),
            pl.BlockSpec((1, PROJ_TILE), lambda i, j: (0, j)),
        ],
        out_specs=pl.BlockSpec((tm, PROJ_TILE), lambda i, j: (i, j)),
        scratch_shapes=[pltpu.VMEM((tm, D_MODEL), BF16)],
        compiler_params=_params(("arbitrary", "arbitrary")),
        name="in_proj",
    )(x2d, ln, w, cos, sin, colscale)


def _attn_kernel(sink_ref, q_ref, kc_ref, vc_ref, kx_ref, vx_ref, o_ref, *, first_ctx_invalid):
    bt, c, _ = q_ref.shape
    rows = GROUP_A * c
    qi = lax.broadcasted_iota(jnp.int32, (rows, WINDOW), 0) % c
    kj = lax.broadcasted_iota(jnp.int32, (rows, WINDOW), 1)
    ctx_ok = kj >= qi
    if first_ctx_invalid:
        ctx_ok = jnp.logical_and(ctx_ok, pl.program_id(1) > 0)
    qi_c = lax.broadcasted_iota(jnp.int32, (rows, c), 0) % c
    kj_c = lax.broadcasted_iota(jnp.int32, (rows, c), 1)
    cur_ok = kj_c <= qi_c
    grp = lax.broadcasted_iota(jnp.int32, (rows, 1), 0) // c
    scale = HEAD_DIM_A ** -0.5

    def one(b):
        q = q_ref[b]
        kc, vc, kx, vx = kc_ref[b], vc_ref[b], kx_ref[b], vx_ref[b]
        outs = [None] * N_HEADS_A
        for kh in range(N_KV_A):
            sl = slice(kh * HEAD_DIM_A, (kh + 1) * HEAD_DIM_A)
            q4 = jnp.concatenate(
                [q[:, (kh * GROUP_A + g) * HEAD_DIM_A:(kh * GROUP_A + g + 1) * HEAD_DIM_A] for g in range(GROUP_A)],
                axis=0).astype(BF16)
            s_x = jnp.where(ctx_ok, _dot_nt(q4, kx[:, sl].astype(BF16)) * scale, NEG_INF)
            s_c = jnp.where(cur_ok, _dot_nt(q4, kc[:, sl].astype(BF16)) * scale, NEG_INF)
            sink = jnp.zeros((rows, 1), F32)
            for g in range(GROUP_A):
                sink = jnp.where(grp == g, sink_ref[kh * GROUP_A + g], sink)
            m = jnp.maximum(jnp.maximum(jnp.max(s_x, axis=-1, keepdims=True), jnp.max(s_c, axis=-1, keepdims=True)), sink)
            e_x = jnp.exp(s_x - m)
            e_c = jnp.exp(s_c - m)
            den = jnp.sum(e_x, axis=-1, keepdims=True) + jnp.sum(e_c, axis=-1, keepdims=True) + jnp.exp(sink - m)
            o4 = _dot((e_x / den).astype(BF16), vx[:, sl].astype(BF16)) + _dot((e_c / den).astype(BF16), vc[:, sl].astype(BF16))
            for g in range(GROUP_A):
                outs[kh * GROUP_A + g] = o4[g * c:(g + 1) * c]
        o_ref[b] = jnp.concatenate(outs, axis=1).astype(o_ref.dtype)

    if bt == 1:
        one(0)
    else:
        def body(b, carry):
            one(b)
            return carry
        lax.fori_loop(0, bt, body, 0)


def _attention(sinks, proj3, ctx_k, ctx_v, bt, c, prompt):
    bn, t, _ = proj3.shape
    qb, kb, vb = _blk("q_a", 1024), _blk("k_a", 256), _blk("v_a", 256)
    if prompt:
        ctx_specs = [pl.BlockSpec((bt, WINDOW, 256), lambda b, n: (b, jnp.maximum(n - 1, 0), kb)),
                     pl.BlockSpec((bt, WINDOW, 256), lambda b, n: (b, jnp.maximum(n - 1, 0), vb))]
    else:
        ctx_specs = [pl.BlockSpec((bt, WINDOW, 256), lambda b, n: (b, 0, 0)),
                     pl.BlockSpec((bt, WINDOW, 256), lambda b, n: (b, 0, 0))]
    return pl.pallas_call(
        functools.partial(_attn_kernel, first_ctx_invalid=prompt),
        out_shape=jax.ShapeDtypeStruct((bn, t, N_HEADS_A * HEAD_DIM_A), BF16),
        grid=(bn // bt, t // c),
        in_specs=[
            pl.BlockSpec(memory_space=pltpu.SMEM),
            pl.BlockSpec((bt, c, 1024), lambda b, n: (b, n, qb)),
            pl.BlockSpec((bt, c, 256), lambda b, n: (b, n, kb)),
            pl.BlockSpec((bt, c, 256), lambda b, n: (b, n, vb)),
        ] + ctx_specs,
        out_specs=pl.BlockSpec((bt, c, 1024), lambda b, n: (b, n, 0)),
        compiler_params=_params(("arbitrary", "arbitrary")),
        name="attn_prompt" if prompt else "attn_sample",
    )(sinks, proj3, proj3, proj3, ctx_k, ctx_v)


def _ret_kernel(cdec_ref, q_ref, k_ref, v_ref, g_ref, intra_ref, cross_ref, kdec_ref, ln_ref, *rest, has_s0):
    if has_s0:
        s0_ref, o_ref, sout_ref, s_ref = rest
    else:
        o_ref, sout_ref, s_ref = rest
    bt = q_ref.shape[0]
    ci = pl.program_id(1)

    @pl.when(ci == 0)
    def _():
        s_ref[...] = s0_ref[...] if has_s0 else jnp.zeros(s_ref.shape, F32)

    def one(b):
        q, k, v, gt = q_ref[b], k_ref[b], v_ref[b], g_ref[b]
        outs = []
        for h in range(N_HEADS_R):
            qh = q[:, h * DK_R:(h + 1) * DK_R].astype(BF16)
            kh = k[:, h * DK_R:(h + 1) * DK_R]
            vh = v[:, h * DV_R:(h + 1) * DV_R].astype(BF16)
            s = s_ref[b, h]
            att = _dot_nt(qh, kh.astype(BF16)) * intra_ref[h]
            o = _dot(att.astype(BF16), vh) + _dot(qh, s.astype(BF16)) * cross_ref[h]
            s_ref[b, h] = s * cdec_ref[h] + _dot_tn((kh * kdec_ref[h]).astype(BF16), vh)
            mu = jnp.mean(o, axis=-1, keepdims=True)
            d = o - mu
            var = jnp.mean(d * d, axis=-1, keepdims=True)
            gh = gt[:, h * DV_R:(h + 1) * DV_R]
            outs.append(d * lax.rsqrt(var + EPS) * ln_ref[h:h + 1, :] * (gh * _sigmoid(gh)))
        o_ref[b] = jnp.concatenate(outs, axis=1).astype(o_ref.dtype)

    if bt == 1:
        one(0)
    else:
        def body(b, carry):
            one(b)
            return carry
        lax.fori_loop(0, bt, body, 0)

    @pl.when(ci == pl.num_programs(1) - 1)
    def _():
        sout_ref[...] = s_ref[...]


def _retention(proj3, s0, ln_ret, bt, c):
    bn, t, _ = proj3.shape
    log_g = jnp.log1p(-jnp.exp2(-5.0 - jnp.arange(N_HEADS_R, dtype=F32)))
    i = jnp.arange(c, dtype=F32)
    diff = i[:, None] - i[None, :]
    intra = jnp.where(diff[None] >= 0, jnp.exp(jnp.maximum(diff, 0.0)[None] * log_g[:, None, None]), 0.0)
    cross = jnp.exp((i + 1.0)[None, :] * log_g[:, None])
    kdec = jnp.exp((c - 1.0 - i)[None, :] * log_g[:, None])
    cdec = jnp.exp(c * log_g)
    cross_b = jnp.broadcast_to(cross[:, :, None], (N_HEADS_R, c, DV_R))
    kdec_b = jnp.broadcast_to(kdec[:, :, None], (N_HEADS_R, c, DK_R))
    has_s0 = s0 is not None
    const3 = lambda b, n: (0, 0, 0)
    state_spec = pl.BlockSpec((bt, N_HEADS_R, DK_R, DV_R), lambda b, n: (b, 0, 0, 0))
    in_specs = [
        pl.BlockSpec(memory_space=pltpu.SMEM),
        pl.BlockSpec((bt, c, 512), lambda b, n: (b, n, _blk("q_r", 512))),
        pl.BlockSpec((bt, c, 512), lambda b, n: (b, n, _blk("k_r", 512))),
        pl.BlockSpec((bt, c, 1024), lambda b, n: (b, n, _blk("v_r", 1024))),
        pl.BlockSpec((bt, c, 1024), lambda b, n: (b, n, _blk("g_r", 1024))),
        pl.BlockSpec((N_HEADS_R, c, c), const3),
        pl.BlockSpec((N_HEADS_R, c, DV_R), const3),
        pl.BlockSpec((N_HEADS_R, c, DK_R), const3),
        pl.BlockSpec((N_HEADS_R, DV_R), lambda b, n: (0, 0)),
    ]
    args = [cdec, proj3, proj3, proj3, proj3, intra, cross_b, kdec_b, ln_ret]
    if has_s0:
        in_specs.append(state_spec)
        args.append(s0)
    return pl.pallas_call(
        functools.partial(_ret_kernel, has_s0=has_s0),
        out_shape=(jax.ShapeDtypeStruct((bn, t, N_HEADS_R * DV_R), BF16),
                   jax.ShapeDtypeStruct((bn, N_HEADS_R, DK_R, DV_R), F32)),
        grid=(bn // bt, t // c),
        in_specs=in_specs,
        out_specs=(pl.BlockSpec((bt, c, 1024), lambda b, n: (b, n, 0)), state_spec),
        scratch_shapes=[pltpu.VMEM((bt, N_HEADS_R, DK_R, DV_R), F32)],
        compiler_params=_params(("arbitrary", "arbitrary")),
        name="ret_sample" if has_s0 else "ret_prompt",
    )(*args)


def _merge_kernel(oa_ref, or_ref, ga_ref, gr_ref, wa_ref, wr_ref, m_ref):
    br_a = _dot(oa_ref[...], wa_ref[...])
    br_r = _dot(or_ref[...], wr_ref[...])
    m_ref[...] = (_sigmoid(ga_ref[...]) * br_a + _sigmoid(gr_ref[...]) * br_r).astype(m_ref.dtype)


def _merge(o_a, o_r, proj, w_a, w_r, tm, tn=512):
    n = o_a.shape[0]
    ga, gr = _blk("gate_a", tn), _blk("gate_r", tn)
    return pl.pallas_call(
        _merge_kernel,
        out_shape=jax.ShapeDtypeStruct((n, D_MODEL), BF16),
        grid=(n // tm, D_MODEL // tn),
        in_specs=[
            pl.BlockSpec((tm, 1024), lambda i, j: (i, 0)),
            pl.BlockSpec((tm, 1024), lambda i, j: (i, 0)),
            pl.BlockSpec((tm, tn), lambda i, j: (i, ga + j)),
            pl.BlockSpec((tm, tn), lambda i, j: (i, gr + j)),
            pl.BlockSpec((1024, tn), lambda i, j: (0, j)),
            pl.BlockSpec((1024, tn), lambda i, j: (0, j)),
        ],
        out_specs=pl.BlockSpec((tm, tn), lambda i, j: (i, j)),
        compiler_params=_params(("arbitrary", "arbitrary")),
        name="merge",
    )(o_a, o_r, proj, proj, w_a, w_r)


def _out_proj_kernel(x_ref, m_ref, w_ref, o_ref):
    o_ref[...] = x_ref[...] + _dot(m_ref[...], w_ref[...])


def _out_proj(x2d, m, w_o, tm, tn=512):
    n = x2d.shape[0]
    return pl.pallas_call(
        _out_proj_kernel,
        out_shape=jax.ShapeDtypeStruct((n, D_MODEL), F32),
        grid=(n // tm, D_MODEL // tn),
        in_specs=[
            pl.BlockSpec((tm, tn), lambda i, j: (i, j)),
            pl.BlockSpec((tm, D_MODEL), lambda i, j: (i, 0)),
            pl.BlockSpec((D_MODEL, tn), lambda i, j: (0, j)),
        ],
        out_specs=pl.BlockSpec((tm, tn), lambda i, j: (i, j)),
        compiler_params=_params(("arbitrary", "arbitrary")),
        name="out_proj",
    )(x2d, m, w_o)


def _scores_kernel(x_ref, g_ref, w_ref, sk_ref, h_out_ref, s_ref, h_ref):
    j = pl.program_id(1)

    @pl.when(j == 0)
    def _():
        h = _rms(x_ref[...], g_ref[...])
        h_ref[...] = h.astype(BF16)
        h_out_ref[...] = h

    qry = _dot(h_ref[...], w_ref[...]).astype(BF16)
    for g in range(qry.shape[1] // D_KEY_HALF):
        s_ref[g] = _dot_nt(sk_ref[g % 2], qry[:, g * D_KEY_HALF:(g + 1) * D_KEY_HALF])


def _scores(x2d, ln, w_q, sub_keys, tm, tn=512):
    n = x2d.shape[0]
    ng = tn // D_KEY_HALF
    return pl.pallas_call(
        _scores_kernel,
        out_shape=(jax.ShapeDtypeStruct((n, D_MODEL), F32),
                   jax.ShapeDtypeStruct((2 * PEER_HEADS, N_KEYS, n), F32)),
        grid=(n // tm, D_MODEL // tn),
        in_specs=[
            pl.BlockSpec((tm, D_MODEL), lambda i, j: (i, 0)),
            pl.BlockSpec((1, D_MODEL), lambda i, j: (0, 0)),
            pl.BlockSpec((D_MODEL, tn), lambda i, j: (0, j)),
            pl.BlockSpec((2, N_KEYS, D_KEY_HALF), lambda i, j: (0, 0, 0)),
        ],
        out_specs=(pl.BlockSpec((tm, D_MODEL), lambda i, j: (i, 0)),
                   pl.BlockSpec((ng, N_KEYS, tm), lambda i, j: (j, 0, i))),
        scratch_shapes=[pltpu.VMEM((tm, D_MODEL), BF16)],
        compiler_params=_params(("arbitrary", "arbitrary")),
        name="peer_scores",
    )(x2d, ln, w_q, sub_keys)


def _take_top(vals, iota, count, fill, payload=None):
    n_rows = vals.shape[0]
    top_v, top_i = [], []
    for _ in range(count):
        m = jnp.max(vals, axis=0, keepdims=True)
        pos = jnp.min(jnp.where(vals == m, iota, n_rows), axis=0, keepdims=True)
        sel = iota == pos
        top_v.append(m)
        top_i.append(pos if payload is None else jnp.max(jnp.where(sel, payload, -1), axis=0, keepdims=True))
        vals = jnp.where(sel, fill, vals)
    return jnp.concatenate(top_v, axis=0), jnp.concatenate(top_i, axis=0)


_COMB_PIECES = tuple((i, PEER_TOPK // (i + 1)) for i in range(SUBLANES))
_COMB_ROWS = sum(max(nj, SUBLANES) for _, nj in _COMB_PIECES) + SUBLANES


def _topk_kernel(s_ref, e_ref, g_ref):
    tt = s_ref.shape[2]
    key_iota = lax.broadcasted_iota(jnp.int32, (N_KEYS, tt), 0)
    comb_iota = lax.broadcasted_iota(jnp.int32, (_COMB_ROWS, tt), 0)
    ninf = float("-inf")

    def head(h, carry):
        s0, i0 = _take_top(s_ref[2 * h], key_iota, PEER_TOPK, ninf)
        s1, i1 = _take_top(s_ref[2 * h + 1], key_iota, PEER_TOPK, ninf)
        comb, cidx = [], []
        for i, nj in _COMB_PIECES:
            rows = max(nj, SUBLANES)
            sv = s0[i:i + 1, :] + s1[:rows]
            iv = i0[i:i + 1, :] * N_KEYS + i1[:rows]
            if nj < rows:
                keep = lax.broadcasted_iota(jnp.int32, (rows, tt), 0) < nj
                sv, iv = jnp.where(keep, sv, ninf), jnp.where(keep, iv, -1)
            comb.append(sv)
            cidx.append(iv)
        comb.append(s0[SUBLANES:] + s1[0:1, :])
        cidx.append(i0[SUBLANES:] * N_KEYS + i1[0:1, :])
        comb, cidx = jnp.concatenate(comb, axis=0), jnp.concatenate(cidx, axis=0)
        best, eidx = _take_top(comb, comb_iota, PEER_TOPK, ninf, payload=cidx)
        ex = jnp.exp(best - jnp.max(best, axis=0, keepdims=True))
        g_ref[h] = ex / jnp.sum(ex, axis=0, keepdims=True)
        e_ref[h] = eidx
        return carry

    lax.fori_loop(0, PEER_HEADS, head, 0)


def _topk(scores_t, tt=256):
    n = scores_t.shape[2]
    out_spec = pl.BlockSpec((PEER_HEADS, PEER_TOPK, tt), lambda i: (0, 0, i))
    return pl.pallas_call(
        _topk_kernel,
        out_shape=(jax.ShapeDtypeStruct((PEER_HEADS, PEER_TOPK, n), jnp.int32),
                   jax.ShapeDtypeStruct((PEER_HEADS, PEER_TOPK, n), F32)),
        grid=(n // tt,),
        in_specs=[pl.BlockSpec((2 * PEER_HEADS, N_KEYS, tt), lambda i: (0, 0, i))],
        out_specs=(out_spec, out_spec),
        compiler_params=_params(("arbitrary",)),
        name="peer_topk",
    )(scores_t)


def _gelu(a):
    return 0.5 * a * (1.0 + lax.erf(a * (2.0 ** -0.5)))


def _peer_kernel(idx_hbm, h_ref, g2_ref, x_ref, uv_hbm, o_ref, idx_smem, buf, sem_i, sem):
    tt = h_ref.shape[0]
    half = D_MODEL // 2
    step = pl.program_id(0)
    more = step + 1 < pl.num_programs(0)
    cur = step % 2

    def idx_copy(s, islot):
        return pltpu.make_async_copy(idx_hbm.at[pl.ds(s * tt, tt), :], idx_smem.at[islot], sem_i.at[islot])

    n_groups = tt // SUBLANES
    n_parts = 2 * PEER_CHUNKS
    cw = half // PEER_CHUNKS

    def gather_start(islot, t, bank, j, part=None):
        ks = range(N_SEL) if part is None else range(part * N_SEL // n_parts, (part + 1) * N_SEL // n_parts)
        for k in ks:
            e = idx_smem[islot, t, k]
            pltpu.make_async_copy(uv_hbm.at[e], buf.at[bank, j, pl.ds(k, 1), :],
                                  sem.at[bank]).start()

    def gather_wait(bank):
        pltpu.make_async_copy(buf.at[bank], buf.at[bank], sem.at[bank]).wait()

    @pl.when(step == 0)
    def _():
        first = idx_copy(0, 0)
        first.start()
        first.wait()
        for j in range(SUBLANES):
            gather_start(0, j, 0, j)

    @pl.when(more)
    def _():
        idx_copy(step + 1, 1 - cur).start()

    row_iota = lax.broadcasted_iota(jnp.int32, (SUBLANES, 1), 0)
    even = lax.broadcasted_iota(jnp.int32, (SUBLANES, 2 * N_SEL), 1) % 2 == 0

    def group(gi, carry):
        r0 = pl.multiple_of(gi * SUBLANES, SUBLANES)
        last = gi == n_groups - 1
        bank = gi % 2

        @pl.when(jnp.logical_and(last, more))
        def _():
            idx_copy(step + 1, 1 - cur).wait()

        n_islot = jnp.where(jnp.logical_and(last, more), 1 - cur, cur)
        n_t0 = jnp.where(last, 0, r0 + SUBLANES)
        gather_wait(bank)
        h8 = h_ref[pl.ds(r0, SUBLANES), :]
        g2 = g2_ref[pl.ds(r0, SUBLANES), :]
        acc = [jnp.zeros((2 * SUBLANES, cw), F32) for _ in range(PEER_CHUNKS)]
        issued = [0]

        def request():
            q = issued[0]
            issued[0] += 1
            gather_start(n_islot, n_t0 + q // n_parts, 1 - bank, q // n_parts, part=q % n_parts)

        def u_dot(r, c, p):
            sel = row_iota == r
            hm = jnp.concatenate([jnp.where(sel, h8[:, c * cw:(c + 1) * cw], 0.0),
                                  jnp.where(sel, h8[:, half + c * cw:half + (c + 1) * cw], 0.0)],
                                 axis=0).astype(BF16)
            ub = pltpu.bitcast(buf[bank, r, :, c * cw:(c + 1) * cw], BF16)
            return p + _dot_nt(hm, ub)

        def mix_weights(p):
            part = jnp.where(even, p[:SUBLANES], p[SUBLANES:])
            a2 = part + jnp.where(even, pltpu.roll(part, 2 * N_SEL - 1, 1), pltpu.roll(part, 1, 1))
            w2 = _gelu(a2) * g2
            return jnp.concatenate([jnp.where(even, w2, 0.0), jnp.where(even, 0.0, w2)], axis=0).astype(BF16)

        def v_dot(r, c, wm):
            vb = pltpu.bitcast(buf[bank, r, :, half + c * cw:half + (c + 1) * cw], BF16)
            acc[c] = acc[c] + _dot(wm, vb)

        p_next = jnp.zeros((2 * SUBLANES, 2 * N_SEL), F32)
        for c in range(PEER_CHUNKS):
            request()
            p_next = u_dot(0, c, p_next)
        for r in range(SUBLANES):
            wm = mix_weights(p_next)
            p_next = jnp.zeros((2 * SUBLANES, 2 * N_SEL), F32)
            for c in range(PEER_CHUNKS):
                if r + 1 < SUBLANES:
                    request()
                    p_next = u_dot(r + 1, c, p_next)
                request()
                v_dot(r, c, wm)
        assert issued[0] == SUBLANES * n_parts
        peer = jnp.concatenate([a[:SUBLANES] for a in acc] + [a[SUBLANES:] for a in acc], axis=1)
        o_ref[pl.ds(r0, SUBLANES), :] = x_ref[pl.ds(r0, SUBLANES), :] + peer
        return carry

    lax.fori_loop(0, n_groups, group, 0)

    @pl.when(jnp.logical_not(more))
    def _():
        gather_wait(n_groups % 2)


def _pack_rows(tab):
    bits = lax.bitcast_convert_type(tab.astype(BF16), jnp.uint16).astype(jnp.uint32)
    half = tab.shape[1] // 2
    return bits[:, :half] | (bits[:, half:] << 16)


def _peer(eidx, h2, gate2, x2d, uv, tt):
    n = x2d.shape[0]
    assert tt % (2 * SUBLANES) == 0
    row = lambda i: (i, 0)
    return pl.pallas_call(
        _peer_kernel,
        out_shape=jax.ShapeDtypeStruct((n, D_MODEL), F32),
        grid=(n // tt,),
        in_specs=[
            pl.BlockSpec(memory_space=pl.ANY),
            pl.BlockSpec((tt, D_MODEL), row),
            pl.BlockSpec((tt, 2 * N_SEL), row),
            pl.BlockSpec((tt, D_MODEL), row),
            pl.BlockSpec(memory_space=pl.ANY),
        ],
        out_specs=pl.BlockSpec((tt, D_MODEL), row),
        scratch_shapes=[
            pltpu.SMEM((2, tt, N_SEL), jnp.int32),
            pltpu.VMEM((2, SUBLANES, N_SEL, D_MODEL), jnp.uint32),
            pltpu.SemaphoreType.DMA((2,)),
            pltpu.SemaphoreType.DMA((2,)),
        ],
        compiler_params=_params(("arbitrary",)),
        name="peer_mix",
    )(eidx, h2, gate2, x2d, uv)


def _ple_kernel(x_ref, p_ref, lnp_ref, wg_ref, wp_ref, lnf_ref, y_ref):
    x = x_ref[...]
    gate = _sigmoid(_dot(_rms(x, lnp_ref[...]).astype(BF16), wg_ref[...]))
    x = x + gate * _dot(p_ref[...].astype(BF16), wp_ref[...])
    y_ref[...] = _rms(x, lnf_ref[...])


def _ple(x2d, p2d, ln_ple, w_gate, w_proj, ln_final, tm):
    n = x2d.shape[0]
    pd = p2d.shape[1]
    row = lambda i: (i, 0)
    const = lambda i: (0, 0)
    return pl.pallas_call(
        _ple_kernel,
        out_shape=jax.ShapeDtypeStruct((n, D_MODEL), F32),
        grid=(n // tm,),
        in_specs=[
            pl.BlockSpec((tm, D_MODEL), row),
            pl.BlockSpec((tm, pd), row),
            pl.BlockSpec((1, D_MODEL), const),
            pl.BlockSpec((D_MODEL, D_MODEL), const),
            pl.BlockSpec((pd, D_MODEL), const),
            pl.BlockSpec((1, D_MODEL), const),
        ],
        out_specs=pl.BlockSpec((tm, D_MODEL), row),
        compiler_params=_params(("arbitrary",)),
        name="ple_final",
    )(x2d, p2d, ln_ple, w_gate, w_proj, ln_final)


def _rope_tables(pos):
    inv = ROPE_THETA ** (-jnp.arange(0, HEAD_DIM_A, 2, dtype=F32) / HEAD_DIM_A)
    ang = pos.astype(F32)[:, None] * inv[None, :]
    cos, sin = jnp.cos(ang), jnp.sin(ang)
    reps = PROJ_TILE // HEAD_DIM_A
    return (jnp.tile(jnp.concatenate([cos, cos], axis=1), (1, reps)),
            jnp.tile(jnp.concatenate([-sin, sin], axis=1), (1, reps)))


def _pick(n, prefs):
    for c in prefs:
        if n % c == 0:
            return c
    raise ValueError(f"no tile of {prefs} divides {n}")


def _group(x, pe, pos0, cache_k, cache_v, s0, wts):
    bn, t, _ = x.shape
    n = bn * t
    prompt = cache_k is None
    x2d = x.reshape(n, D_MODEL)
    tm = _pick(n, (1024, 512, 256, 128))

    pos = pos0 + jnp.arange(t, dtype=jnp.int32)
    cos, sin = _rope_tables(pos)
    if t < tm:
        cos, sin = jnp.tile(cos, (tm // t, 1)), jnp.tile(sin, (tm // t, 1))
    proj = _in_proj(x2d, wts["ln1"], wts["w_in"], cos, sin, wts["colscale"], tm)
    proj3 = proj.reshape(bn, t, IN_COLS)

    if prompt:
        c_att, bt_att = ATTN_BLOCK, 1
        ctx_k = ctx_v = proj3
    else:
        c_att, bt_att = t, _pick(bn, (8, 4, 2, 1))
        ctx_k = cache_k.reshape(bn, WINDOW, N_KV_A * HEAD_DIM_A)
        ctx_v = cache_v.reshape(bn, WINDOW, N_KV_A * HEAD_DIM_A)
    o_a = _attention(wts["sinks"], proj3, ctx_k, ctx_v, bt_att, c_att, prompt)

    c_ret = math.gcd(t, RET_CHUNK)
    bt_ret = 1 if prompt else _pick(bn, (8, 4, 2, 1))
    o_r, s_new = _retention(proj3, s0, wts["ln_ret"], bt_ret, c_ret)

    m = _merge(o_a.reshape(n, -1), o_r.reshape(n, -1), proj, wts["w_br_a"], wts["w_br_r"], tm)
    x2 = _out_proj(x2d, m, wts["w_o"], tm)

    h2, scores_t = _scores(x2, wts["ln2"], wts["w_q"], wts["sub_keys"], _pick(n, (512, 256, 128)))
    eidx_t, gate_t = _topk(scores_t, _pick(n, (256, 128)))
    eidx = eidx_t.reshape(N_SEL, n).T
    gate2 = jnp.repeat(gate_t.reshape(N_SEL, n).T, 2, axis=1)
    x3 = _peer(eidx, h2, gate2, x2, wts["peer_uv"], _pick(n, (64, 32, 16, 8)))

    y = _ple(x3, pe.reshape(n, -1), wts["ln_ple"], wts["w_ple_gate"], wts["w_ple_proj"], wts["ln_final"],
             _pick(n, (256, 128)))

    k_off, v_off = COL["k_a"][0], COL["v_a"][0]
    kv_w = N_KV_A * HEAD_DIM_A
    k_new, v_new = proj3[:, :, k_off:k_off + kv_w], proj3[:, :, v_off:v_off + kv_w]
    if prompt:
        k_win, v_win = k_new[:, -WINDOW:], v_new[:, -WINDOW:]
    else:
        k_win = jnp.concatenate([ctx_k, k_new], axis=1)[:, -WINDOW:]
        v_win = jnp.concatenate([ctx_v, v_new], axis=1)[:, -WINDOW:]
    shp = (bn, WINDOW, N_KV_A, HEAD_DIM_A)
    return y.reshape(bn, t, D_MODEL), k_win.reshape(shp), v_win.reshape(shp), s_new


def kernel(x_prompt, x_sample, cache_k_win, cache_v_win, state_ret, p_prompt, p_sample, ln1, w_in, attn_sinks, ln_ret, w_branch_attn, w_branch_ret, w_out, ln2, w_peer_query, peer_sub_keys, peer_u, peer_v, ln_ple, w_ple_gate, w_ple_proj, ln_final):
    depth = ln1.shape[0]
    assert depth == 1, "single-layer step"
    i = 0
    colscale = jnp.ones((IN_COLS,), F32).at[COL["k_r"][0]:COL["k_r"][0] + COL["k_r"][1]].set(DK_R ** -0.5)
    wts = dict(
        ln1=ln1[i][None, :],
        w_in=jnp.concatenate([w_in[i][:, _REF_COL[f][0]:_REF_COL[f][0] + _REF_COL[f][1]] for f in _NEW_ORDER],
                             axis=1).astype(BF16),
        colscale=colscale[None, :],
        sinks=attn_sinks[i],
        ln_ret=ln_ret[i].reshape(N_HEADS_R, DV_R),
        w_br_a=w_branch_attn[i].astype(BF16),
        w_br_r=w_branch_ret[i].astype(BF16),
        w_o=w_out[i].astype(BF16),
        ln2=ln2[i][None, :],
        w_q=w_peer_query[i].astype(BF16),
        sub_keys=peer_sub_keys[i].astype(BF16),
        peer_uv=jnp.concatenate([_pack_rows(peer_u[i]), _pack_rows(peer_v[i])], axis=1)[:, None, :],
        ln_ple=ln_ple[i][None, :],
        w_ple_gate=w_ple_gate[i].astype(BF16),
        w_ple_proj=w_ple_proj[i].astype(BF16),
        ln_final=ln_final[None, :],
    )
    yp, kp, vp, sp = _group(x_prompt, p_prompt[i], 0, None, None, None, wts)
    ys, ks, vs, ss = _group(x_sample, p_sample[i], PAST_LEN, cache_k_win[i], cache_v_win[i], state_ret[i], wts)
    return (yp, ys, kp[None], vp[None], sp[None], ks[None], vs[None], ss[None])
```

```python
import functools
import math

import jax
import jax.numpy as jnp
from jax import lax
from jax.experimental import pallas as pl
from jax.experimental.pallas import tpu as pltpu

F32 = jnp.float32
BF16 = jnp.bfloat16

D_MODEL = 2048
PAST_LEN = 16384
HEAD_DIM_A = 64
N_HEADS_A = 16
N_KV_A = 4
GROUP_A = 4
WINDOW = 128
ATTN_BLOCK = 128
N_HEADS_R = 8
DK_R = 64
DV_R = 128
RET_CHUNK = 128
PEER_HEADS = 8
N_KEYS = 128
PEER_TOPK = 16
D_KEY_HALF = 128
N_SEL = PEER_HEADS * PEER_TOPK
PEER_CHUNKS = 4
ROPE_THETA = 10000.0
EPS = 1e-6
NEG_INF = -1e30

LANES = 128
SUBLANES = 8
VMEM_LIMIT_BYTES = 56 * 1024 * 1024

_REF_FIELDS = (("q_a", 1024), ("k_a", 256), ("v_a", 256), ("q_r", 512), ("k_r", 512),
               ("v_r", 1024), ("g_r", 1024), ("gate_a", 2048), ("gate_r", 2048))
_NEW_ORDER = ("gate_a", "gate_r", "q_a", "v_r", "g_r", "q_r", "k_r", "k_a", "v_a")
_ROPE_FIELDS = ("q_a", "q_r", "k_r", "k_a")
PROJ_TILE = 256


def _layout():
    ref_off, o = {}, 0
    for name, w in _REF_FIELDS:
        ref_off[name] = (o, w)
        o += w
    new_off, o = {}, 0
    for name in _NEW_ORDER:
        new_off[name] = (o, ref_off[name][1])
        o += ref_off[name][1]
    return new_off, ref_off, o


COL, _REF_COL, IN_COLS = _layout()


def _blk(name, width):
    off, w = COL[name]
    assert off % width == 0 and w % width == 0
    return off // width


def _params(sem, vmem=VMEM_LIMIT_BYTES):
    return pltpu.CompilerParams(dimension_semantics=sem, vmem_limit_bytes=vmem)


def _rms(x, g):
    return x * lax.rsqrt(jnp.mean(x * x, axis=-1, keepdims=True) + EPS) * g


def _sigmoid(x):
    return 1.0 / (1.0 + jnp.exp(-x))


def _dot(a, b):
    return jnp.dot(a, b, preferred_element_type=F32)


def _dot_nt(a, b):
    return lax.dot_general(a, b, (((1,), (1,)), ((), ())), preferred_element_type=F32)


def _dot_tn(a, b):
    return lax.dot_general(a, b, (((0,), (0,)), ((), ())), preferred_element_type=F32)


def _in_proj_kernel(x_ref, g_ref, w_ref, cos_ref, sin_ref, cs_ref, o_ref, h_ref, *, rope_ranges):
    j = pl.program_id(1)

    @pl.when(j == 0)
    def _():
        h_ref[...] = _rms(x_ref[...], g_ref[...]).astype(BF16)

    acc = _dot(h_ref[...], w_ref[...])
    is_rope = functools.reduce(jnp.logical_or, [(j >= a) & (j < b) for a, b in rope_ranges])

    @pl.when(is_rope)
    def _():
        tn = acc.shape[1]
        lane = lax.broadcasted_iota(jnp.int32, acc.shape, 1)
        first_half = (lane % HEAD_DIM_A) < (HEAD_DIM_A // 2)
        partner = jnp.where(first_half, pltpu.roll(acc, tn - HEAD_DIM_A // 2, 1), pltpu.roll(acc, HEAD_DIM_A // 2, 1))
        o_ref[...] = (acc * cos_ref[...] + partner * sin_ref[...]) * cs_ref[...]

    @pl.when(jnp.logical_not(is_rope))
    def _():
        o_ref[...] = acc


def _in_proj(x2d, ln, w, cos, sin, colscale, tm):
    n = x2d.shape[0]
    rt = cos.shape[0] // tm
    rope_ranges = tuple((COL[f][0] // PROJ_TILE, (COL[f][0] + COL[f][1]) // PROJ_TILE) for f in _ROPE_FIELDS)
    return pl.pallas_call(
        functools.partial(_in_proj_kernel, rope_ranges=rope_ranges),
        out_shape=jax.ShapeDtypeStruct((n, IN_COLS), F32),
        grid=(n // tm, IN_COLS // PROJ_TILE),
        in_specs=[
            pl.BlockSpec((tm, D_MODEL), lambda i, j: (i, 0)),
            pl.BlockSpec((1, D_MODEL), lambda i, j: (0, 0)),
            pl.BlockSpec((D_MODEL, PROJ_TILE), lambda i, j: (0, j)),
            pl.BlockSpec((tm, PROJ_TILE), lambda i, j: (i % rt, 0)),
            pl.BlockSpec((tm, PROJ_TILE), lambda i, j: (i % rt, 0)),
            pl.BlockSpec((1, PROJ_TILE), lambda i, j: (0, j)),
        ],
        out_specs=pl.BlockSpec((tm, PROJ_TILE), lambda i, j: (i, j)),
        scratch_shapes=[pltpu.VMEM((tm, D_MODEL), BF16)],
        compiler_params=_params(("arbitrary", "arbitrary")),
        name="in_proj",
    )(x2d, ln, w, cos, sin, colscale)


def _attn_kernel(sink_ref, q_ref, kc_ref, vc_ref, kx_ref, vx_ref, o_ref, *, first_ctx_invalid):
    bt, c, _ = q_ref.shape
    rows = GROUP_A * c
    qi = lax.broadcasted_iota(jnp.int32, (rows, WINDOW), 0) % c
    kj = lax.broadcasted_iota(jnp.int32, (rows, WINDOW), 1)
    ctx_ok = kj >= qi
    if first_ctx_invalid:
        ctx_ok = jnp.logical_and(ctx_ok, pl.program_id(1) > 0)
    qi_c = lax.broadcasted_iota(jnp.int32, (rows, c), 0) % c
    kj_c = lax.broadcasted_iota(jnp.int32, (rows, c), 1)
    cur_ok = kj_c <= qi_c
    grp = lax.broadcasted_iota(jnp.int32, (rows, 1), 0) // c
    scale = HEAD_DIM_A ** -0.5

    def one(b):
        q = q_ref[b]
        kc, vc, kx, vx = kc_ref[b], vc_ref[b], kx_ref[b], vx_ref[b]
        outs = [None] * N_HEADS_A
        for kh in range(N_KV_A):
            sl = slice(kh * HEAD_DIM_A, (kh + 1) * HEAD_DIM_A)
            q4 = jnp.concatenate(
                [q[:, (kh * GROUP_A + g) * HEAD_DIM_A:(kh * GROUP_A + g + 1) * HEAD_DIM_A] for g in range(GROUP_A)],
                axis=0).astype(BF16)
            s_x = jnp.where(ctx_ok, _dot_nt(q4, kx[:, sl].astype(BF16)) * scale, NEG_INF)
            s_c = jnp.where(cur_ok, _dot_nt(q4, kc[:, sl].astype(BF16)) * scale, NEG_INF)
            sink = jnp.zeros((rows, 1), F32)
            for g in range(GROUP_A):
                sink = jnp.where(grp == g, sink_ref[kh * GROUP_A + g], sink)
            m = jnp.maximum(jnp.maximum(jnp.max(s_x, axis=-1, keepdims=True), jnp.max(s_c, axis=-1, keepdims=True)), sink)
            e_x = jnp.exp(s_x - m)
            e_c = jnp.exp(s_c - m)
            den = jnp.sum(e_x, axis=-1, keepdims=True) + jnp.sum(e_c, axis=-1, keepdims=True) + jnp.exp(sink - m)
            o4 = _dot((e_x / den).astype(BF16), vx[:, sl].astype(BF16)) + _dot((e_c / den).astype(BF16), vc[:, sl].astype(BF16))
            for g in range(GROUP_A):
                outs[kh * GROUP_A + g] = o4[g * c:(g + 1) * c]
        o_ref[b] = jnp.concatenate(outs, axis=1).astype(o_ref.dtype)

    for b in range(bt):
        one(b)


def _attention(sinks, proj3, ctx_k, ctx_v, bt, c, prompt):
    bn, t, _ = proj3.shape
    qb, kb, vb = _blk("q_a", 1024), _blk("k_a", 256), _blk("v_a", 256)
    if prompt:
        ctx_specs = [pl.BlockSpec((bt, WINDOW, 256), lambda b, n: (b, jnp.maximum(n - 1, 0), kb)),
                     pl.BlockSpec((bt, WINDOW, 256), lambda b, n: (b, jnp.maximum(n - 1, 0), vb))]
    else:
        ctx_specs = [pl.BlockSpec((bt, WINDOW, 256), lambda b, n: (b, 0, 0)),
                     pl.BlockSpec((bt, WINDOW, 256), lambda b, n: (b, 0, 0))]
    return pl.pallas_call(
        functools.partial(_attn_kernel, first_ctx_invalid=prompt),
        out_shape=jax.ShapeDtypeStruct((bn, t, N_HEADS_A * HEAD_DIM_A), BF16),
        grid=(bn // bt, t // c),
        in_specs=[
            pl.BlockSpec(memory_space=pltpu.SMEM),
            pl.BlockSpec((bt, c, 1024), lambda b, n: (b, n, qb)),
            pl.BlockSpec((bt, c, 256), lambda b, n: (b, n, kb)),
            pl.BlockSpec((bt, c, 256), lambda b, n: (b, n, vb)),
        ] + ctx_specs,
        out_specs=pl.BlockSpec((bt, c, 1024), lambda b, n: (b, n, 0)),
        compiler_params=_params(("arbitrary", "arbitrary")),
        name="attn_prompt" if prompt else "attn_sample",
    )(sinks, proj3, proj3, proj3, ctx_k, ctx_v)


def _ret_kernel(cdec_ref, q_ref, k_ref, v_ref, g_ref, intra_ref, cross_ref, kdec_ref, ln_ref, *rest, has_s0):
    if has_s0:
        s0_ref, o_ref, sout_ref, s_ref = rest
    else:
        o_ref, sout_ref, s_ref = rest
    bt = q_ref.shape[0]
    ci = pl.program_id(1)

    @pl.when(ci == 0)
    def _():
        s_ref[...] = s0_ref[...] if has_s0 else jnp.zeros(s_ref.shape, F32)

    def one(b):
        q, k, v, gt = q_ref[b], k_ref[b], v_ref[b], g_ref[b]
        outs = []
        for h in range(N_HEADS_R):
            qh = q[:, h * DK_R:(h + 1) * DK_R].astype(BF16)
            kh = k[:, h * DK_R:(h + 1) * DK_R]
            vh = v[:, h * DV_R:(h + 1) * DV_R].astype(BF16)
            s = s_ref[b, h]
            att = _dot_nt(qh, kh.astype(BF16)) * intra_ref[h]
            o = _dot(att.astype(BF16), vh) + _dot(qh, s.astype(BF16)) * cross_ref[h]
            s_ref[b, h] = s * cdec_ref[h] + _dot_tn((kh * kdec_ref[h]).astype(BF16), vh)
            mu = jnp.mean(o, axis=-1, keepdims=True)
            d = o - mu
            var = jnp.mean(d * d, axis=-1, keepdims=True)
            gh = gt[:, h * DV_R:(h + 1) * DV_R]
            outs.append(d * lax.rsqrt(var + EPS) * ln_ref[h:h + 1, :] * (gh * _sigmoid(gh)))
        o_ref[b] = jnp.concatenate(outs, axis=1).astype(o_ref.dtype)

    for b in range(bt):
        one(b)

    @pl.when(ci == pl.num_programs(1) - 1)
    def _():
        sout_ref[...] = s_ref[...]


def _retention(proj3, s0, ln_ret, bt, c):
    bn, t, _ = proj3.shape
    log_g = jnp.log1p(-jnp.exp2(-5.0 - jnp.arange(N_HEADS_R, dtype=F32)))
    i = jnp.arange(c, dtype=F32)
    diff = i[:, None] - i[None, :]
    intra = jnp.where(diff[None] >= 0, jnp.exp(jnp.maximum(diff, 0.0)[None] * log_g[:, None, None]), 0.0)
    cross = jnp.exp((i + 1.0)[None, :] * log_g[:, None])
    kdec = jnp.exp((c - 1.0 - i)[None, :] * log_g[:, None])
    cdec = jnp.exp(c * log_g)
    cross_b = jnp.broadcast_to(cross[:, :, None], (N_HEADS_R, c, DV_R))
    kdec_b = jnp.broadcast_to(kdec[:, :, None], (N_HEADS_R, c, DK_R))
    has_s0 = s0 is not None
    const3 = lambda b, n: (0, 0, 0)
    state_spec = pl.BlockSpec((bt, N_HEADS_R, DK_R, DV_R), lambda b, n: (b, 0, 0, 0))
    in_specs = [
        pl.BlockSpec(memory_space=pltpu.SMEM),
        pl.BlockSpec((bt, c, 512), lambda b, n: (b, n, _blk("q_r", 512))),
        pl.BlockSpec((bt, c, 512), lambda b, n: (b, n, _blk("k_r", 512))),
        pl.BlockSpec((bt, c, 1024), lambda b, n: (b, n, _blk("v_r", 1024))),
        pl.BlockSpec((bt, c, 1024), lambda b, n: (b, n, _blk("g_r", 1024))),
        pl.BlockSpec((N_HEADS_R, c, c), const3),
        pl.BlockSpec((N_HEADS_R, c, DV_R), const3),
        pl.BlockSpec((N_HEADS_R, c, DK_R), const3),
        pl.BlockSpec((N_HEADS_R, DV_R), lambda b, n: (0, 0)),
    ]
    args = [cdec, proj3, proj3, proj3, proj3, intra, cross_b, kdec_b, ln_ret]
    if has_s0:
        in_specs.append(state_spec)
        args.append(s0)
    return pl.pallas_call(
        functools.partial(_ret_kernel, has_s0=has_s0),
        out_shape=(jax.ShapeDtypeStruct((bn, t, N_HEADS_R * DV_R), BF16),
                   jax.ShapeDtypeStruct((bn, N_HEADS_R, DK_R, DV_R), F32)),
        grid=(bn // bt, t // c),
        in_specs=in_specs,
        out_specs=(pl.BlockSpec((bt, c, 1024), lambda b, n: (b, n, 0)), state_spec),
        scratch_shapes=[pltpu.VMEM((bt, N_HEADS_R, DK_R, DV_R), F32)],
        compiler_params=_params(("arbitrary", "arbitrary")),
        name="ret_sample" if has_s0 else "ret_prompt",
    )(*args)


def _merge_kernel(oa_ref, or_ref, ga_ref, gr_ref, wa_ref, wr_ref, m_ref):
    br_a = _dot(oa_ref[...], wa_ref[...])
    br_r = _dot(or_ref[...], wr_ref[...])
    m_ref[...] = (_sigmoid(ga_ref[...]) * br_a + _sigmoid(gr_ref[...]) * br_r).astype(m_ref.dtype)


def _merge(o_a, o_r, proj, w_a, w_r, tm, tn=512):
    n = o_a.shape[0]
    ga, gr = _blk("gate_a", tn), _blk("gate_r", tn)
    return pl.pallas_call(
        _merge_kernel,
        out_shape=jax.ShapeDtypeStruct((n, D_MODEL), BF16),
        grid=(n // tm, D_MODEL // tn),
        in_specs=[
            pl.BlockSpec((tm, 1024), lambda i, j: (i, 0)),
            pl.BlockSpec((tm, 1024), lambda i, j: (i, 0)),
            pl.BlockSpec((tm, tn), lambda i, j: (i, ga + j)),
            pl.BlockSpec((tm, tn), lambda i, j: (i, gr + j)),
            pl.BlockSpec((1024, tn), lambda i, j: (0, j)),
            pl.BlockSpec((1024, tn), lambda i, j: (0, j)),
        ],
        out_specs=pl.BlockSpec((tm, tn), lambda i, j: (i, j)),
        compiler_params=_params(("arbitrary", "arbitrary")),
        name="merge",
    )(o_a, o_r, proj, proj, w_a, w_r)


def _out_proj_kernel(x_ref, m_ref, w_ref, o_ref):
    o_ref[...] = x_ref[...] + _dot(m_ref[...], w_ref[...])


def _out_proj(x2d, m, w_o, tm, tn=512):
    n = x2d.shape[0]
    return pl.pallas_call(
        _out_proj_kernel,
        out_shape=jax.ShapeDtypeStruct((n, D_MODEL), F32),
        grid=(n // tm, D_MODEL // tn),
        in_specs=[
            pl.BlockSpec((tm, tn), lambda i, j: (i, j)),
            pl.BlockSpec((tm, D_MODEL), lambda i, j: (i, 0)),
            pl.BlockSpec((D_MODEL, tn), lambda i, j: (0, j)),
        ],
        out_specs=pl.BlockSpec((tm, tn), lambda i, j: (i, j)),
        compiler_params=_params(("arbitrary", "arbitrary")),
        name="out_proj",
    )(x2d, m, w_o)


def _scores_kernel(x_ref, g_ref, w_ref, sk_ref, h_out_ref, s_ref, h_ref):
    j = pl.program_id(1)

    @pl.when(j == 0)
    def _():
        h = _rms(x_ref[...], g_ref[...])
        h_ref[...] = h.astype(BF16)
        h_out_ref[...] = h

    qry = _dot(h_ref[...], w_ref[...]).astype(BF16)
    for g in range(qry.shape[1] // D_KEY_HALF):
        s_ref[g] = _dot_nt(sk_ref[g % 2], qry[:, g * D_KEY_HALF:(g + 1) * D_KEY_HALF])


def _scores(x2d, ln, w_q, sub_keys, tm, tn=512):
    n = x2d.shape[0]
    ng = tn // D_KEY_HALF
    return pl.pallas_call(
        _scores_kernel,
        out_shape=(jax.ShapeDtypeStruct((n, D_MODEL), F32),
                   jax.ShapeDtypeStruct((2 * PEER_HEADS, N_KEYS, n), F32)),
        grid=(n // tm, D_MODEL // tn),
        in_specs=[
            pl.BlockSpec((tm, D_MODEL), lambda i, j: (i, 0)),
            pl.BlockSpec((1, D_MODEL), lambda i, j: (0, 0)),
            pl.BlockSpec((D_MODEL, tn), lambda i, j: (0, j)),
            pl.BlockSpec((2, N_KEYS, D_KEY_HALF), lambda i, j: (0, 0, 0)),
        ],
        out_specs=(pl.BlockSpec((tm, D_MODEL), lambda i, j: (i, 0)),
                   pl.BlockSpec((ng, N_KEYS, tm), lambda i, j: (j, 0, i))),
        scratch_shapes=[pltpu.VMEM((tm, D_MODEL), BF16)],
        compiler_params=_params(("arbitrary", "arbitrary")),
        name="peer_scores",
    )(x2d, ln, w_q, sub_keys)


def _take_top(vals, iota, count, fill, payload=None):
    n_rows = vals.shape[0]
    top_v, top_i = [], []
    for _ in range(count):
        m = jnp.max(vals, axis=0, keepdims=True)
        pos = jnp.min(jnp.where(vals == m, iota, n_rows), axis=0, keepdims=True)
        sel = iota == pos
        top_v.append(m)
        top_i.append(pos if payload is None else jnp.max(jnp.where(sel, payload, -1), axis=0, keepdims=True))
        vals = jnp.where(sel, fill, vals)
    return jnp.concatenate(top_v, axis=0), jnp.concatenate(top_i, axis=0)


_COMB_PIECES = tuple((i, PEER_TOPK // (i + 1)) for i in range(SUBLANES))
_COMB_ROWS = sum(max(nj, SUBLANES) for _, nj in _COMB_PIECES) + SUBLANES


def _topk_kernel(s_ref, e_ref, g_ref):
    tt = s_ref.shape[2]
    key_iota = lax.broadcasted_iota(jnp.int32, (N_KEYS, tt), 0)
    comb_iota = lax.broadcasted_iota(jnp.int32, (_COMB_ROWS, tt), 0)
    ninf = float("-inf")

    def head(h, carry):
        s0, i0 = _take_top(s_ref[2 * h], key_iota, PEER_TOPK, ninf)
        s1, i1 = _take_top(s_ref[2 * h + 1], key_iota, PEER_TOPK, ninf)
        comb, cidx = [], []
        for i, nj in _COMB_PIECES:
            rows = max(nj, SUBLANES)
            sv = s0[i:i + 1, :] + s1[:rows]
            iv = i0[i:i + 1, :] * N_KEYS + i1[:rows]
            if nj < rows:
                keep = lax.broadcasted_iota(jnp.int32, (rows, tt), 0) < nj
                sv, iv = jnp.where(keep, sv, ninf), jnp.where(keep, iv, -1)
            comb.append(sv)
            cidx.append(iv)
        comb.append(s0[SUBLANES:] + s1[0:1, :])
        cidx.append(i0[SUBLANES:] * N_KEYS + i1[0:1, :])
        comb, cidx = jnp.concatenate(comb, axis=0), jnp.concatenate(cidx, axis=0)
        best, eidx = _take_top(comb, comb_iota, PEER_TOPK, ninf, payload=cidx)
        ex = jnp.exp(best - jnp.max(best, axis=0, keepdims=True))
        g_ref[h] = ex / jnp.sum(ex, axis=0, keepdims=True)
        e_ref[h] = eidx
        return carry

    lax.fori_loop(0, PEER_HEADS, head, 0)


def _topk(scores_t, tt=256):
    n = scores_t.shape[2]
    out_spec = pl.BlockSpec((PEER_HEADS, PEER_TOPK, tt), lambda i: (0, 0, i))
    return pl.pallas_call(
        _topk_kernel,
        out_shape=(jax.ShapeDtypeStruct((PEER_HEADS, PEER_TOPK, n), jnp.int32),
                   jax.ShapeDtypeStruct((PEER_HEADS, PEER_TOPK, n), F32)),
        grid=(n // tt,),
        in_specs=[pl.BlockSpec((2 * PEER_HEADS, N_KEYS, tt), lambda i: (0, 0, i))],
        out_specs=(out_spec, out_spec),
        compiler_params=_params(("arbitrary",)),
        name="peer_topk",
    )(scores_t)


def _gelu(a):
    return 0.5 * a * (1.0 + lax.erf(a * (2.0 ** -0.5)))


def _peer_kernel(idx_hbm, h_ref, g2_ref, x_ref, uv_hbm, o_ref, idx_smem, buf, sem_i, sem):
    tt = h_ref.shape[0]
    half = D_MODEL // 2
    step = pl.program_id(0)
    more = step + 1 < pl.num_programs(0)
    cur = step % 2

    def idx_copy(s, islot):
        return pltpu.make_async_copy(idx_hbm.at[pl.ds(s * tt, tt), :], idx_smem.at[islot], sem_i.at[islot])

    n_groups = tt // SUBLANES
    n_parts = 2 * PEER_CHUNKS
    cw = half // PEER_CHUNKS

    def gather_start(islot, t, bank, j, part=None):
        ks = range(N_SEL) if part is None else range(part * N_SEL // n_parts, (part + 1) * N_SEL // n_parts)
        for k in ks:
            e = idx_smem[islot, t, k]
            pltpu.make_async_copy(uv_hbm.at[e], buf.at[bank, j, pl.ds(k, 1), :],
                                  sem.at[bank]).start(priority=k % 2)

    def gather_wait(bank):
        pltpu.make_async_copy(buf.at[bank], buf.at[bank], sem.at[bank]).wait()

    @pl.when(step == 0)
    def _():
        first = idx_copy(0, 0)
        first.start()
        first.wait()
        for j in range(SUBLANES):
            gather_start(0, j, 0, j)

    @pl.when(more)
    def _():
        idx_copy(step + 1, 1 - cur).start()

    row_iota = lax.broadcasted_iota(jnp.int32, (SUBLANES, 1), 0)
    even = lax.broadcasted_iota(jnp.int32, (SUBLANES, 2 * N_SEL), 1) % 2 == 0

    def group(gi, carry):
        r0 = pl.multiple_of(gi * SUBLANES, SUBLANES)
        last = gi == n_groups - 1
        bank = gi % 2

        @pl.when(jnp.logical_and(last, more))
        def _():
            idx_copy(step + 1, 1 - cur).wait()

        n_islot = jnp.where(jnp.logical_and(last, more), 1 - cur, cur)
        n_t0 = jnp.where(last, 0, r0 + SUBLANES)
        gather_wait(bank)
        h8 = h_ref[pl.ds(r0, SUBLANES), :]
        g2 = g2_ref[pl.ds(r0, SUBLANES), :]
        acc = [jnp.zeros((2 * SUBLANES, cw), F32) for _ in range(PEER_CHUNKS)]
        issued = [0]

        def request():
            q = issued[0]
            issued[0] += 1
            gather_start(n_islot, n_t0 + q // n_parts, 1 - bank, q // n_parts, part=q % n_parts)

        def u_dot(r, c, p):
            sel = row_iota == r
            hm = jnp.concatenate([jnp.where(sel, h8[:, c * cw:(c + 1) * cw], 0.0),
                                  jnp.where(sel, h8[:, half + c * cw:half + (c + 1) * cw], 0.0)],
                                 axis=0).astype(BF16)
            ub = pltpu.bitcast(buf[bank, r, :, c * cw:(c + 1) * cw], BF16)
            return p + _dot_nt(hm, ub)

        def mix_weights(p):
            part = jnp.where(even, p[:SUBLANES], p[SUBLANES:])
            a2 = part + jnp.where(even, pltpu.roll(part, 2 * N_SEL - 1, 1), pltpu.roll(part, 1, 1))
            w2 = _gelu(a2) * g2
            return jnp.concatenate([jnp.where(even, w2, 0.0), jnp.where(even, 0.0, w2)], axis=0).astype(BF16)

        def v_dot(r, c, wm):
            vb = pltpu.bitcast(buf[bank, r, :, half + c * cw:half + (c + 1) * cw], BF16)
            acc[c] = acc[c] + _dot(wm, vb)

        p_next = jnp.zeros((2 * SUBLANES, 2 * N_SEL), F32)
        for c in range(PEER_CHUNKS):
            request()
            p_next = u_dot(0, c, p_next)
        for r in range(SUBLANES):
            wm = mix_weights(p_next)
            p_next = jnp.zeros((2 * SUBLANES, 2 * N_SEL), F32)
            for c in range(PEER_CHUNKS):
                if r + 1 < SUBLANES:
                    request()
                    p_next = u_dot(r + 1, c, p_next)
                request()
                v_dot(r, c, wm)
        assert issued[0] == SUBLANES * n_parts
        peer = jnp.concatenate([a[:SUBLANES] for a in acc] + [a[SUBLANES:] for a in acc], axis=1)
        o_ref[pl.ds(r0, SUBLANES), :] = x_ref[pl.ds(r0, SUBLANES), :] + peer
        return carry

    lax.fori_loop(0, n_groups, group, 0)

    @pl.when(jnp.logical_not(more))
    def _():
        gather_wait(n_groups % 2)


def _pack_rows(tab):
    bits = lax.bitcast_convert_type(tab.astype(BF16), jnp.uint16).astype(jnp.uint32)
    half = tab.shape[-1] // 2
    return bits[..., :half] | (bits[..., half:] << 16)


def _peer(eidx, h2, gate2, x2d, uv, tt):
    n = x2d.shape[0]
    assert tt % (2 * SUBLANES) == 0
    row = lambda i: (i, 0)
    return pl.pallas_call(
        _peer_kernel,
        out_shape=jax.ShapeDtypeStruct((n, D_MODEL), F32),
        grid=(n // tt,),
        in_specs=[
            pl.BlockSpec(memory_space=pl.ANY),
            pl.BlockSpec((tt, D_MODEL), row),
            pl.BlockSpec((tt, 2 * N_SEL), row),
            pl.BlockSpec((tt, D_MODEL), row),
            pl.BlockSpec(memory_space=pl.ANY),
        ],
        out_specs=pl.BlockSpec((tt, D_MODEL), row),
        scratch_shapes=[
            pltpu.SMEM((2, tt, N_SEL), jnp.int32),
            pltpu.VMEM((2, SUBLANES, N_SEL, D_MODEL), jnp.uint32),
            pltpu.SemaphoreType.DMA((2,)),
            pltpu.SemaphoreType.DMA((2,)),
        ],
        compiler_params=_params(("arbitrary",)),
        name="peer_mix",
    )(eidx, h2, gate2, x2d, uv)


def _ple_kernel(x_ref, p_ref, lnp_ref, wg_ref, wp_ref, lnf_ref, y_ref):
    x = x_ref[...]
    gate = _sigmoid(_dot(_rms(x, lnp_ref[...]).astype(BF16), wg_ref[...]))
    x = x + gate * _dot(p_ref[...].astype(BF16), wp_ref[...])
    y_ref[...] = _rms(x, lnf_ref[...])


def _ple(x2d, p2d, ln_ple, w_gate, w_proj, ln_final, tm):
    n = x2d.shape[0]
    pd = p2d.shape[1]
    row = lambda i: (i, 0)
    const = lambda i: (0, 0)
    return pl.pallas_call(
        _ple_kernel,
        out_shape=jax.ShapeDtypeStruct((n, D_MODEL), F32),
        grid=(n // tm,),
        in_specs=[
            pl.BlockSpec((tm, D_MODEL), row),
            pl.BlockSpec((tm, pd), row),
            pl.BlockSpec((1, D_MODEL), const),
            pl.BlockSpec((D_MODEL, D_MODEL), const),
            pl.BlockSpec((pd, D_MODEL), const),
            pl.BlockSpec((1, D_MODEL), const),
        ],
        out_specs=pl.BlockSpec((tm, D_MODEL), row),
        compiler_params=_params(("arbitrary",)),
        name="ple_final",
    )(x2d, p2d, ln_ple, w_gate, w_proj, ln_final)


def _rope_tables(pos):
    inv = ROPE_THETA ** (-jnp.arange(0, HEAD_DIM_A, 2, dtype=F32) / HEAD_DIM_A)
    ang = pos.astype(F32)[:, None] * inv[None, :]
    cos, sin = jnp.cos(ang), jnp.sin(ang)
    reps = PROJ_TILE // HEAD_DIM_A
    return (jnp.tile(jnp.concatenate([cos, cos], axis=1), (1, reps)),
            jnp.tile(jnp.concatenate([-sin, sin], axis=1), (1, reps)))


def _pick(n, prefs):
    for c in prefs:
        if n % c == 0:
            return c
    raise ValueError(f"no tile of {prefs} divides {n}")


def _group(x, pe, pos0, cache_k, cache_v, s0, wts):
    bn, t, _ = x.shape
    n = bn * t
    prompt = cache_k is None
    x2d = x.reshape(n, D_MODEL)
    tm = _pick(n, (1024, 512, 256, 128))

    pos = pos0 + jnp.arange(t, dtype=jnp.int32)
    cos, sin = _rope_tables(pos)
    if t < tm:
        cos, sin = jnp.tile(cos, (tm // t, 1)), jnp.tile(sin, (tm // t, 1))
    proj = _in_proj(x2d, wts["ln1"], wts["w_in"], cos, sin, wts["colscale"], tm)
    proj3 = proj.reshape(bn, t, IN_COLS)

    if prompt:
        c_att, bt_att = ATTN_BLOCK, 1
        ctx_k = ctx_v = proj3
    else:
        c_att, bt_att = t, _pick(bn, (8, 4, 2, 1))
        ctx_k = cache_k.reshape(bn, WINDOW, N_KV_A * HEAD_DIM_A)
        ctx_v = cache_v.reshape(bn, WINDOW, N_KV_A * HEAD_DIM_A)
    o_a = _attention(wts["sinks"], proj3, ctx_k, ctx_v, bt_att, c_att, prompt)

    c_ret = math.gcd(t, RET_CHUNK)
    bt_ret = 1 if prompt else _pick(bn, (8, 4, 2, 1))
    o_r, s_new = _retention(proj3, s0, wts["ln_ret"], bt_ret, c_ret)

    m = _merge(o_a.reshape(n, -1), o_r.reshape(n, -1), proj, wts["w_br_a"], wts["w_br_r"], tm)
    x2 = _out_proj(x2d, m, wts["w_o"], tm)

    h2, scores_t = _scores(x2, wts["ln2"], wts["w_q"], wts["sub_keys"], _pick(n, (512, 256, 128)))
    eidx_t, gate_t = _topk(scores_t, _pick(n, (256, 128)))
    eidx = eidx_t.reshape(N_SEL, n).T
    gate2 = jnp.repeat(gate_t.reshape(N_SEL, n).T, 2, axis=1)
    x3 = _peer(eidx, h2, gate2, x2, wts["peer_uv"], _pick(n, (64, 32, 16, 8)))

    y = _ple(x3, pe.reshape(n, -1), wts["ln_ple"], wts["w_ple_gate"], wts["w_ple_proj"], wts["ln_final"],
             _pick(n, (256, 128)))

    k_off, v_off = COL["k_a"][0], COL["v_a"][0]
    kv_w = N_KV_A * HEAD_DIM_A
    k_new, v_new = proj3[:, :, k_off:k_off + kv_w], proj3[:, :, v_off:v_off + kv_w]
    if prompt:
        k_win, v_win = k_new[:, -WINDOW:], v_new[:, -WINDOW:]
    else:
        k_win = jnp.concatenate([ctx_k, k_new], axis=1)[:, -WINDOW:]
        v_win = jnp.concatenate([ctx_v, v_new], axis=1)[:, -WINDOW:]
    shp = (bn, WINDOW, N_KV_A, HEAD_DIM_A)
    return y.reshape(bn, t, D_MODEL), k_win.reshape(shp), v_win.reshape(shp), s_new


def kernel(x_prompt, x_sample, cache_k_win, cache_v_win, state_ret, p_prompt, p_sample, ln1, w_in, attn_sinks, ln_ret, w_branch_attn, w_branch_ret, w_out, ln2, w_peer_query, peer_sub_keys, peer_u, peer_v, ln_ple, w_ple_gate, w_ple_proj, ln_final):
    depth = ln1.shape[0]
    assert depth == 1, "single-layer step"
    i = 0
    colscale = jnp.ones((IN_COLS,), F32).at[COL["k_r"][0]:COL["k_r"][0] + COL["k_r"][1]].set(DK_R ** -0.5)
    wts = dict(
        ln1=ln1[i][None, :],
        w_in=jnp.concatenate([w_in[i][:, _REF_COL[f][0]:_REF_COL[f][0] + _REF_COL[f][1]] for f in _NEW_ORDER],
                             axis=1).astype(BF16),
        colscale=colscale[None, :],
        sinks=attn_sinks[i],
        ln_ret=ln_ret[i].reshape(N_HEADS_R, DV_R),
        w_br_a=w_branch_attn[i].astype(BF16),
        w_br_r=w_branch_ret[i].astype(BF16),
        w_o=w_out[i].astype(BF16),
        ln2=ln2[i][None, :],
        w_q=w_peer_query[i].astype(BF16),
        sub_keys=peer_sub_keys[i].astype(BF16),
        peer_uv=jnp.concatenate([_pack_rows(peer_u[i][:, None, :]), _pack_rows(peer_v[i][:, None, :])], axis=2),
        ln_ple=ln_ple[i][None, :],
        w_ple_gate=w_ple_gate[i].astype(BF16),
        w_ple_proj=w_ple_proj[i].astype(BF16),
        ln_final=ln_final[None, :],
    )
    yp, kp, vp, sp = _group(x_prompt, p_prompt[i], 0, None, None, None, wts)
    ys, ks, vs, ss = _group(x_sample, p_sample[i], PAST_LEN, cache_k_win[i], cache_v_win[i], state_ret[i], wts)
    return (yp, ys, kp[None], vp[None], sp[None], ks[None], vs[None], ss[None])
```

```python
import functools
import math

import jax
import jax.numpy as jnp
from jax import lax
from jax.experimental import pallas as pl
from jax.experimental.pallas import tpu as pltpu

F32 = jnp.float32
BF16 = jnp.bfloat16

D_MODEL = 2048
PAST_LEN = 16384
HEAD_DIM_A = 64
N_HEADS_A = 16
N_KV_A = 4
GROUP_A = 4
WINDOW = 128
ATTN_BLOCK = 128
N_HEADS_R = 8
DK_R = 64
DV_R = 128
RET_CHUNK = 128
PEER_HEADS = 8
N_KEYS = 128
PEER_TOPK = 16
D_KEY_HALF = 128
N_SEL = PEER_HEADS * PEER_TOPK
PEER_CHUNKS = 4
ROPE_THETA = 10000.0
EPS = 1e-6
NEG_INF = -1e30

LANES = 128
SUBLANES = 8
VMEM_LIMIT_BYTES = 56 * 1024 * 1024

_REF_FIELDS = (("q_a", 1024), ("k_a", 256), ("v_a", 256), ("q_r", 512), ("k_r", 512),
               ("v_r", 1024), ("g_r", 1024), ("gate_a", 2048), ("gate_r", 2048))
_NEW_ORDER = ("gate_a", "gate_r", "q_a", "v_r", "g_r", "q_r", "k_r", "k_a", "v_a")
_ROPE_FIELDS = ("q_a", "q_r", "k_r", "k_a")
PROJ_TILE = 512


def _layout():
    ref_off, o = {}, 0
    for name, w in _REF_FIELDS:
        ref_off[name] = (o, w)
        o += w
    new_off, o = {}, 0
    for name in _NEW_ORDER:
        new_off[name] = (o, ref_off[name][1])
        o += ref_off[name][1]
    return new_off, ref_off, o


COL, _REF_COL, IN_COLS = _layout()


def _blk(name, width):
    off, w = COL[name]
    assert off % width == 0 and w % width == 0
    return off // width


def _params(sem, vmem=VMEM_LIMIT_BYTES):
    return pltpu.CompilerParams(dimension_semantics=sem, vmem_limit_bytes=vmem)


def _rms(x, g):
    return x * lax.rsqrt(jnp.mean(x * x, axis=-1, keepdims=True) + EPS) * g


def _sigmoid(x):
    return 1.0 / (1.0 + jnp.exp(-x))


def _dot(a, b):
    return jnp.dot(a, b, preferred_element_type=F32)


def _dot_nt(a, b):
    return lax.dot_general(a, b, (((1,), (1,)), ((), ())), preferred_element_type=F32)


def _dot_tn(a, b):
    return lax.dot_general(a, b, (((0,), (0,)), ((), ())), preferred_element_type=F32)


def _in_proj_kernel(x_ref, g_ref, w_ref, cos_ref, sin_ref, cs_ref, o_ref, h_ref, *, rope_ranges):
    j = pl.program_id(1)

    @pl.when(j == 0)
    def _():
        h_ref[...] = _rms(x_ref[...], g_ref[...]).astype(BF16)

    acc = _dot(h_ref[...], w_ref[...])
    is_rope = functools.reduce(jnp.logical_or, [(j >= a) & (j < b) for a, b in rope_ranges])

    @pl.when(is_rope)
    def _():
        tn = acc.shape[1]
        lane = lax.broadcasted_iota(jnp.int32, acc.shape, 1)
        first_half = (lane % HEAD_DIM_A) < (HEAD_DIM_A // 2)
        partner = jnp.where(first_half, pltpu.roll(acc, tn - HEAD_DIM_A // 2, 1), pltpu.roll(acc, HEAD_DIM_A // 2, 1))
        o_ref[...] = (acc * cos_ref[...] + partner * sin_ref[...]) * cs_ref[...]

    @pl.when(jnp.logical_not(is_rope))
    def _():
        o_ref[...] = acc


def _in_proj(x2d, ln, w, cos, sin, colscale, tm):
    n = x2d.shape[0]
    rt = cos.shape[0] // tm
    rope_ranges = tuple((COL[f][0] // PROJ_TILE, -(-(COL[f][0] + COL[f][1]) // PROJ_TILE)) for f in _ROPE_FIELDS)
    ends = [COL[f][0] + COL[f][1] for f in _ROPE_FIELDS if (COL[f][0] + COL[f][1]) % PROJ_TILE]
    assert len(ends) <= 1 and cos.shape[1] == (1 + len(ends)) * PROJ_TILE
    mixed = ends[0] // PROJ_TILE if ends else -1
    table_spec = pl.BlockSpec((tm, PROJ_TILE), lambda i, j: (i % rt, jnp.where(j == mixed, 1, 0)))
    return pl.pallas_call(
        functools.partial(_in_proj_kernel, rope_ranges=rope_ranges),
        out_shape=jax.ShapeDtypeStruct((n, IN_COLS), F32),
        grid=(n // tm, IN_COLS // PROJ_TILE),
        in_specs=[
            pl.BlockSpec((tm, D_MODEL), lambda i, j: (i, 0)),
            pl.BlockSpec((1, D_MODEL), lambda i, j: (0, 0)),
            pl.BlockSpec((D_MODEL, PROJ_TILE), lambda i, j: (0, j)),
            table_spec,
            table_spec,
            pl.BlockSpec((1, PROJ_TILE), lambda i, j: (0, j)),
        ],
        out_specs=pl.BlockSpec((tm, PROJ_TILE), lambda i, j: (i, j)),
        scratch_shapes=[pltpu.VMEM((tm, D_MODEL), BF16)],
        compiler_params=_params(("arbitrary", "arbitrary")),
        name="in_proj",
    )(x2d, ln, w, cos, sin, colscale)


def _attn_kernel(sink_ref, q_ref, kc_ref, vc_ref, kx_ref, vx_ref, o_ref, *, first_ctx_invalid):
    bt, c, _ = q_ref.shape
    rows = GROUP_A * c
    qi = lax.broadcasted_iota(jnp.int32, (rows, WINDOW), 0) % c
    kj = lax.broadcasted_iota(jnp.int32, (rows, WINDOW), 1)
    ctx_ok = kj >= qi
    if first_ctx_invalid:
        ctx_ok = jnp.logical_and(ctx_ok, pl.program_id(1) > 0)
    qi_c = lax.broadcasted_iota(jnp.int32, (rows, c), 0) % c
    kj_c = lax.broadcasted_iota(jnp.int32, (rows, c), 1)
    cur_ok = kj_c <= qi_c
    grp = lax.broadcasted_iota(jnp.int32, (rows, 1), 0) // c
    scale = HEAD_DIM_A ** -0.5

    def one(b):
        q = q_ref[b]
        kc, vc, kx, vx = kc_ref[b], vc_ref[b], kx_ref[b], vx_ref[b]
        outs = [None] * N_HEADS_A
        for kh in range(N_KV_A):
            sl = slice(kh * HEAD_DIM_A, (kh + 1) * HEAD_DIM_A)
            q4 = jnp.concatenate(
                [q[:, (kh * GROUP_A + g) * HEAD_DIM_A:(kh * GROUP_A + g + 1) * HEAD_DIM_A] for g in range(GROUP_A)],
                axis=0).astype(BF16)
            s_x = jnp.where(ctx_ok, _dot_nt(q4, kx[:, sl].astype(BF16)) * scale, NEG_INF)
            s_c = jnp.where(cur_ok, _dot_nt(q4, kc[:, sl].astype(BF16)) * scale, NEG_INF)
            sink = jnp.zeros((rows, 1), F32)
            for g in range(GROUP_A):
                sink = jnp.where(grp == g, sink_ref[kh * GROUP_A + g], sink)
            m = jnp.maximum(jnp.maximum(jnp.max(s_x, axis=-1, keepdims=True), jnp.max(s_c, axis=-1, keepdims=True)), sink)
            e_x = jnp.exp(s_x - m)
            e_c = jnp.exp(s_c - m)
            den = jnp.sum(e_x, axis=-1, keepdims=True) + jnp.sum(e_c, axis=-1, keepdims=True) + jnp.exp(sink - m)
            o4 = _dot((e_x / den).astype(BF16), vx[:, sl].astype(BF16)) + _dot((e_c / den).astype(BF16), vc[:, sl].astype(BF16))
            for g in range(GROUP_A):
                outs[kh * GROUP_A + g] = o4[g * c:(g + 1) * c]
        o_ref[b] = jnp.concatenate(outs, axis=1).astype(o_ref.dtype)

    for b in range(bt):
        one(b)


def _attention(sinks, proj3, ctx_k, ctx_v, bt, c, prompt):
    bn, t, _ = proj3.shape
    qb, kb, vb = _blk("q_a", 1024), _blk("k_a", 256), _blk("v_a", 256)
    if prompt:
        ctx_specs = [pl.BlockSpec((bt, WINDOW, 256), lambda b, n: (b, jnp.maximum(n - 1, 0), kb)),
                     pl.BlockSpec((bt, WINDOW, 256), lambda b, n: (b, jnp.maximum(n - 1, 0), vb))]
    else:
        ctx_specs = [pl.BlockSpec((bt, WINDOW, 256), lambda b, n: (b, 0, 0)),
                     pl.BlockSpec((bt, WINDOW, 256), lambda b, n: (b, 0, 0))]
    return pl.pallas_call(
        functools.partial(_attn_kernel, first_ctx_invalid=prompt),
        out_shape=jax.ShapeDtypeStruct((bn, t, N_HEADS_A * HEAD_DIM_A), BF16),
        grid=(bn // bt, t // c),
        in_specs=[
            pl.BlockSpec(memory_space=pltpu.SMEM),
            pl.BlockSpec((bt, c, 1024), lambda b, n: (b, n, qb)),
            pl.BlockSpec((bt, c, 256), lambda b, n: (b, n, kb)),
            pl.BlockSpec((bt, c, 256), lambda b, n: (b, n, vb)),
        ] + ctx_specs,
        out_specs=pl.BlockSpec((bt, c, 1024), lambda b, n: (b, n, 0)),
        compiler_params=_params(("arbitrary", "arbitrary")),
        name="attn_prompt" if prompt else "attn_sample",
    )(sinks, proj3, proj3, proj3, ctx_k, ctx_v)


def _ret_kernel(cdec_ref, q_ref, k_ref, v_ref, g_ref, intra_ref, cross_ref, kdec_ref, ln_ref, *rest, has_s0):
    if has_s0:
        s0_ref, o_ref, sout_ref, s_ref = rest
    else:
        o_ref, sout_ref, s_ref = rest
    bt = q_ref.shape[0]
    ci = pl.program_id(1)

    @pl.when(ci == 0)
    def _():
        s_ref[...] = s0_ref[...] if has_s0 else jnp.zeros(s_ref.shape, F32)

    def one(b):
        q, k, v, gt = q_ref[b], k_ref[b], v_ref[b], g_ref[b]
        outs = []
        for h in range(N_HEADS_R):
            qh = q[:, h * DK_R:(h + 1) * DK_R].astype(BF16)
            kh = k[:, h * DK_R:(h + 1) * DK_R]
            vh = v[:, h * DV_R:(h + 1) * DV_R].astype(BF16)
            s = s_ref[b, h]
            att = _dot_nt(qh, kh.astype(BF16)) * intra_ref[h]
            o = _dot(att.astype(BF16), vh) + _dot(qh, s.astype(BF16)) * cross_ref[h]
            s_ref[b, h] = s * cdec_ref[h] + _dot_tn((kh * kdec_ref[h]).astype(BF16), vh)
            mu = jnp.mean(o, axis=-1, keepdims=True)
            d = o - mu
            var = jnp.mean(d * d, axis=-1, keepdims=True)
            gh = gt[:, h * DV_R:(h + 1) * DV_R]
            outs.append(d * lax.rsqrt(var + EPS) * ln_ref[h:h + 1, :] * (gh * _sigmoid(gh)))
        o_ref[b] = jnp.concatenate(outs, axis=1).astype(o_ref.dtype)

    for b in range(bt):
        one(b)

    @pl.when(ci == pl.num_programs(1) - 1)
    def _():
        sout_ref[...] = s_ref[...]


def _retention(proj3, s0, ln_ret, bt, c):
    bn, t, _ = proj3.shape
    log_g = jnp.log1p(-jnp.exp2(-5.0 - jnp.arange(N_HEADS_R, dtype=F32)))
    i = jnp.arange(c, dtype=F32)
    diff = i[:, None] - i[None, :]
    intra = jnp.where(diff[None] >= 0, jnp.exp(jnp.maximum(diff, 0.0)[None] * log_g[:, None, None]), 0.0)
    cross = jnp.exp((i + 1.0)[None, :] * log_g[:, None])
    kdec = jnp.exp((c - 1.0 - i)[None, :] * log_g[:, None])
    cdec = jnp.exp(c * log_g)
    cross_b = jnp.broadcast_to(cross[:, :, None], (N_HEADS_R, c, DV_R))
    kdec_b = jnp.broadcast_to(kdec[:, :, None], (N_HEADS_R, c, DK_R))
    has_s0 = s0 is not None
    const3 = lambda b, n: (0, 0, 0)
    state_spec = pl.BlockSpec((bt, N_HEADS_R, DK_R, DV_R), lambda b, n: (b, 0, 0, 0))
    in_specs = [
        pl.BlockSpec(memory_space=pltpu.SMEM),
        pl.BlockSpec((bt, c, 512), lambda b, n: (b, n, _blk("q_r", 512))),
        pl.BlockSpec((bt, c, 512), lambda b, n: (b, n, _blk("k_r", 512))),
        pl.BlockSpec((bt, c, 1024), lambda b, n: (b, n, _blk("v_r", 1024))),
        pl.BlockSpec((bt, c, 1024), lambda b, n: (b, n, _blk("g_r", 1024))),
        pl.BlockSpec((N_HEADS_R, c, c), const3),
        pl.BlockSpec((N_HEADS_R, c, DV_R), const3),
        pl.BlockSpec((N_HEADS_R, c, DK_R), const3),
        pl.BlockSpec((N_HEADS_R, DV_R), lambda b, n: (0, 0)),
    ]
    args = [cdec, proj3, proj3, proj3, proj3, intra, cross_b, kdec_b, ln_ret]
    if has_s0:
        in_specs.append(state_spec)
        args.append(s0)
    return pl.pallas_call(
        functools.partial(_ret_kernel, has_s0=has_s0),
        out_shape=(jax.ShapeDtypeStruct((bn, t, N_HEADS_R * DV_R), BF16),
                   jax.ShapeDtypeStruct((bn, N_HEADS_R, DK_R, DV_R), F32)),
        grid=(bn // bt, t // c),
        in_specs=in_specs,
        out_specs=(pl.BlockSpec((bt, c, 1024), lambda b, n: (b, n, 0)), state_spec),
        scratch_shapes=[pltpu.VMEM((bt, N_HEADS_R, DK_R, DV_R), F32)],
        compiler_params=_params(("arbitrary", "arbitrary")),
        name="ret_sample" if has_s0 else "ret_prompt",
    )(*args)


def _merge_kernel(oa_ref, or_ref, ga_ref, gr_ref, wa_ref, wr_ref, m_ref):
    br_a = _dot(oa_ref[...], wa_ref[...])
    br_r = _dot(or_ref[...], wr_ref[...])
    m_ref[...] = (_sigmoid(ga_ref[...]) * br_a + _sigmoid(gr_ref[...]) * br_r).astype(m_ref.dtype)


def _merge(o_a, o_r, proj, w_a, w_r, tm, tn=512):
    n = o_a.shape[0]
    ga, gr = _blk("gate_a", tn), _blk("gate_r", tn)
    return pl.pallas_call(
        _merge_kernel,
        out_shape=jax.ShapeDtypeStruct((n, D_MODEL), BF16),
        grid=(n // tm, D_MODEL // tn),
        in_specs=[
            pl.BlockSpec((tm, 1024), lambda i, j: (i, 0)),
            pl.BlockSpec((tm, 1024), lambda i, j: (i, 0)),
            pl.BlockSpec((tm, tn), lambda i, j: (i, ga + j)),
            pl.BlockSpec((tm, tn), lambda i, j: (i, gr + j)),
            pl.BlockSpec((1024, tn), lambda i, j: (0, j)),
            pl.BlockSpec((1024, tn), lambda i, j: (0, j)),
        ],
        out_specs=pl.BlockSpec((tm, tn), lambda i, j: (i, j)),
        compiler_params=_params(("arbitrary", "arbitrary")),
        name="merge",
    )(o_a, o_r, proj, proj, w_a, w_r)


def _out_proj_kernel(x_ref, m_ref, w_ref, o_ref):
    o_ref[...] = x_ref[...] + _dot(m_ref[...], w_ref[...])


def _out_proj(x2d, m, w_o, tm, tn=512):
    n = x2d.shape[0]
    return pl.pallas_call(
        _out_proj_kernel,
        out_shape=jax.ShapeDtypeStruct((n, D_MODEL), F32),
        grid=(n // tm, D_MODEL // tn),
        in_specs=[
            pl.BlockSpec((tm, tn), lambda i, j: (i, j)),
            pl.BlockSpec((tm, D_MODEL), lambda i, j: (i, 0)),
            pl.BlockSpec((D_MODEL, tn), lambda i, j: (0, j)),
        ],
        out_specs=pl.BlockSpec((tm, tn), lambda i, j: (i, j)),
        compiler_params=_params(("arbitrary", "arbitrary")),
        name="out_proj",
    )(x2d, m, w_o)


def _scores_kernel(x_ref, g_ref, w_ref, sk_ref, h_out_ref, s_ref, h_ref):
    j = pl.program_id(1)

    @pl.when(j == 0)
    def _():
        h = _rms(x_ref[...], g_ref[...])
        h_ref[...] = h.astype(BF16)
        h_out_ref[...] = h

    qry = _dot(h_ref[...], w_ref[...]).astype(BF16)
    for g in range(qry.shape[1] // D_KEY_HALF):
        s_ref[g] = _dot_nt(sk_ref[g % 2], qry[:, g * D_KEY_HALF:(g + 1) * D_KEY_HALF])


def _scores(x2d, ln, w_q, sub_keys, tm, tn=512):
    n = x2d.shape[0]
    ng = tn // D_KEY_HALF
    return pl.pallas_call(
        _scores_kernel,
        out_shape=(jax.ShapeDtypeStruct((n, D_MODEL), F32),
                   jax.ShapeDtypeStruct((2 * PEER_HEADS, N_KEYS, n), F32)),
        grid=(n // tm, D_MODEL // tn),
        in_specs=[
            pl.BlockSpec((tm, D_MODEL), lambda i, j: (i, 0)),
            pl.BlockSpec((1, D_MODEL), lambda i, j: (0, 0)),
            pl.BlockSpec((D_MODEL, tn), lambda i, j: (0, j)),
            pl.BlockSpec((2, N_KEYS, D_KEY_HALF), lambda i, j: (0, 0, 0)),
        ],
        out_specs=(pl.BlockSpec((tm, D_MODEL), lambda i, j: (i, 0)),
                   pl.BlockSpec((ng, N_KEYS, tm), lambda i, j: (j, 0, i))),
        scratch_shapes=[pltpu.VMEM((tm, D_MODEL), BF16)],
        compiler_params=_params(("arbitrary", "arbitrary")),
        name="peer_scores",
    )(x2d, ln, w_q, sub_keys)


def _take_top(vals, iota, count, fill, payload=None):
    n_rows = vals.shape[0]
    top_v, top_i = [], []
    for _ in range(count):
        m = jnp.max(vals, axis=0, keepdims=True)
        pos = jnp.min(jnp.where(vals == m, iota, n_rows), axis=0, keepdims=True)
        sel = iota == pos
        top_v.append(m)
        top_i.append(pos if payload is None else jnp.max(jnp.where(sel, payload, -1), axis=0, keepdims=True))
        vals = jnp.where(sel, fill, vals)
    return jnp.concatenate(top_v, axis=0), jnp.concatenate(top_i, axis=0)


_COMB_PIECES = tuple((i, PEER_TOPK // (i + 1)) for i in range(SUBLANES))
_COMB_ROWS = sum(max(nj, SUBLANES) for _, nj in _COMB_PIECES) + SUBLANES


def _topk_kernel(s_ref, e_ref, g_ref):
    tt = s_ref.shape[2]
    key_iota = lax.broadcasted_iota(jnp.int32, (N_KEYS, tt), 0)
    comb_iota = lax.broadcasted_iota(jnp.int32, (_COMB_ROWS, tt), 0)
    ninf = float("-inf")

    def head(h, carry):
        s0, i0 = _take_top(s_ref[2 * h], key_iota, PEER_TOPK, ninf)
        s1, i1 = _take_top(s_ref[2 * h + 1], key_iota, PEER_TOPK, ninf)
        comb, cidx = [], []
        for i, nj in _COMB_PIECES:
            rows = max(nj, SUBLANES)
            sv = s0[i:i + 1, :] + s1[:rows]
            iv = i0[i:i + 1, :] * N_KEYS + i1[:rows]
            if nj < rows:
                keep = lax.broadcasted_iota(jnp.int32, (rows, tt), 0) < nj
                sv, iv = jnp.where(keep, sv, ninf), jnp.where(keep, iv, -1)
            comb.append(sv)
            cidx.append(iv)
        comb.append(s0[SUBLANES:] + s1[0:1, :])
        cidx.append(i0[SUBLANES:] * N_KEYS + i1[0:1, :])
        comb, cidx = jnp.concatenate(comb, axis=0), jnp.concatenate(cidx, axis=0)
        best, eidx = _take_top(comb, comb_iota, PEER_TOPK, ninf, payload=cidx)
        ex = jnp.exp(best - jnp.max(best, axis=0, keepdims=True))
        g_ref[h] = ex / jnp.sum(ex, axis=0, keepdims=True)
        e_ref[h] = eidx
        return carry

    lax.fori_loop(0, PEER_HEADS, head, 0)


def _topk(scores_t, tt=256):
    n = scores_t.shape[2]
    out_spec = pl.BlockSpec((PEER_HEADS, PEER_TOPK, tt), lambda i: (0, 0, i))
    return pl.pallas_call(
        _topk_kernel,
        out_shape=(jax.ShapeDtypeStruct((PEER_HEADS, PEER_TOPK, n), jnp.int32),
                   jax.ShapeDtypeStruct((PEER_HEADS, PEER_TOPK, n), F32)),
        grid=(n // tt,),
        in_specs=[pl.BlockSpec((2 * PEER_HEADS, N_KEYS, tt), lambda i: (0, 0, i))],
        out_specs=(out_spec, out_spec),
        compiler_params=_params(("arbitrary",)),
        name="peer_topk",
    )(scores_t)


def _gelu(a):
    return 0.5 * a * (1.0 + lax.erf(a * (2.0 ** -0.5)))


def _peer_kernel(idx_hbm, h_ref, g2_ref, x_ref, uv_hbm, o_ref, idx_smem, buf, sem_i, sem):
    tt = h_ref.shape[0]
    half = D_MODEL // 2
    step = pl.program_id(0)
    more = step + 1 < pl.num_programs(0)
    cur = step % 2

    def idx_copy(s, islot):
        return pltpu.make_async_copy(idx_hbm.at[pl.ds(s * tt, tt), :], idx_smem.at[islot], sem_i.at[islot])

    n_groups = tt // SUBLANES
    n_parts = 2 * PEER_CHUNKS
    cw = half // PEER_CHUNKS

    def gather_start(islot, t, bank, j, part=None):
        ks = range(N_SEL) if part is None else range(part * N_SEL // n_parts, (part + 1) * N_SEL // n_parts)
        for k in ks:
            e = idx_smem[islot, t, k]
            pltpu.make_async_copy(uv_hbm.at[e], buf.at[bank, j, pl.ds(k, 1), :],
                                  sem.at[bank]).start(priority=k % 2)

    def gather_wait(bank):
        pltpu.make_async_copy(buf.at[bank], buf.at[bank], sem.at[bank]).wait()

    @pl.when(step == 0)
    def _():
        first = idx_copy(0, 0)
        first.start()
        first.wait()
        for j in range(SUBLANES):
            gather_start(0, j, 0, j)

    @pl.when(more)
    def _():
        idx_copy(step + 1, 1 - cur).start()

    row_iota = lax.broadcasted_iota(jnp.int32, (SUBLANES, 1), 0)
    even = lax.broadcasted_iota(jnp.int32, (SUBLANES, 2 * N_SEL), 1) % 2 == 0

    def group(gi, carry):
        r0 = pl.multiple_of(gi * SUBLANES, SUBLANES)
        last = gi == n_groups - 1
        bank = gi % 2

        @pl.when(jnp.logical_and(last, more))
        def _():
            idx_copy(step + 1, 1 - cur).wait()

        n_islot = jnp.where(jnp.logical_and(last, more), 1 - cur, cur)
        n_t0 = jnp.where(last, 0, r0 + SUBLANES)
        gather_wait(bank)
        h8 = h_ref[pl.ds(r0, SUBLANES), :]
        g2 = g2_ref[pl.ds(r0, SUBLANES), :]
        acc = [jnp.zeros((2 * SUBLANES, cw), F32) for _ in range(PEER_CHUNKS)]
        issued = [0]

        def request():
            q = issued[0]
            issued[0] += 1
            gather_start(n_islot, n_t0 + q // n_parts, 1 - bank, q // n_parts, part=q % n_parts)

        def u_dot(r, c, p):
            sel = row_iota == r
            hm = jnp.concatenate([jnp.where(sel, h8[:, c * cw:(c + 1) * cw], 0.0),
                                  jnp.where(sel, h8[:, half + c * cw:half + (c + 1) * cw], 0.0)],
                                 axis=0).astype(BF16)
            ub = pltpu.bitcast(buf[bank, r, :, c * cw:(c + 1) * cw], BF16)
            return p + _dot_nt(hm, ub)

        def mix_weights(p):
            part = jnp.where(even, p[:SUBLANES], p[SUBLANES:])
            a2 = part + jnp.where(even, pltpu.roll(part, 2 * N_SEL - 1, 1), pltpu.roll(part, 1, 1))
            w2 = _gelu(a2) * g2
            return jnp.concatenate([jnp.where(even, w2, 0.0), jnp.where(even, 0.0, w2)], axis=0).astype(BF16)

        def v_dot(r, c, wm):
            vb = pltpu.bitcast(buf[bank, r, :, half + c * cw:half + (c + 1) * cw], BF16)
            acc[c] = acc[c] + _dot(wm, vb)

        p_next = jnp.zeros((2 * SUBLANES, 2 * N_SEL), F32)
        for c in range(PEER_CHUNKS):
            request()
            p_next = u_dot(0, c, p_next)
        for r in range(SUBLANES):
            wm = mix_weights(p_next)
            p_next = jnp.zeros((2 * SUBLANES, 2 * N_SEL), F32)
            for c in range(PEER_CHUNKS):
                if r + 1 < SUBLANES:
                    request()
                    p_next = u_dot(r + 1, c, p_next)
                request()
                v_dot(r, c, wm)
        assert issued[0] == SUBLANES * n_parts
        peer = jnp.concatenate([a[:SUBLANES] for a in acc] + [a[SUBLANES:] for a in acc], axis=1)
        o_ref[pl.ds(r0, SUBLANES), :] = x_ref[pl.ds(r0, SUBLANES), :] + peer
        return carry

    lax.fori_loop(0, n_groups, group, 0)

    @pl.when(jnp.logical_not(more))
    def _():
        gather_wait(n_groups % 2)


def _pack_kernel(u_ref, v_ref, o_ref):
    half = D_MODEL // 2

    def words(t):
        bits = lax.bitcast_convert_type(t.astype(BF16).astype(F32), jnp.uint32)
        return (bits[:, :half] >> 16) | bits[:, half:]

    o_ref[:, :half] = words(u_ref[...])
    o_ref[:, half:] = words(v_ref[...])


def _pack_tables(tab_u, tab_v, tm=512):
    e = tab_u.shape[0]
    row = lambda i: (i, 0)
    return pl.pallas_call(
        _pack_kernel,
        out_shape=jax.ShapeDtypeStruct((e, D_MODEL), jnp.uint32),
        grid=(e // tm,),
        in_specs=[pl.BlockSpec((tm, D_MODEL), row), pl.BlockSpec((tm, D_MODEL), row)],
        out_specs=pl.BlockSpec((tm, D_MODEL), row),
        compiler_params=_params(("arbitrary",)),
        name="peer_pack",
    )(tab_u, tab_v)


def _peer(eidx, h2, gate2, x2d, uv, tt):
    n = x2d.shape[0]
    assert tt % (2 * SUBLANES) == 0
    row = lambda i: (i, 0)
    return pl.pallas_call(
        _peer_kernel,
        out_shape=jax.ShapeDtypeStruct((n, D_MODEL), F32),
        grid=(n // tt,),
        in_specs=[
            pl.BlockSpec(memory_space=pl.ANY),
            pl.BlockSpec((tt, D_MODEL), row),
            pl.BlockSpec((tt, 2 * N_SEL), row),
            pl.BlockSpec((tt, D_MODEL), row),
            pl.BlockSpec(memory_space=pl.ANY),
        ],
        out_specs=pl.BlockSpec((tt, D_MODEL), row),
        scratch_shapes=[
            pltpu.SMEM((2, tt, N_SEL), jnp.int32),
            pltpu.VMEM((2, SUBLANES, N_SEL, D_MODEL), jnp.uint32),
            pltpu.SemaphoreType.DMA((2,)),
            pltpu.SemaphoreType.DMA((2,)),
        ],
        compiler_params=_params(("arbitrary",)),
        name="peer_mix",
    )(eidx, h2, gate2, x2d, uv)


def _ple_kernel(x_ref, p_ref, lnp_ref, wg_ref, wp_ref, lnf_ref, y_ref):
    x = x_ref[...]
    gate = _sigmoid(_dot(_rms(x, lnp_ref[...]).astype(BF16), wg_ref[...]))
    x = x + gate * _dot(p_ref[...].astype(BF16), wp_ref[...])
    y_ref[...] = _rms(x, lnf_ref[...])


def _ple(x2d, p2d, ln_ple, w_gate, w_proj, ln_final, tm):
    n = x2d.shape[0]
    pd = p2d.shape[1]
    row = lambda i: (i, 0)
    const = lambda i: (0, 0)
    return pl.pallas_call(
        _ple_kernel,
        out_shape=jax.ShapeDtypeStruct((n, D_MODEL), F32),
        grid=(n // tm,),
        in_specs=[
            pl.BlockSpec((tm, D_MODEL), row),
            pl.BlockSpec((tm, pd), row),
            pl.BlockSpec((1, D_MODEL), const),
            pl.BlockSpec((D_MODEL, D_MODEL), const),
            pl.BlockSpec((pd, D_MODEL), const),
            pl.BlockSpec((1, D_MODEL), const),
        ],
        out_specs=pl.BlockSpec((tm, D_MODEL), row),
        compiler_params=_params(("arbitrary",)),
        name="ple_final",
    )(x2d, p2d, ln_ple, w_gate, w_proj, ln_final)


def _rope_tables(pos):
    inv = ROPE_THETA ** (-jnp.arange(0, HEAD_DIM_A, 2, dtype=F32) / HEAD_DIM_A)
    ang = pos.astype(F32)[:, None] * inv[None, :]
    cos, sin = jnp.cos(ang), jnp.sin(ang)
    reps = PROJ_TILE // HEAD_DIM_A
    cos = jnp.tile(jnp.concatenate([cos, cos], axis=1), (1, reps))
    sin = jnp.tile(jnp.concatenate([-sin, sin], axis=1), (1, reps))
    k_end = (COL["k_a"][0] + COL["k_a"][1]) % PROJ_TILE
    if k_end:
        keep = jnp.arange(PROJ_TILE) < k_end
        cos = jnp.concatenate([cos, jnp.where(keep, cos, 1.0)], axis=1)
        sin = jnp.concatenate([sin, jnp.where(keep, sin, 0.0)], axis=1)
    return cos, sin


def _pick(n, prefs):
    for c in prefs:
        if n % c == 0:
            return c
    raise ValueError(f"no tile of {prefs} divides {n}")


def _group(x, pe, pos0, cache_k, cache_v, s0, wts):
    bn, t, _ = x.shape
    n = bn * t
    prompt = cache_k is None
    x2d = x.reshape(n, D_MODEL)
    tm = _pick(n, (1024, 512, 256, 128))

    pos = pos0 + jnp.arange(t, dtype=jnp.int32)
    cos, sin = _rope_tables(pos)
    if t < tm:
        cos, sin = jnp.tile(cos, (tm // t, 1)), jnp.tile(sin, (tm // t, 1))
    proj = _in_proj(x2d, wts["ln1"], wts["w_in"], cos, sin, wts["colscale"], tm)
    proj3 = proj.reshape(bn, t, IN_COLS)

    if prompt:
        c_att, bt_att = ATTN_BLOCK, 1
        ctx_k = ctx_v = proj3
    else:
        c_att, bt_att = t, _pick(bn, (8, 4, 2, 1))
        ctx_k = cache_k.reshape(bn, WINDOW, N_KV_A * HEAD_DIM_A)
        ctx_v = cache_v.reshape(bn, WINDOW, N_KV_A * HEAD_DIM_A)
    o_a = _attention(wts["sinks"], proj3, ctx_k, ctx_v, bt_att, c_att, prompt)

    c_ret = math.gcd(t, RET_CHUNK)
    bt_ret = 1 if prompt else _pick(bn, (8, 4, 2, 1))
    o_r, s_new = _retention(proj3, s0, wts["ln_ret"], bt_ret, c_ret)

    m = _merge(o_a.reshape(n, -1), o_r.reshape(n, -1), proj, wts["w_br_a"], wts["w_br_r"], tm)
    x2 = _out_proj(x2d, m, wts["w_o"], tm)

    h2, scores_t = _scores(x2, wts["ln2"], wts["w_q"], wts["sub_keys"], _pick(n, (512, 256, 128)))
    eidx_t, gate_t = _topk(scores_t, _pick(n, (256, 128)))
    eidx = eidx_t.reshape(N_SEL, n).T
    gate2 = jnp.repeat(gate_t.reshape(N_SEL, n).T, 2, axis=1)
    x3 = _peer(eidx, h2, gate2, x2, wts["peer_uv"], _pick(n, (64, 32, 16, 8)))

    y = _ple(x3, pe.reshape(n, -1), wts["ln_ple"], wts["w_ple_gate"], wts["w_ple_proj"], wts["ln_final"],
             _pick(n, (256, 128)))

    k_off, v_off = COL["k_a"][0], COL["v_a"][0]
    kv_w = N_KV_A * HEAD_DIM_A
    k_new, v_new = proj3[:, :, k_off:k_off + kv_w], proj3[:, :, v_off:v_off + kv_w]
    if prompt:
        k_win, v_win = k_new[:, -WINDOW:], v_new[:, -WINDOW:]
    else:
        k_win = jnp.concatenate([ctx_k, k_new], axis=1)[:, -WINDOW:]
        v_win = jnp.concatenate([ctx_v, v_new], axis=1)[:, -WINDOW:]
    shp = (bn, WINDOW, N_KV_A, HEAD_DIM_A)
    return y.reshape(bn, t, D_MODEL), k_win.reshape(shp), v_win.reshape(shp), s_new


def kernel(x_prompt, x_sample, cache_k_win, cache_v_win, state_ret, p_prompt, p_sample, ln1, w_in, attn_sinks, ln_ret, w_branch_attn, w_branch_ret, w_out, ln2, w_peer_query, peer_sub_keys, peer_u, peer_v, ln_ple, w_ple_gate, w_ple_proj, ln_final):
    depth = ln1.shape[0]
    assert depth == 1, "single-layer step"
    i = 0
    colscale = jnp.ones((IN_COLS,), F32).at[COL["k_r"][0]:COL["k_r"][0] + COL["k_r"][1]].set(DK_R ** -0.5)
    wts = dict(
        ln1=ln1[i][None, :],
        w_in=jnp.concatenate([w_in[i][:, _REF_COL[f][0]:_REF_COL[f][0] + _REF_COL[f][1]] for f in _NEW_ORDER],
                             axis=1).astype(BF16),
        colscale=colscale[None, :],
        sinks=attn_sinks[i],
        ln_ret=ln_ret[i].reshape(N_HEADS_R, DV_R),
        w_br_a=w_branch_attn[i].astype(BF16),
        w_br_r=w_branch_ret[i].astype(BF16),
        w_o=w_out[i].astype(BF16),
        ln2=ln2[i][None, :],
        w_q=w_peer_query[i].astype(BF16),
        sub_keys=peer_sub_keys[i].astype(BF16),
        peer_uv=_pack_tables(peer_u[i], peer_v[i])[:, None, :],
        ln_ple=ln_ple[i][None, :],
        w_ple_gate=w_ple_gate[i].astype(BF16),
        w_ple_proj=w_ple_proj[i].astype(BF16),
        ln_final=ln_final[None, :],
    )
    yp, kp, vp, sp = _group(x_prompt, p_prompt[i], 0, None, None, None, wts)
    ys, ks, vs, ss = _group(x_sample, p_sample[i], PAST_LEN, cache_k_win[i], cache_v_win[i], state_ret[i], wts)
    return (yp, ys, kp[None], vp[None], sp[None], ks[None], vs[None], ss[None])
```

```python
import functools
import math

import jax
import jax.numpy as jnp
from jax import lax
from jax.experimental import pallas as pl
from jax.experimental.pallas import tpu as pltpu

F32 = jnp.float32
BF16 = jnp.bfloat16

D_MODEL = 2048
PAST_LEN = 16384
HEAD_DIM_A = 64
N_HEADS_A = 16
N_KV_A = 4
GROUP_A = 4
WINDOW = 128
ATTN_BLOCK = 128
N_HEADS_R = 8
DK_R = 64
DV_R = 128
RET_CHUNK = 128
PEER_HEADS = 8
N_KEYS = 128
PEER_TOPK = 16
D_KEY_HALF = 128
N_SEL = PEER_HEADS * PEER_TOPK
PEER_CHUNKS = 4
ROPE_THETA = 10000.0
EPS = 1e-6
NEG_INF = -1e30

LANES = 128
SUBLANES = 8
VMEM_LIMIT_BYTES = 56 * 1024 * 1024

_REF_FIELDS = (("q_a", 1024), ("k_a", 256), ("v_a", 256), ("q_r", 512), ("k_r", 512),
               ("v_r", 1024), ("g_r", 1024), ("gate_a", 2048), ("gate_r", 2048))
_NEW_ORDER = ("gate_a", "gate_r", "q_a", "v_r", "g_r", "q_r", "k_r", "k_a", "v_a")
_ROPE_FIELDS = ("q_a", "q_r", "k_r", "k_a")
PROJ_TILE = 512


def _layout():
    ref_off, o = {}, 0
    for name, w in _REF_FIELDS:
        ref_off[name] = (o, w)
        o += w
    new_off, o = {}, 0
    for name in _NEW_ORDER:
        new_off[name] = (o, ref_off[name][1])
        o += ref_off[name][1]
    return new_off, ref_off, o


COL, _REF_COL, IN_COLS = _layout()


def _blk(name, width):
    off, w = COL[name]
    assert off % width == 0 and w % width == 0
    return off // width


def _params(sem, vmem=VMEM_LIMIT_BYTES):
    return pltpu.CompilerParams(dimension_semantics=sem, vmem_limit_bytes=vmem)


def _rms(x, g):
    return x * lax.rsqrt(jnp.mean(x * x, axis=-1, keepdims=True) + EPS) * g


def _sigmoid(x):
    return 1.0 / (1.0 + jnp.exp(-x))


def _dot(a, b):
    return jnp.dot(a, b, preferred_element_type=F32)


def _dot_nt(a, b):
    return lax.dot_general(a, b, (((1,), (1,)), ((), ())), preferred_element_type=F32)


def _dot_tn(a, b):
    return lax.dot_general(a, b, (((0,), (0,)), ((), ())), preferred_element_type=F32)


def _in_proj_kernel(x_ref, g_ref, w_ref, cos_ref, sin_ref, cs_ref, o_ref, h_ref, *, rope_ranges):
    j = pl.program_id(1)

    @pl.when(j == 0)
    def _():
        h_ref[...] = _rms(x_ref[...], g_ref[...]).astype(BF16)

    acc = _dot(h_ref[...], w_ref[...])
    is_rope = functools.reduce(jnp.logical_or, [(j >= a) & (j < b) for a, b in rope_ranges])

    @pl.when(is_rope)
    def _():
        tn = acc.shape[1]
        lane = lax.broadcasted_iota(jnp.int32, acc.shape, 1)
        first_half = (lane % HEAD_DIM_A) < (HEAD_DIM_A // 2)
        partner = jnp.where(first_half, pltpu.roll(acc, tn - HEAD_DIM_A // 2, 1), pltpu.roll(acc, HEAD_DIM_A // 2, 1))
        o_ref[...] = (acc * cos_ref[...] + partner * sin_ref[...]) * cs_ref[...]

    @pl.when(jnp.logical_not(is_rope))
    def _():
        o_ref[...] = acc


def _in_proj(x2d, ln, w, cos, sin, colscale, tm):
    n = x2d.shape[0]
    rt = cos.shape[0] // tm
    rope_ranges = tuple((COL[f][0] // PROJ_TILE, -(-(COL[f][0] + COL[f][1]) // PROJ_TILE)) for f in _ROPE_FIELDS)
    ends = [COL[f][0] + COL[f][1] for f in _ROPE_FIELDS if (COL[f][0] + COL[f][1]) % PROJ_TILE]
    assert len(ends) <= 1 and cos.shape[1] == (1 + len(ends)) * PROJ_TILE
    mixed = ends[0] // PROJ_TILE if ends else -1
    table_spec = pl.BlockSpec((tm, PROJ_TILE), lambda i, j: (i % rt, jnp.where(j == mixed, 1, 0)))
    return pl.pallas_call(
        functools.partial(_in_proj_kernel, rope_ranges=rope_ranges),
        out_shape=jax.ShapeDtypeStruct((n, IN_COLS), F32),
        grid=(n // tm, IN_COLS // PROJ_TILE),
        in_specs=[
            pl.BlockSpec((tm, D_MODEL), lambda i, j: (i, 0)),
            pl.BlockSpec((1, D_MODEL), lambda i, j: (0, 0)),
            pl.BlockSpec((D_MODEL, PROJ_TILE), lambda i, j: (0, j)),
            table_spec,
            table_spec,
            pl.BlockSpec((1, PROJ_TILE), lambda i, j: (0, j)),
        ],
        out_specs=pl.BlockSpec((tm, PROJ_TILE), lambda i, j: (i, j)),
        scratch_shapes=[pltpu.VMEM((tm, D_MODEL), BF16)],
        compiler_params=_params(("arbitrary", "arbitrary")),
        name="in_proj",
    )(x2d, ln, w, cos, sin, colscale)


def _attn_kernel(sink_ref, q_ref, kc_ref, vc_ref, kx_ref, vx_ref, o_ref, *, first_ctx_invalid):
    bt, c, _ = q_ref.shape
    rows = GROUP_A * c
    qi = lax.broadcasted_iota(jnp.int32, (rows, WINDOW), 0) % c
    kj = lax.broadcasted_iota(jnp.int32, (rows, WINDOW), 1)
    ctx_ok = kj >= qi
    if first_ctx_invalid:
        ctx_ok = jnp.logical_and(ctx_ok, pl.program_id(1) > 0)
    qi_c = lax.broadcasted_iota(jnp.int32, (rows, c), 0) % c
    kj_c = lax.broadcasted_iota(jnp.int32, (rows, c), 1)
    cur_ok = kj_c <= qi_c
    grp = lax.broadcasted_iota(jnp.int32, (rows, 1), 0) // c
    scale = HEAD_DIM_A ** -0.5

    def one(b):
        q = q_ref[b]
        kc, vc, kx, vx = kc_ref[b], vc_ref[b], kx_ref[b], vx_ref[b]
        outs = [None] * N_HEADS_A
        for kh in range(N_KV_A):
            sl = slice(kh * HEAD_DIM_A, (kh + 1) * HEAD_DIM_A)
            q4 = jnp.concatenate(
                [q[:, (kh * GROUP_A + g) * HEAD_DIM_A:(kh * GROUP_A + g + 1) * HEAD_DIM_A] for g in range(GROUP_A)],
                axis=0).astype(BF16)
            s_x = jnp.where(ctx_ok, _dot_nt(q4, kx[:, sl].astype(BF16)) * scale, NEG_INF)
            s_c = jnp.where(cur_ok, _dot_nt(q4, kc[:, sl].astype(BF16)) * scale, NEG_INF)
            sink = jnp.zeros((rows, 1), F32)
            for g in range(GROUP_A):
                sink = jnp.where(grp == g, sink_ref[kh * GROUP_A + g], sink)
            m = jnp.maximum(jnp.maximum(jnp.max(s_x, axis=-1, keepdims=True), jnp.max(s_c, axis=-1, keepdims=True)), sink)
            e_x = jnp.exp(s_x - m)
            e_c = jnp.exp(s_c - m)
            den = jnp.sum(e_x, axis=-1, keepdims=True) + jnp.sum(e_c, axis=-1, keepdims=True) + jnp.exp(sink - m)
            o4 = _dot((e_x / den).astype(BF16), vx[:, sl].astype(BF16)) + _dot((e_c / den).astype(BF16), vc[:, sl].astype(BF16))
            for g in range(GROUP_A):
                outs[kh * GROUP_A + g] = o4[g * c:(g + 1) * c]
        o_ref[b] = jnp.concatenate(outs, axis=1).astype(o_ref.dtype)

    for b in range(bt):
        one(b)


def _attention(sinks, proj3, ctx_k, ctx_v, bt, c, prompt):
    bn, t, _ = proj3.shape
    qb, kb, vb = _blk("q_a", 1024), _blk("k_a", 256), _blk("v_a", 256)
    if prompt:
        ctx_specs = [pl.BlockSpec((bt, WINDOW, 256), lambda b, n: (b, jnp.maximum(n - 1, 0), kb)),
                     pl.BlockSpec((bt, WINDOW, 256), lambda b, n: (b, jnp.maximum(n - 1, 0), vb))]
    else:
        ctx_specs = [pl.BlockSpec((bt, WINDOW, 256), lambda b, n: (b, 0, 0)),
                     pl.BlockSpec((bt, WINDOW, 256), lambda b, n: (b, 0, 0))]
    return pl.pallas_call(
        functools.partial(_attn_kernel, first_ctx_invalid=prompt),
        out_shape=jax.ShapeDtypeStruct((bn, t, N_HEADS_A * HEAD_DIM_A), BF16),
        grid=(bn // bt, t // c),
        in_specs=[
            pl.BlockSpec(memory_space=pltpu.SMEM),
            pl.BlockSpec((bt, c, 1024), lambda b, n: (b, n, qb)),
            pl.BlockSpec((bt, c, 256), lambda b, n: (b, n, kb)),
            pl.BlockSpec((bt, c, 256), lambda b, n: (b, n, vb)),
        ] + ctx_specs,
        out_specs=pl.BlockSpec((bt, c, 1024), lambda b, n: (b, n, 0)),
        compiler_params=_params(("arbitrary", "arbitrary")),
        name="attn_prompt" if prompt else "attn_sample",
    )(sinks, proj3, proj3, proj3, ctx_k, ctx_v)


def _ret_kernel(cdec_ref, q_ref, k_ref, v_ref, g_ref, intra_ref, cross_ref, kdec_ref, ln_ref, *rest, has_s0):
    if has_s0:
        s0_ref, o_ref, sout_ref, s_ref = rest
    else:
        o_ref, sout_ref, s_ref = rest
    bt = q_ref.shape[0]
    ci = pl.program_id(1)

    @pl.when(ci == 0)
    def _():
        s_ref[...] = s0_ref[...] if has_s0 else jnp.zeros(s_ref.shape, F32)

    def one(b):
        q, k, v, gt = q_ref[b], k_ref[b], v_ref[b], g_ref[b]
        outs = []
        for h in range(N_HEADS_R):
            qh = q[:, h * DK_R:(h + 1) * DK_R].astype(BF16)
            kh = k[:, h * DK_R:(h + 1) * DK_R]
            vh = v[:, h * DV_R:(h + 1) * DV_R].astype(BF16)
            s = s_ref[b, h]
            att = _dot_nt(qh, kh.astype(BF16)) * intra_ref[h]
            o = _dot(att.astype(BF16), vh) + _dot(qh, s.astype(BF16)) * cross_ref[h]
            s_ref[b, h] = s * cdec_ref[h] + _dot_tn((kh * kdec_ref[h]).astype(BF16), vh)
            mu = jnp.mean(o, axis=-1, keepdims=True)
            d = o - mu
            var = jnp.mean(d * d, axis=-1, keepdims=True)
            gh = gt[:, h * DV_R:(h + 1) * DV_R]
            outs.append(d * lax.rsqrt(var + EPS) * ln_ref[h:h + 1, :] * (gh * _sigmoid(gh)))
        o_ref[b] = jnp.concatenate(outs, axis=1).astype(o_ref.dtype)

    for b in range(bt):
        one(b)

    @pl.when(ci == pl.num_programs(1) - 1)
    def _():
        sout_ref[...] = s_ref[...]


def _retention(proj3, s0, ln_ret, bt, c):
    bn, t, _ = proj3.shape
    log_g = jnp.log1p(-jnp.exp2(-5.0 - jnp.arange(N_HEADS_R, dtype=F32)))
    i = jnp.arange(c, dtype=F32)
    diff = i[:, None] - i[None, :]
    intra = jnp.where(diff[None] >= 0, jnp.exp(jnp.maximum(diff, 0.0)[None] * log_g[:, None, None]), 0.0)
    cross = jnp.exp((i + 1.0)[None, :] * log_g[:, None])
    kdec = jnp.exp((c - 1.0 - i)[None, :] * log_g[:, None])
    cdec = jnp.exp(c * log_g)
    cross_b = jnp.broadcast_to(cross[:, :, None], (N_HEADS_R, c, DV_R))
    kdec_b = jnp.broadcast_to(kdec[:, :, None], (N_HEADS_R, c, DK_R))
    has_s0 = s0 is not None
    const3 = lambda b, n: (0, 0, 0)
    state_spec = pl.BlockSpec((bt, N_HEADS_R, DK_R, DV_R), lambda b, n: (b, 0, 0, 0))
    in_specs = [
        pl.BlockSpec(memory_space=pltpu.SMEM),
        pl.BlockSpec((bt, c, 512), lambda b, n: (b, n, _blk("q_r", 512))),
        pl.BlockSpec((bt, c, 512), lambda b, n: (b, n, _blk("k_r", 512))),
        pl.BlockSpec((bt, c, 1024), lambda b, n: (b, n, _blk("v_r", 1024))),
        pl.BlockSpec((bt, c, 1024), lambda b, n: (b, n, _blk("g_r", 1024))),
        pl.BlockSpec((N_HEADS_R, c, c), const3),
        pl.BlockSpec((N_HEADS_R, c, DV_R), const3),
        pl.BlockSpec((N_HEADS_R, c, DK_R), const3),
        pl.BlockSpec((N_HEADS_R, DV_R), lambda b, n: (0, 0)),
    ]
    args = [cdec, proj3, proj3, proj3, proj3, intra, cross_b, kdec_b, ln_ret]
    if has_s0:
        in_specs.append(state_spec)
        args.append(s0)
    return pl.pallas_call(
        functools.partial(_ret_kernel, has_s0=has_s0),
        out_shape=(jax.ShapeDtypeStruct((bn, t, N_HEADS_R * DV_R), BF16),
                   jax.ShapeDtypeStruct((bn, N_HEADS_R, DK_R, DV_R), F32)),
        grid=(bn // bt, t // c),
        in_specs=in_specs,
        out_specs=(pl.BlockSpec((bt, c, 1024), lambda b, n: (b, n, 0)), state_spec),
        scratch_shapes=[pltpu.VMEM((bt, N_HEADS_R, DK_R, DV_R), F32)],
        compiler_params=_params(("arbitrary", "arbitrary")),
        name="ret_sample" if has_s0 else "ret_prompt",
    )(*args)


def _merge_kernel(oa_ref, or_ref, ga_ref, gr_ref, wa_ref, wr_ref, m_ref):
    br_a = _dot(oa_ref[...], wa_ref[...])
    br_r = _dot(or_ref[...], wr_ref[...])
    m_ref[...] = (_sigmoid(ga_ref[...]) * br_a + _sigmoid(gr_ref[...]) * br_r).astype(m_ref.dtype)


def _merge(o_a, o_r, proj, w_a, w_r, tm, tn=512):
    n = o_a.shape[0]
    ga, gr = _blk("gate_a", tn), _blk("gate_r", tn)
    return pl.pallas_call(
        _merge_kernel,
        out_shape=jax.ShapeDtypeStruct((n, D_MODEL), BF16),
        grid=(n // tm, D_MODEL // tn),
        in_specs=[
            pl.BlockSpec((tm, 1024), lambda i, j: (i, 0)),
            pl.BlockSpec((tm, 1024), lambda i, j: (i, 0)),
            pl.BlockSpec((tm, tn), lambda i, j: (i, ga + j)),
            pl.BlockSpec((tm, tn), lambda i, j: (i, gr + j)),
            pl.BlockSpec((1024, tn), lambda i, j: (0, j)),
            pl.BlockSpec((1024, tn), lambda i, j: (0, j)),
        ],
        out_specs=pl.BlockSpec((tm, tn), lambda i, j: (i, j)),
        compiler_params=_params(("arbitrary", "arbitrary")),
        name="merge",
    )(o_a, o_r, proj, proj, w_a, w_r)


def _out_proj_kernel(x_ref, m_ref, w_ref, o_ref):
    o_ref[...] = x_ref[...] + _dot(m_ref[...], w_ref[...])


def _out_proj(x2d, m, w_o, tm, tn=512):
    n = x2d.shape[0]
    return pl.pallas_call(
        _out_proj_kernel,
        out_shape=jax.ShapeDtypeStruct((n, D_MODEL), F32),
        grid=(n // tm, D_MODEL // tn),
        in_specs=[
            pl.BlockSpec((tm, tn), lambda i, j: (i, j)),
            pl.BlockSpec((tm, D_MODEL), lambda i, j: (i, 0)),
            pl.BlockSpec((D_MODEL, tn), lambda i, j: (0, j)),
        ],
        out_specs=pl.BlockSpec((tm, tn), lambda i, j: (i, j)),
        compiler_params=_params(("arbitrary", "arbitrary")),
        name="out_proj",
    )(x2d, m, w_o)


def _scores_kernel(x_ref, g_ref, w_ref, sk_ref, h_out_ref, s_ref, h_ref):
    j = pl.program_id(1)

    @pl.when(j == 0)
    def _():
        h = _rms(x_ref[...], g_ref[...])
        h_ref[...] = h.astype(BF16)
        h_out_ref[...] = h

    qry = _dot(h_ref[...], w_ref[...]).astype(BF16)
    for g in range(qry.shape[1] // D_KEY_HALF):
        s_ref[g] = _dot_nt(sk_ref[g % 2], qry[:, g * D_KEY_HALF:(g + 1) * D_KEY_HALF])


def _scores(x2d, ln, w_q, sub_keys, tm, tn=512):
    n = x2d.shape[0]
    ng = tn // D_KEY_HALF
    return pl.pallas_call(
        _scores_kernel,
        out_shape=(jax.ShapeDtypeStruct((n, D_MODEL), F32),
                   jax.ShapeDtypeStruct((2 * PEER_HEADS, N_KEYS, n), F32)),
        grid=(n // tm, D_MODEL // tn),
        in_specs=[
            pl.BlockSpec((tm, D_MODEL), lambda i, j: (i, 0)),
            pl.BlockSpec((1, D_MODEL), lambda i, j: (0, 0)),
            pl.BlockSpec((D_MODEL, tn), lambda i, j: (0, j)),
            pl.BlockSpec((2, N_KEYS, D_KEY_HALF), lambda i, j: (0, 0, 0)),
        ],
        out_specs=(pl.BlockSpec((tm, D_MODEL), lambda i, j: (i, 0)),
                   pl.BlockSpec((ng, N_KEYS, tm), lambda i, j: (j, 0, i))),
        scratch_shapes=[pltpu.VMEM((tm, D_MODEL), BF16)],
        compiler_params=_params(("arbitrary", "arbitrary")),
        name="peer_scores",
    )(x2d, ln, w_q, sub_keys)


def _take_top(vals, iota, count, fill, payload=None):
    n_rows = vals.shape[0]
    tags = (iota,) if payload is None else (iota, payload)

    def first_max(v):
        row8 = lax.broadcasted_iota(jnp.int32, (SUBLANES, v.shape[1]), 0)
        blocks = [(v[i:i + SUBLANES], row8 + i) + tuple(t[i:i + SUBLANES] for t in tags[1:])
                  for i in range(0, n_rows, SUBLANES)]
        while len(blocks) > 1:
            nxt = []
            for a, b in zip(blocks[0::2], blocks[1::2]):
                take_a = a[0] >= b[0]
                nxt.append(tuple(jnp.where(take_a, x, y) for x, y in zip(a, b)))
            blocks = nxt + ([blocks[-1]] if len(blocks) % 2 else [])
        cur = blocks[0]
        m = jnp.max(cur[0], axis=0, keepdims=True)
        pos = jnp.min(jnp.where(cur[0] == m, cur[1], n_rows), axis=0, keepdims=True)
        rest = tuple(jnp.max(jnp.where(cur[1] == pos, t, -1), axis=0, keepdims=True) for t in cur[2:])
        return (m, pos) + rest

    top_v, top_i = [], []
    for _ in range(count):
        best = first_max(vals)
        top_v.append(best[0])
        top_i.append(best[-1])
        vals = jnp.where(iota == best[1], fill, vals)
    return jnp.concatenate(top_v, axis=0), jnp.concatenate(top_i, axis=0)


_COMB_PIECES = tuple((i, PEER_TOPK // (i + 1)) for i in range(SUBLANES))
_COMB_ROWS = sum(max(nj, SUBLANES) for _, nj in _COMB_PIECES) + SUBLANES


def _topk_kernel(s_ref, e_ref, g_ref):
    tt = s_ref.shape[2]
    key_iota = lax.broadcasted_iota(jnp.int32, (N_KEYS, tt), 0)
    comb_iota = lax.broadcasted_iota(jnp.int32, (_COMB_ROWS, tt), 0)
    ninf = float("-inf")

    def head(h, carry):
        s0, i0 = _take_top(s_ref[2 * h], key_iota, PEER_TOPK, ninf)
        s1, i1 = _take_top(s_ref[2 * h + 1], key_iota, PEER_TOPK, ninf)
        comb, cidx = [], []
        for i, nj in _COMB_PIECES:
            rows = max(nj, SUBLANES)
            sv = s0[i:i + 1, :] + s1[:rows]
            iv = i0[i:i + 1, :] * N_KEYS + i1[:rows]
            if nj < rows:
                keep = lax.broadcasted_iota(jnp.int32, (rows, tt), 0) < nj
                sv, iv = jnp.where(keep, sv, ninf), jnp.where(keep, iv, -1)
            comb.append(sv)
            cidx.append(iv)
        comb.append(s0[SUBLANES:] + s1[0:1, :])
        cidx.append(i0[SUBLANES:] * N_KEYS + i1[0:1, :])
        comb, cidx = jnp.concatenate(comb, axis=0), jnp.concatenate(cidx, axis=0)
        best, eidx = _take_top(comb, comb_iota, PEER_TOPK, ninf, payload=cidx)
        ex = jnp.exp(best - jnp.max(best, axis=0, keepdims=True))
        g_ref[h] = ex / jnp.sum(ex, axis=0, keepdims=True)
        e_ref[h] = eidx
        return carry

    lax.fori_loop(0, PEER_HEADS, head, 0)


def _topk(scores_t, tt=256):
    n = scores_t.shape[2]
    out_spec = pl.BlockSpec((PEER_HEADS, PEER_TOPK, tt), lambda i: (0, 0, i))
    return pl.pallas_call(
        _topk_kernel,
        out_shape=(jax.ShapeDtypeStruct((PEER_HEADS, PEER_TOPK, n), jnp.int32),
                   jax.ShapeDtypeStruct((PEER_HEADS, PEER_TOPK, n), F32)),
        grid=(n // tt,),
        in_specs=[pl.BlockSpec((2 * PEER_HEADS, N_KEYS, tt), lambda i: (0, 0, i))],
        out_specs=(out_spec, out_spec),
        compiler_params=_params(("arbitrary",)),
        name="peer_topk",
    )(scores_t)


def _gelu(a):
    return 0.5 * a * (1.0 + lax.erf(a * (2.0 ** -0.5)))


def _ple_rows(x, p, ln_ple, w_gate, w_proj, ln_final):
    gate = _sigmoid(_dot(_rms(x, ln_ple).astype(BF16), w_gate))
    x = x + gate * _dot(p.astype(BF16), w_proj)
    return _rms(x, ln_final)


def _peer_kernel(idx_hbm, h_ref, g2_ref, x_ref, uv_hbm, p_ref, lnp_ref, wg_ref, wp_ref, lnf_ref, o_ref,
                 idx_smem, buf, sem_i, sem):
    tt = h_ref.shape[0]
    half = D_MODEL // 2
    step = pl.program_id(0)
    more = step + 1 < pl.num_programs(0)
    cur = step % 2

    def idx_copy(s, islot):
        return pltpu.make_async_copy(idx_hbm.at[pl.ds(s * tt, tt), :], idx_smem.at[islot], sem_i.at[islot])

    n_groups = tt // SUBLANES
    n_parts = 2 * PEER_CHUNKS
    cw = half // PEER_CHUNKS

    def gather_start(islot, t, bank, j, part=None):
        ks = range(N_SEL) if part is None else range(part * N_SEL // n_parts, (part + 1) * N_SEL // n_parts)
        for k in ks:
            e = idx_smem[islot, t, k]
            pltpu.make_async_copy(uv_hbm.at[e], buf.at[bank, j, pl.ds(k, 1), :],
                                  sem.at[bank]).start(priority=k % 2)

    def gather_wait(bank):
        pltpu.make_async_copy(buf.at[bank], buf.at[bank], sem.at[bank]).wait()

    @pl.when(step == 0)
    def _():
        first = idx_copy(0, 0)
        first.start()
        first.wait()
        for j in range(SUBLANES):
            gather_start(0, j, 0, j)

    @pl.when(more)
    def _():
        idx_copy(step + 1, 1 - cur).start()

    row_iota = lax.broadcasted_iota(jnp.int32, (SUBLANES, 1), 0)
    even = lax.broadcasted_iota(jnp.int32, (SUBLANES, 2 * N_SEL), 1) % 2 == 0

    def group(gi, carry):
        r0 = pl.multiple_of(gi * SUBLANES, SUBLANES)
        last = gi == n_groups - 1
        bank = gi % 2

        @pl.when(jnp.logical_and(last, more))
        def _():
            idx_copy(step + 1, 1 - cur).wait()

        n_islot = jnp.where(jnp.logical_and(last, more), 1 - cur, cur)
        n_t0 = jnp.where(last, 0, r0 + SUBLANES)
        gather_wait(bank)
        h8 = h_ref[pl.ds(r0, SUBLANES), :]
        g2 = g2_ref[pl.ds(r0, SUBLANES), :]
        acc = [jnp.zeros((2 * SUBLANES, cw), F32) for _ in range(PEER_CHUNKS)]
        issued = [0]

        def request():
            q = issued[0]
            issued[0] += 1
            gather_start(n_islot, n_t0 + q // n_parts, 1 - bank, q // n_parts, part=q % n_parts)

        def u_dot(r, c, p):
            sel = row_iota == r
            hm = jnp.concatenate([jnp.where(sel, h8[:, c * cw:(c + 1) * cw], 0.0),
                                  jnp.where(sel, h8[:, half + c * cw:half + (c + 1) * cw], 0.0)],
                                 axis=0).astype(BF16)
            ub = pltpu.bitcast(buf[bank, r, :, c * cw:(c + 1) * cw], BF16)
            return p + _dot_nt(hm, ub)

        def mix_weights(p):
            part = jnp.where(even, p[:SUBLANES], p[SUBLANES:])
            a2 = part + jnp.where(even, pltpu.roll(part, 2 * N_SEL - 1, 1), pltpu.roll(part, 1, 1))
            w2 = _gelu(a2) * g2
            return jnp.concatenate([jnp.where(even, w2, 0.0), jnp.where(even, 0.0, w2)], axis=0).astype(BF16)

        def v_dot(r, c, wm):
            vb = pltpu.bitcast(buf[bank, r, :, half + c * cw:half + (c + 1) * cw], BF16)
            acc[c] = acc[c] + _dot(wm, vb)

        p_next = jnp.zeros((2 * SUBLANES, 2 * N_SEL), F32)
        for c in range(PEER_CHUNKS):
            request()
            p_next = u_dot(0, c, p_next)
        for r in range(SUBLANES):
            wm = mix_weights(p_next)
            p_next = jnp.zeros((2 * SUBLANES, 2 * N_SEL), F32)
            for c in range(PEER_CHUNKS):
                if r + 1 < SUBLANES:
                    request()
                    p_next = u_dot(r + 1, c, p_next)
                request()
                v_dot(r, c, wm)
        assert issued[0] == SUBLANES * n_parts
        peer = jnp.concatenate([a[:SUBLANES] for a in acc] + [a[SUBLANES:] for a in acc], axis=1)
        o_ref[pl.ds(r0, SUBLANES), :] = x_ref[pl.ds(r0, SUBLANES), :] + peer
        return carry

    lax.fori_loop(0, n_groups, group, 0)

    o_ref[...] = _ple_rows(o_ref[...], p_ref[...], lnp_ref[...], wg_ref[...], wp_ref[...], lnf_ref[...])

    @pl.when(jnp.logical_not(more))
    def _():
        gather_wait(n_groups % 2)


def _pack_kernel(u_ref, v_ref, o_ref):
    half = D_MODEL // 2

    def words(t):
        bits = lax.bitcast_convert_type(t.astype(BF16).astype(F32), jnp.uint32)
        return (bits[:, :half] >> 16) | bits[:, half:]

    o_ref[:, :half] = words(u_ref[...])
    o_ref[:, half:] = words(v_ref[...])


def _pack_tables(tab_u, tab_v, tm=512):
    e = tab_u.shape[0]
    row = lambda i: (i, 0)
    return pl.pallas_call(
        _pack_kernel,
        out_shape=jax.ShapeDtypeStruct((e, D_MODEL), jnp.uint32),
        grid=(e // tm,),
        in_specs=[pl.BlockSpec((tm, D_MODEL), row), pl.BlockSpec((tm, D_MODEL), row)],
        out_specs=pl.BlockSpec((tm, D_MODEL), row),
        compiler_params=_params(("arbitrary",)),
        name="peer_pack",
    )(tab_u, tab_v)


def _peer(eidx, h2, gate2, x2d, uv, p2d, ln_ple, w_gate, w_proj, ln_final, tt):
    n = x2d.shape[0]
    pd = p2d.shape[1]
    assert tt % (2 * SUBLANES) == 0
    row = lambda i: (i, 0)
    const = lambda i: (0, 0)
    return pl.pallas_call(
        _peer_kernel,
        out_shape=jax.ShapeDtypeStruct((n, D_MODEL), F32),
        grid=(n // tt,),
        in_specs=[
            pl.BlockSpec(memory_space=pl.ANY),
            pl.BlockSpec((tt, D_MODEL), row),
            pl.BlockSpec((tt, 2 * N_SEL), row),
            pl.BlockSpec((tt, D_MODEL), row),
            pl.BlockSpec(memory_space=pl.ANY),
            pl.BlockSpec((tt, pd), row),
            pl.BlockSpec((1, D_MODEL), const),
            pl.BlockSpec((D_MODEL, D_MODEL), const),
            pl.BlockSpec((pd, D_MODEL), const),
            pl.BlockSpec((1, D_MODEL), const),
        ],
        out_specs=pl.BlockSpec((tt, D_MODEL), row),
        scratch_shapes=[
            pltpu.SMEM((2, tt, N_SEL), jnp.int32),
            pltpu.VMEM((2, SUBLANES, N_SEL, D_MODEL), jnp.uint32),
            pltpu.SemaphoreType.DMA((2,)),
            pltpu.SemaphoreType.DMA((2,)),
        ],
        compiler_params=_params(("arbitrary",)),
        name="peer_mix",
    )(eidx, h2, gate2, x2d, uv, p2d, ln_ple, w_gate, w_proj, ln_final)


def _rope_tables(pos):
    inv = ROPE_THETA ** (-jnp.arange(0, HEAD_DIM_A, 2, dtype=F32) / HEAD_DIM_A)
    ang = pos.astype(F32)[:, None] * inv[None, :]
    cos, sin = jnp.cos(ang), jnp.sin(ang)
    reps = PROJ_TILE // HEAD_DIM_A
    cos = jnp.tile(jnp.concatenate([cos, cos], axis=1), (1, reps))
    sin = jnp.tile(jnp.concatenate([-sin, sin], axis=1), (1, reps))
    k_end = (COL["k_a"][0] + COL["k_a"][1]) % PROJ_TILE
    if k_end:
        keep = jnp.arange(PROJ_TILE) < k_end
        cos = jnp.concatenate([cos, jnp.where(keep, cos, 1.0)], axis=1)
        sin = jnp.concatenate([sin, jnp.where(keep, sin, 0.0)], axis=1)
    return cos, sin


def _pick(n, prefs):
    for c in prefs:
        if n % c == 0:
            return c
    raise ValueError(f"no tile of {prefs} divides {n}")


def _group(x, pe, pos0, cache_k, cache_v, s0, wts):
    bn, t, _ = x.shape
    n = bn * t
    prompt = cache_k is None
    x2d = x.reshape(n, D_MODEL)
    tm = _pick(n, (1024, 512, 256, 128))

    pos = pos0 + jnp.arange(t, dtype=jnp.int32)
    cos, sin = _rope_tables(pos)
    if t < tm:
        cos, sin = jnp.tile(cos, (tm // t, 1)), jnp.tile(sin, (tm // t, 1))
    proj = _in_proj(x2d, wts["ln1"], wts["w_in"], cos, sin, wts["colscale"], tm)
    proj3 = proj.reshape(bn, t, IN_COLS)

    if prompt:
        c_att, bt_att = ATTN_BLOCK, 1
        ctx_k = ctx_v = proj3
    else:
        c_att, bt_att = t, _pick(bn, (8, 4, 2, 1))
        ctx_k = cache_k.reshape(bn, WINDOW, N_KV_A * HEAD_DIM_A)
        ctx_v = cache_v.reshape(bn, WINDOW, N_KV_A * HEAD_DIM_A)
    o_a = _attention(wts["sinks"], proj3, ctx_k, ctx_v, bt_att, c_att, prompt)

    c_ret = math.gcd(t, RET_CHUNK)
    bt_ret = 1 if prompt else _pick(bn, (8, 4, 2, 1))
    o_r, s_new = _retention(proj3, s0, wts["ln_ret"], bt_ret, c_ret)

    m = _merge(o_a.reshape(n, -1), o_r.reshape(n, -1), proj, wts["w_br_a"], wts["w_br_r"], tm)
    x2 = _out_proj(x2d, m, wts["w_o"], tm)

    h2, scores_t = _scores(x2, wts["ln2"], wts["w_q"], wts["sub_keys"], _pick(n, (512, 256, 128)))
    eidx_t, gate_t = _topk(scores_t, _pick(n, (256, 128)))
    eidx = eidx_t.reshape(N_SEL, n).T
    gate2 = jnp.repeat(gate_t.reshape(N_SEL, n).T, 2, axis=1)
    y = _peer(eidx, h2, gate2, x2, wts["peer_uv"], pe.reshape(n, -1), wts["ln_ple"], wts["w_ple_gate"],
              wts["w_ple_proj"], wts["ln_final"], _pick(n, (64, 32, 16)))

    k_off, v_off = COL["k_a"][0], COL["v_a"][0]
    kv_w = N_KV_A * HEAD_DIM_A
    k_new, v_new = proj3[:, :, k_off:k_off + kv_w], proj3[:, :, v_off:v_off + kv_w]
    if prompt:
        k_win, v_win = k_new[:, -WINDOW:], v_new[:, -WINDOW:]
    else:
        k_win = jnp.concatenate([ctx_k, k_new], axis=1)[:, -WINDOW:]
        v_win = jnp.concatenate([ctx_v, v_new], axis=1)[:, -WINDOW:]
    shp = (bn, WINDOW, N_KV_A, HEAD_DIM_A)
    return y.reshape(bn, t, D_MODEL), k_win.reshape(shp), v_win.reshape(shp), s_new


def kernel(x_prompt, x_sample, cache_k_win, cache_v_win, state_ret, p_prompt, p_sample, ln1, w_in, attn_sinks, ln_ret, w_branch_attn, w_branch_ret, w_out, ln2, w_peer_query, peer_sub_keys, peer_u, peer_v, ln_ple, w_ple_gate, w_ple_proj, ln_final):
    depth = ln1.shape[0]
    assert depth == 1, "single-layer step"
    i = 0
    colscale = jnp.ones((IN_COLS,), F32).at[COL["k_r"][0]:COL["k_r"][0] + COL["k_r"][1]].set(DK_R ** -0.5)
    wts = dict(
        ln1=ln1[i][None, :],
        w_in=jnp.concatenate([w_in[i][:, _REF_COL[f][0]:_REF_COL[f][0] + _REF_COL[f][1]] for f in _NEW_ORDER],
                             axis=1).astype(BF16),
        colscale=colscale[None, :],
        sinks=attn_sinks[i],
        ln_ret=ln_ret[i].reshape(N_HEADS_R, DV_R),
        w_br_a=w_branch_attn[i].astype(BF16),
        w_br_r=w_branch_ret[i].astype(BF16),
        w_o=w_out[i].astype(BF16),
        ln2=ln2[i][None, :],
        w_q=w_peer_query[i].astype(BF16),
        sub_keys=peer_sub_keys[i].astype(BF16),
        peer_uv=_pack_tables(peer_u[i], peer_v[i])[:, None, :],
        ln_ple=ln_ple[i][None, :],
        w_ple_gate=w_ple_gate[i].astype(BF16),
        w_ple_proj=w_ple_proj[i].astype(BF16),
        ln_final=ln_final[None, :],
    )
    yp, kp, vp, sp = _group(x_prompt, p_prompt[i], 0, None, None, None, wts)
    ys, ks, vs, ss = _group(x_sample, p_sample[i], PAST_LEN, cache_k_win[i], cache_v_win[i], state_ret[i], wts)
    return (yp, ys, kp[None], vp[None], sp[None], ks[None], vs[None], ss[None])
```

```python
import functools
import math

import jax
import jax.numpy as jnp
from jax import lax
from jax.experimental import pallas as pl
from jax.experimental.pallas import tpu as pltpu

F32 = jnp.float32
BF16 = jnp.bfloat16

D_MODEL = 2048
PAST_LEN = 16384
HEAD_DIM_A = 64
N_HEADS_A = 16
N_KV_A = 4
GROUP_A = 4
WINDOW = 128
ATTN_BLOCK = 128
N_HEADS_R = 8
DK_R = 64
DV_R = 128
RET_CHUNK = 128
PEER_HEADS = 8
N_KEYS = 128
PEER_TOPK = 16
D_KEY_HALF = 128
N_SEL = PEER_HEADS * PEER_TOPK
PEER_CHUNKS = 4
ROPE_THETA = 10000.0
EPS = 1e-6
NEG_INF = -1e30

LANES = 128
SUBLANES = 8
VMEM_LIMIT_BYTES = 56 * 1024 * 1024

_REF_FIELDS = (("q_a", 1024), ("k_a", 256), ("v_a", 256), ("q_r", 512), ("k_r", 512),
               ("v_r", 1024), ("g_r", 1024), ("gate_a", 2048), ("gate_r", 2048))
_NEW_ORDER = ("gate_a", "gate_r", "q_a", "v_r", "g_r", "q_r", "k_r", "k_a", "v_a")
_ROPE_FIELDS = ("q_a", "q_r", "k_r", "k_a")
PROJ_TILE = 512


def _layout():
    ref_off, o = {}, 0
    for name, w in _REF_FIELDS:
        ref_off[name] = (o, w)
        o += w
    new_off, o = {}, 0
    for name in _NEW_ORDER:
        new_off[name] = (o, ref_off[name][1])
        o += ref_off[name][1]
    return new_off, ref_off, o


COL, _REF_COL, IN_COLS = _layout()


def _blk(name, width):
    off, w = COL[name]
    assert off % width == 0 and w % width == 0
    return off // width


def _params(sem, vmem=VMEM_LIMIT_BYTES):
    return pltpu.CompilerParams(dimension_semantics=sem, vmem_limit_bytes=vmem)


def _rms(x, g):
    return x * lax.rsqrt(jnp.mean(x * x, axis=-1, keepdims=True) + EPS) * g


def _sigmoid(x):
    return 1.0 / (1.0 + jnp.exp(-x))


def _dot(a, b):
    return jnp.dot(a, b, preferred_element_type=F32)


def _dot_nt(a, b):
    return lax.dot_general(a, b, (((1,), (1,)), ((), ())), preferred_element_type=F32)


def _dot_tn(a, b):
    return lax.dot_general(a, b, (((0,), (0,)), ((), ())), preferred_element_type=F32)


def _in_proj_kernel(x_ref, g_ref, w_ref, cos_ref, sin_ref, cs_ref, o_ref, h_ref, *, rope_ranges):
    j = pl.program_id(1)

    @pl.when(j == 0)
    def _():
        h_ref[...] = _rms(x_ref[...], g_ref[...]).astype(BF16)

    acc = _dot(h_ref[...], w_ref[...])
    is_rope = functools.reduce(jnp.logical_or, [(j >= a) & (j < b) for a, b in rope_ranges])

    @pl.when(is_rope)
    def _():
        tn = acc.shape[1]
        lane = lax.broadcasted_iota(jnp.int32, acc.shape, 1)
        first_half = (lane % HEAD_DIM_A) < (HEAD_DIM_A // 2)
        partner = jnp.where(first_half, pltpu.roll(acc, tn - HEAD_DIM_A // 2, 1), pltpu.roll(acc, HEAD_DIM_A // 2, 1))
        o_ref[...] = (acc * cos_ref[...] + partner * sin_ref[...]) * cs_ref[...]

    @pl.when(jnp.logical_not(is_rope))
    def _():
        o_ref[...] = acc


def _in_proj(x2d, ln, w, cos, sin, colscale, tm):
    n = x2d.shape[0]
    rt = cos.shape[0] // tm
    rope_ranges = tuple((COL[f][0] // PROJ_TILE, -(-(COL[f][0] + COL[f][1]) // PROJ_TILE)) for f in _ROPE_FIELDS)
    ends = [COL[f][0] + COL[f][1] for f in _ROPE_FIELDS if (COL[f][0] + COL[f][1]) % PROJ_TILE]
    assert len(ends) <= 1 and cos.shape[1] == (1 + len(ends)) * PROJ_TILE
    mixed = ends[0] // PROJ_TILE if ends else -1
    table_spec = pl.BlockSpec((tm, PROJ_TILE), lambda i, j: (i % rt, jnp.where(j == mixed, 1, 0)))
    return pl.pallas_call(
        functools.partial(_in_proj_kernel, rope_ranges=rope_ranges),
        out_shape=jax.ShapeDtypeStruct((n, IN_COLS), F32),
        grid=(n // tm, IN_COLS // PROJ_TILE),
        in_specs=[
            pl.BlockSpec((tm, D_MODEL), lambda i, j: (i, 0)),
            pl.BlockSpec((1, D_MODEL), lambda i, j: (0, 0)),
            pl.BlockSpec((D_MODEL, PROJ_TILE), lambda i, j: (0, j)),
            table_spec,
            table_spec,
            pl.BlockSpec((1, PROJ_TILE), lambda i, j: (0, j)),
        ],
        out_specs=pl.BlockSpec((tm, PROJ_TILE), lambda i, j: (i, j)),
        scratch_shapes=[pltpu.VMEM((tm, D_MODEL), BF16)],
        compiler_params=_params(("arbitrary", "arbitrary")),
        name="in_proj",
    )(x2d, ln, w, cos, sin, colscale)


def _attn_kernel(sink_ref, q_ref, kc_ref, vc_ref, kx_ref, vx_ref, o_ref, *, first_ctx_invalid):
    bt, c, _ = q_ref.shape
    rows = GROUP_A * c
    qi = lax.broadcasted_iota(jnp.int32, (rows, WINDOW), 0) % c
    kj = lax.broadcasted_iota(jnp.int32, (rows, WINDOW), 1)
    ctx_ok = kj >= qi
    if first_ctx_invalid:
        ctx_ok = jnp.logical_and(ctx_ok, pl.program_id(1) > 0)
    qi_c = lax.broadcasted_iota(jnp.int32, (rows, c), 0) % c
    kj_c = lax.broadcasted_iota(jnp.int32, (rows, c), 1)
    cur_ok = kj_c <= qi_c
    grp = lax.broadcasted_iota(jnp.int32, (rows, 1), 0) // c
    scale = HEAD_DIM_A ** -0.5

    def one(b):
        q = q_ref[b]
        kc, vc, kx, vx = kc_ref[b], vc_ref[b], kx_ref[b], vx_ref[b]
        outs = [None] * N_HEADS_A
        for kh in range(N_KV_A):
            sl = slice(kh * HEAD_DIM_A, (kh + 1) * HEAD_DIM_A)
            q4 = jnp.concatenate(
                [q[:, (kh * GROUP_A + g) * HEAD_DIM_A:(kh * GROUP_A + g + 1) * HEAD_DIM_A] for g in range(GROUP_A)],
                axis=0).astype(BF16)
            s_x = jnp.where(ctx_ok, _dot_nt(q4, kx[:, sl].astype(BF16)) * scale, NEG_INF)
            s_c = jnp.where(cur_ok, _dot_nt(q4, kc[:, sl].astype(BF16)) * scale, NEG_INF)
            sink = jnp.zeros((rows, 1), F32)
            for g in range(GROUP_A):
                sink = jnp.where(grp == g, sink_ref[kh * GROUP_A + g], sink)
            m = jnp.maximum(jnp.maximum(jnp.max(s_x, axis=-1, keepdims=True), jnp.max(s_c, axis=-1, keepdims=True)), sink)
            e_x = jnp.exp(s_x - m)
            e_c = jnp.exp(s_c - m)
            den = jnp.sum(e_x, axis=-1, keepdims=True) + jnp.sum(e_c, axis=-1, keepdims=True) + jnp.exp(sink - m)
            o4 = _dot((e_x / den).astype(BF16), vx[:, sl].astype(BF16)) + _dot((e_c / den).astype(BF16), vc[:, sl].astype(BF16))
            for g in range(GROUP_A):
                outs[kh * GROUP_A + g] = o4[g * c:(g + 1) * c]
        o_ref[b] = jnp.concatenate(outs, axis=1).astype(o_ref.dtype)

    for b in range(bt):
        one(b)


def _attention(sinks, proj3, ctx_k, ctx_v, bt, c, prompt):
    bn, t, _ = proj3.shape
    qb, kb, vb = _blk("q_a", 1024), _blk("k_a", 256), _blk("v_a", 256)
    if prompt:
        ctx_specs = [pl.BlockSpec((bt, WINDOW, 256), lambda b, n: (b, jnp.maximum(n - 1, 0), kb)),
                     pl.BlockSpec((bt, WINDOW, 256), lambda b, n: (b, jnp.maximum(n - 1, 0), vb))]
    else:
        ctx_specs = [pl.BlockSpec((bt, WINDOW, 256), lambda b, n: (b, 0, 0)),
                     pl.BlockSpec((bt, WINDOW, 256), lambda b, n: (b, 0, 0))]
    return pl.pallas_call(
        functools.partial(_attn_kernel, first_ctx_invalid=prompt),
        out_shape=jax.ShapeDtypeStruct((bn, t, N_HEADS_A * HEAD_DIM_A), BF16),
        grid=(bn // bt, t // c),
        in_specs=[
            pl.BlockSpec(memory_space=pltpu.SMEM),
            pl.BlockSpec((bt, c, 1024), lambda b, n: (b, n, qb)),
            pl.BlockSpec((bt, c, 256), lambda b, n: (b, n, kb)),
            pl.BlockSpec((bt, c, 256), lambda b, n: (b, n, vb)),
        ] + ctx_specs,
        out_specs=pl.BlockSpec((bt, c, 1024), lambda b, n: (b, n, 0)),
        compiler_params=_params(("arbitrary", "arbitrary")),
        name="attn_prompt" if prompt else "attn_sample",
    )(sinks, proj3, proj3, proj3, ctx_k, ctx_v)


def _ret_kernel(cdec_ref, q_ref, k_ref, v_ref, g_ref, intra_ref, cross_ref, kdec_ref, ln_ref, *rest, has_s0):
    if has_s0:
        s0_ref, o_ref, sout_ref, s_ref = rest
    else:
        o_ref, sout_ref, s_ref = rest
    bt = q_ref.shape[0]
    ci = pl.program_id(1)

    @pl.when(ci == 0)
    def _():
        s_ref[...] = s0_ref[...] if has_s0 else jnp.zeros(s_ref.shape, F32)

    def one(b):
        q, k, v, gt = q_ref[b], k_ref[b], v_ref[b], g_ref[b]
        outs = []
        for h in range(N_HEADS_R):
            qh = q[:, h * DK_R:(h + 1) * DK_R].astype(BF16)
            kh = k[:, h * DK_R:(h + 1) * DK_R]
            vh = v[:, h * DV_R:(h + 1) * DV_R].astype(BF16)
            s = s_ref[b, h]
            att = _dot_nt(qh, kh.astype(BF16)) * intra_ref[h]
            o = _dot(att.astype(BF16), vh) + _dot(qh, s.astype(BF16)) * cross_ref[h]
            s_ref[b, h] = s * cdec_ref[h] + _dot_tn((kh * kdec_ref[h]).astype(BF16), vh)
            mu = jnp.mean(o, axis=-1, keepdims=True)
            d = o - mu
            var = jnp.mean(d * d, axis=-1, keepdims=True)
            gh = gt[:, h * DV_R:(h + 1) * DV_R]
            outs.append(d * lax.rsqrt(var + EPS) * ln_ref[h:h + 1, :] * (gh * _sigmoid(gh)))
        o_ref[b] = jnp.concatenate(outs, axis=1).astype(o_ref.dtype)

    for b in range(bt):
        one(b)

    @pl.when(ci == pl.num_programs(1) - 1)
    def _():
        sout_ref[...] = s_ref[...]


def _retention(proj3, s0, ln_ret, bt, c):
    bn, t, _ = proj3.shape
    log_g = jnp.log1p(-jnp.exp2(-5.0 - jnp.arange(N_HEADS_R, dtype=F32)))
    i = jnp.arange(c, dtype=F32)
    diff = i[:, None] - i[None, :]
    intra = jnp.where(diff[None] >= 0, jnp.exp(jnp.maximum(diff, 0.0)[None] * log_g[:, None, None]), 0.0)
    cross = jnp.exp((i + 1.0)[None, :] * log_g[:, None])
    kdec = jnp.exp((c - 1.0 - i)[None, :] * log_g[:, None])
    cdec = jnp.exp(c * log_g)
    cross_b = jnp.broadcast_to(cross[:, :, None], (N_HEADS_R, c, DV_R))
    kdec_b = jnp.broadcast_to(kdec[:, :, None], (N_HEADS_R, c, DK_R))
    has_s0 = s0 is not None
    const3 = lambda b, n: (0, 0, 0)
    state_spec = pl.BlockSpec((bt, N_HEADS_R, DK_R, DV_R), lambda b, n: (b, 0, 0, 0))
    in_specs = [
        pl.BlockSpec(memory_space=pltpu.SMEM),
        pl.BlockSpec((bt, c, 512), lambda b, n: (b, n, _blk("q_r", 512))),
        pl.BlockSpec((bt, c, 512), lambda b, n: (b, n, _blk("k_r", 512))),
        pl.BlockSpec((bt, c, 1024), lambda b, n: (b, n, _blk("v_r", 1024))),
        pl.BlockSpec((bt, c, 1024), lambda b, n: (b, n, _blk("g_r", 1024))),
        pl.BlockSpec((N_HEADS_R, c, c), const3),
        pl.BlockSpec((N_HEADS_R, c, DV_R), const3),
        pl.BlockSpec((N_HEADS_R, c, DK_R), const3),
        pl.BlockSpec((N_HEADS_R, DV_R), lambda b, n: (0, 0)),
    ]
    args = [cdec, proj3, proj3, proj3, proj3, intra, cross_b, kdec_b, ln_ret]
    if has_s0:
        in_specs.append(state_spec)
        args.append(s0)
    return pl.pallas_call(
        functools.partial(_ret_kernel, has_s0=has_s0),
        out_shape=(jax.ShapeDtypeStruct((bn, t, N_HEADS_R * DV_R), BF16),
                   jax.ShapeDtypeStruct((bn, N_HEADS_R, DK_R, DV_R), F32)),
        grid=(bn // bt, t // c),
        in_specs=in_specs,
        out_specs=(pl.BlockSpec((bt, c, 1024), lambda b, n: (b, n, 0)), state_spec),
        scratch_shapes=[pltpu.VMEM((bt, N_HEADS_R, DK_R, DV_R), F32)],
        compiler_params=_params(("arbitrary", "arbitrary")),
        name="ret_sample" if has_s0 else "ret_prompt",
    )(*args)


def _merge_kernel(oa_ref, or_ref, ga_ref, gr_ref, wa_ref, wr_ref, m_ref):
    br_a = _dot(oa_ref[...], wa_ref[...])
    br_r = _dot(or_ref[...], wr_ref[...])
    m_ref[...] = (_sigmoid(ga_ref[...]) * br_a + _sigmoid(gr_ref[...]) * br_r).astype(m_ref.dtype)


def _merge(o_a, o_r, proj, w_a, w_r, tm, tn=512):
    n = o_a.shape[0]
    ga, gr = _blk("gate_a", tn), _blk("gate_r", tn)
    return pl.pallas_call(
        _merge_kernel,
        out_shape=jax.ShapeDtypeStruct((n, D_MODEL), BF16),
        grid=(n // tm, D_MODEL // tn),
        in_specs=[
            pl.BlockSpec((tm, 1024), lambda i, j: (i, 0)),
            pl.BlockSpec((tm, 1024), lambda i, j: (i, 0)),
            pl.BlockSpec((tm, tn), lambda i, j: (i, ga + j)),
            pl.BlockSpec((tm, tn), lambda i, j: (i, gr + j)),
            pl.BlockSpec((1024, tn), lambda i, j: (0, j)),
            pl.BlockSpec((1024, tn), lambda i, j: (0, j)),
        ],
        out_specs=pl.BlockSpec((tm, tn), lambda i, j: (i, j)),
        compiler_params=_params(("arbitrary", "arbitrary")),
        name="merge",
    )(o_a, o_r, proj, proj, w_a, w_r)


def _out_proj_kernel(x_ref, m_ref, w_ref, o_ref):
    o_ref[...] = x_ref[...] + _dot(m_ref[...], w_ref[...])


def _out_proj(x2d, m, w_o, tm, tn=512):
    n = x2d.shape[0]
    return pl.pallas_call(
        _out_proj_kernel,
        out_shape=jax.ShapeDtypeStruct((n, D_MODEL), F32),
        grid=(n // tm, D_MODEL // tn),
        in_specs=[
            pl.BlockSpec((tm, tn), lambda i, j: (i, j)),
            pl.BlockSpec((tm, D_MODEL), lambda i, j: (i, 0)),
            pl.BlockSpec((D_MODEL, tn), lambda i, j: (0, j)),
        ],
        out_specs=pl.BlockSpec((tm, tn), lambda i, j: (i, j)),
        compiler_params=_params(("arbitrary", "arbitrary")),
        name="out_proj",
    )(x2d, m, w_o)


def _scores_kernel(x_ref, g_ref, w_ref, sk_ref, h_out_ref, s_ref, h_ref):
    j = pl.program_id(1)

    @pl.when(j == 0)
    def _():
        h = _rms(x_ref[...], g_ref[...])
        h_ref[...] = h.astype(BF16)
        h_out_ref[...] = h

    qry = _dot(h_ref[...], w_ref[...]).astype(BF16)
    for g in range(qry.shape[1] // D_KEY_HALF):
        s_ref[g] = _dot_nt(sk_ref[g % 2], qry[:, g * D_KEY_HALF:(g + 1) * D_KEY_HALF])


def _scores(x2d, ln, w_q, sub_keys, tm, tn=512):
    n = x2d.shape[0]
    ng = tn // D_KEY_HALF
    return pl.pallas_call(
        _scores_kernel,
        out_shape=(jax.ShapeDtypeStruct((n, D_MODEL), F32),
                   jax.ShapeDtypeStruct((2 * PEER_HEADS, N_KEYS, n), F32)),
        grid=(n // tm, D_MODEL // tn),
        in_specs=[
            pl.BlockSpec((tm, D_MODEL), lambda i, j: (i, 0)),
            pl.BlockSpec((1, D_MODEL), lambda i, j: (0, 0)),
            pl.BlockSpec((D_MODEL, tn), lambda i, j: (0, j)),
            pl.BlockSpec((2, N_KEYS, D_KEY_HALF), lambda i, j: (0, 0, 0)),
        ],
        out_specs=(pl.BlockSpec((tm, D_MODEL), lambda i, j: (i, 0)),
                   pl.BlockSpec((ng, N_KEYS, tm), lambda i, j: (j, 0, i))),
        scratch_shapes=[pltpu.VMEM((tm, D_MODEL), BF16)],
        compiler_params=_params(("arbitrary", "arbitrary")),
        name="peer_scores",
    )(x2d, ln, w_q, sub_keys)


def _take_top(vals, iota, count, fill, payload=None):
    n_rows = vals.shape[0]
    tags = (iota,) if payload is None else (iota, payload)

    def first_max(v):
        row8 = lax.broadcasted_iota(jnp.int32, (SUBLANES, v.shape[1]), 0)
        blocks = [(v[i:i + SUBLANES], row8 + i) + tuple(t[i:i + SUBLANES] for t in tags[1:])
                  for i in range(0, n_rows, SUBLANES)]
        while len(blocks) > 1:
            nxt = []
            for a, b in zip(blocks[0::2], blocks[1::2]):
                take_a = a[0] >= b[0]
                nxt.append(tuple(jnp.where(take_a, x, y) for x, y in zip(a, b)))
            blocks = nxt + ([blocks[-1]] if len(blocks) % 2 else [])
        cur = blocks[0]
        m = jnp.max(cur[0], axis=0, keepdims=True)
        pos = jnp.min(jnp.where(cur[0] == m, cur[1], n_rows), axis=0, keepdims=True)
        rest = tuple(jnp.max(jnp.where(cur[1] == pos, t, -1), axis=0, keepdims=True) for t in cur[2:])
        return (m, pos) + rest

    top_v, top_i = [], []
    for _ in range(count):
        best = first_max(vals)
        top_v.append(best[0])
        top_i.append(best[-1])
        vals = jnp.where(iota == best[1], fill, vals)
    return jnp.concatenate(top_v, axis=0), jnp.concatenate(top_i, axis=0)


_COMB_PIECES = tuple((i, PEER_TOPK // (i + 1)) for i in range(SUBLANES))
_COMB_ROWS = sum(max(nj, SUBLANES) for _, nj in _COMB_PIECES) + SUBLANES


def _topk_kernel(s_ref, e_ref, g_ref):
    tt = s_ref.shape[2]
    key_iota = lax.broadcasted_iota(jnp.int32, (N_KEYS, tt), 0)
    comb_iota = lax.broadcasted_iota(jnp.int32, (_COMB_ROWS, tt), 0)
    ninf = float("-inf")

    def head(h, carry):
        s0, i0 = _take_top(s_ref[2 * h], key_iota, PEER_TOPK, ninf)
        s1, i1 = _take_top(s_ref[2 * h + 1], key_iota, PEER_TOPK, ninf)
        comb, cidx = [], []
        for i, nj in _COMB_PIECES:
            rows = max(nj, SUBLANES)
            sv = s0[i:i + 1, :] + s1[:rows]
            iv = i0[i:i + 1, :] * N_KEYS + i1[:rows]
            if nj < rows:
                keep = lax.broadcasted_iota(jnp.int32, (rows, tt), 0) < nj
                sv, iv = jnp.where(keep, sv, ninf), jnp.where(keep, iv, -1)
            comb.append(sv)
            cidx.append(iv)
        comb.append(s0[SUBLANES:] + s1[0:1, :])
        cidx.append(i0[SUBLANES:] * N_KEYS + i1[0:1, :])
        comb, cidx = jnp.concatenate(comb, axis=0), jnp.concatenate(cidx, axis=0)
        best, eidx = _take_top(comb, comb_iota, PEER_TOPK, ninf, payload=cidx)
        ex = jnp.exp(best - jnp.max(best, axis=0, keepdims=True))
        g_ref[h] = ex / jnp.sum(ex, axis=0, keepdims=True)
        e_ref[h] = eidx
        return carry

    lax.fori_loop(0, PEER_HEADS, head, 0)


def _topk(scores_t, tt=256):
    n = scores_t.shape[2]
    out_spec = pl.BlockSpec((PEER_HEADS, PEER_TOPK, tt), lambda i: (0, 0, i))
    return pl.pallas_call(
        _topk_kernel,
        out_shape=(jax.ShapeDtypeStruct((PEER_HEADS, PEER_TOPK, n), jnp.int32),
                   jax.ShapeDtypeStruct((PEER_HEADS, PEER_TOPK, n), F32)),
        grid=(n // tt,),
        in_specs=[pl.BlockSpec((2 * PEER_HEADS, N_KEYS, tt), lambda i: (0, 0, i))],
        out_specs=(out_spec, out_spec),
        compiler_params=_params(("arbitrary",)),
        name="peer_topk",
    )(scores_t)


def _gelu(a):
    return 0.5 * a * (1.0 + lax.erf(a * (2.0 ** -0.5)))


def _peer_kernel(idx_hbm, h_ref, g2_ref, x_ref, uv_hbm, o_ref, idx_smem, buf, sem_i, sem):
    tt = h_ref.shape[0]
    half = D_MODEL // 2
    step = pl.program_id(0)
    more = step + 1 < pl.num_programs(0)
    cur = step % 2

    def idx_copy(s, islot):
        return pltpu.make_async_copy(idx_hbm.at[pl.ds(s * tt, tt), :], idx_smem.at[islot], sem_i.at[islot])

    n_groups = tt // SUBLANES
    n_parts = 2 * PEER_CHUNKS
    cw = half // PEER_CHUNKS

    def gather_start(islot, t, bank, j, part=None):
        ks = range(N_SEL) if part is None else range(part * N_SEL // n_parts, (part + 1) * N_SEL // n_parts)
        for k in ks:
            e = idx_smem[islot, t, k]
            pltpu.make_async_copy(uv_hbm.at[e], buf.at[bank, j, pl.ds(k, 1), :],
                                  sem.at[bank]).start(priority=k % 2)

    def gather_wait(bank):
        pltpu.make_async_copy(buf.at[bank], buf.at[bank], sem.at[bank]).wait()

    @pl.when(step == 0)
    def _():
        first = idx_copy(0, 0)
        first.start()
        first.wait()
        for j in range(SUBLANES):
            gather_start(0, j, 0, j)

    @pl.when(more)
    def _():
        idx_copy(step + 1, 1 - cur).start()

    row_iota = lax.broadcasted_iota(jnp.int32, (SUBLANES, 1), 0)
    even = lax.broadcasted_iota(jnp.int32, (SUBLANES, 2 * N_SEL), 1) % 2 == 0

    def group(gi, carry):
        r0 = pl.multiple_of(gi * SUBLANES, SUBLANES)
        last = gi == n_groups - 1
        bank = gi % 2

        @pl.when(jnp.logical_and(last, more))
        def _():
            idx_copy(step + 1, 1 - cur).wait()

        n_islot = jnp.where(jnp.logical_and(last, more), 1 - cur, cur)
        n_t0 = jnp.where(last, 0, r0 + SUBLANES)
        gather_wait(bank)
        h8 = h_ref[pl.ds(r0, SUBLANES), :]
        g2 = g2_ref[pl.ds(r0, SUBLANES), :]
        acc = [jnp.zeros((2 * SUBLANES, cw), F32) for _ in range(PEER_CHUNKS)]
        issued = [0]

        def request():
            q = issued[0]
            issued[0] += 1
            gather_start(n_islot, n_t0 + q // n_parts, 1 - bank, q // n_parts, part=q % n_parts)

        def u_dot(r, c, p):
            sel = row_iota == r
            hm = jnp.concatenate([jnp.where(sel, h8[:, c * cw:(c + 1) * cw], 0.0),
                                  jnp.where(sel, h8[:, half + c * cw:half + (c + 1) * cw], 0.0)],
                                 axis=0).astype(BF16)
            ub = pltpu.bitcast(buf[bank, r, :, c * cw:(c + 1) * cw], BF16)
            return p + _dot_nt(hm, ub)

        def mix_weights(p):
            part = jnp.where(even, p[:SUBLANES], p[SUBLANES:])
            a2 = part + jnp.where(even, pltpu.roll(part, 2 * N_SEL - 1, 1), pltpu.roll(part, 1, 1))
            w2 = _gelu(a2) * g2
            return jnp.concatenate([jnp.where(even, w2, 0.0), jnp.where(even, 0.0, w2)], axis=0).astype(BF16)

        def v_dot(r, c, wm):
            vb = pltpu.bitcast(buf[bank, r, :, half + c * cw:half + (c + 1) * cw], BF16)
            acc[c] = acc[c] + _dot(wm, vb)

        p_next = jnp.zeros((2 * SUBLANES, 2 * N_SEL), F32)
        for c in range(PEER_CHUNKS):
            request()
            p_next = u_dot(0, c, p_next)
        for r in range(SUBLANES):
            wm = mix_weights(p_next)
            p_next = jnp.zeros((2 * SUBLANES, 2 * N_SEL), F32)
            for c in range(PEER_CHUNKS):
                if r + 1 < SUBLANES:
                    request()
                    p_next = u_dot(r + 1, c, p_next)
                request()
                v_dot(r, c, wm)
        assert issued[0] == SUBLANES * n_parts
        peer = jnp.concatenate([a[:SUBLANES] for a in acc] + [a[SUBLANES:] for a in acc], axis=1)
        o_ref[pl.ds(r0, SUBLANES), :] = x_ref[pl.ds(r0, SUBLANES), :] + peer
        return carry

    lax.fori_loop(0, n_groups, group, 0)

    @pl.when(jnp.logical_not(more))
    def _():
        gather_wait(n_groups % 2)


def _pack_kernel(u_ref, v_ref, o_ref):
    half = D_MODEL // 2

    def words(t):
        bits = lax.bitcast_convert_type(t.astype(BF16).astype(F32), jnp.uint32)
        return (bits[:, :half] >> 16) | bits[:, half:]

    o_ref[:, :half] = words(u_ref[...])
    o_ref[:, half:] = words(v_ref[...])


def _pack_tables(tab_u, tab_v, tm=512):
    e = tab_u.shape[0]
    row = lambda i: (i, 0)
    return pl.pallas_call(
        _pack_kernel,
        out_shape=jax.ShapeDtypeStruct((e, D_MODEL), jnp.uint32),
        grid=(e // tm,),
        in_specs=[pl.BlockSpec((tm, D_MODEL), row), pl.BlockSpec((tm, D_MODEL), row)],
        out_specs=pl.BlockSpec((tm, D_MODEL), row),
        compiler_params=_params(("arbitrary",)),
        name="peer_pack",
    )(tab_u, tab_v)


def _peer(eidx, h2, gate2, x2d, uv, tt):
    n = x2d.shape[0]
    assert tt % (2 * SUBLANES) == 0
    row = lambda i: (i, 0)
    return pl.pallas_call(
        _peer_kernel,
        out_shape=jax.ShapeDtypeStruct((n, D_MODEL), F32),
        grid=(n // tt,),
        in_specs=[
            pl.BlockSpec(memory_space=pl.ANY),
            pl.BlockSpec((tt, D_MODEL), row),
            pl.BlockSpec((tt, 2 * N_SEL), row),
            pl.BlockSpec((tt, D_MODEL), row),
            pl.BlockSpec(memory_space=pl.ANY),
        ],
        out_specs=pl.BlockSpec((tt, D_MODEL), row),
        scratch_shapes=[
            pltpu.SMEM((2, tt, N_SEL), jnp.int32),
            pltpu.VMEM((2, SUBLANES, N_SEL, D_MODEL), jnp.uint32),
            pltpu.SemaphoreType.DMA((2,)),
            pltpu.SemaphoreType.DMA((2,)),
        ],
        compiler_params=_params(("arbitrary",)),
        name="peer_mix",
    )(eidx, h2, gate2, x2d, uv)


def _ple_kernel(x_ref, p_ref, lnp_ref, wg_ref, wp_ref, lnf_ref, y_ref):
    x = x_ref[...]
    gate = _sigmoid(_dot(_rms(x, lnp_ref[...]).astype(BF16), wg_ref[...]))
    x = x + gate * _dot(p_ref[...].astype(BF16), wp_ref[...])
    y_ref[...] = _rms(x, lnf_ref[...])


def _ple(x2d, p2d, ln_ple, w_gate, w_proj, ln_final, tm):
    n = x2d.shape[0]
    pd = p2d.shape[1]
    row = lambda i: (i, 0)
    const = lambda i: (0, 0)
    return pl.pallas_call(
        _ple_kernel,
        out_shape=jax.ShapeDtypeStruct((n, D_MODEL), F32),
        grid=(n // tm,),
        in_specs=[
            pl.BlockSpec((tm, D_MODEL), row),
            pl.BlockSpec((tm, pd), row),
            pl.BlockSpec((1, D_MODEL), const),
            pl.BlockSpec((D_MODEL, D_MODEL), const),
            pl.BlockSpec((pd, D_MODEL), const),
            pl.BlockSpec((1, D_MODEL), const),
        ],
        out_specs=pl.BlockSpec((tm, D_MODEL), row),
        compiler_params=_params(("arbitrary",)),
        name="ple_final",
    )(x2d, p2d, ln_ple, w_gate, w_proj, ln_final)


def _rope_tables(pos):
    inv = ROPE_THETA ** (-jnp.arange(0, HEAD_DIM_A, 2, dtype=F32) / HEAD_DIM_A)
    ang = pos.astype(F32)[:, None] * inv[None, :]
    cos, sin = jnp.cos(ang), jnp.sin(ang)
    reps = PROJ_TILE // HEAD_DIM_A
    cos = jnp.tile(jnp.concatenate([cos, cos], axis=1), (1, reps))
    sin = jnp.tile(jnp.concatenate([-sin, sin], axis=1), (1, reps))
    k_end = (COL["k_a"][0] + COL["k_a"][1]) % PROJ_TILE
    if k_end:
        keep = jnp.arange(PROJ_TILE) < k_end
        cos = jnp.concatenate([cos, jnp.where(keep, cos, 1.0)], axis=1)
        sin = jnp.concatenate([sin, jnp.where(keep, sin, 0.0)], axis=1)
    return cos, sin


def _pick(n, prefs):
    for c in prefs:
        if n % c == 0:
            return c
    raise ValueError(f"no tile of {prefs} divides {n}")


def _group(x, pe, pos0, cache_k, cache_v, s0, wts):
    bn, t, _ = x.shape
    n = bn * t
    prompt = cache_k is None
    x2d = x.reshape(n, D_MODEL)
    tm = _pick(n, (1024, 512, 256, 128))

    pos = pos0 + jnp.arange(t, dtype=jnp.int32)
    cos, sin = _rope_tables(pos)
    if t < tm:
        cos, sin = jnp.tile(cos, (tm // t, 1)), jnp.tile(sin, (tm // t, 1))
    proj = _in_proj(x2d, wts["ln1"], wts["w_in"], cos, sin, wts["colscale"], tm)
    proj3 = proj.reshape(bn, t, IN_COLS)

    if prompt:
        c_att, bt_att = ATTN_BLOCK, 1
        ctx_k = ctx_v = proj3
    else:
        c_att, bt_att = t, _pick(bn, (8, 4, 2, 1))
        ctx_k = cache_k.reshape(bn, WINDOW, N_KV_A * HEAD_DIM_A)
        ctx_v = cache_v.reshape(bn, WINDOW, N_KV_A * HEAD_DIM_A)
    o_a = _attention(wts["sinks"], proj3, ctx_k, ctx_v, bt_att, c_att, prompt)

    c_ret = math.gcd(t, RET_CHUNK)
    bt_ret = 1 if prompt else _pick(bn, (8, 4, 2, 1))
    o_r, s_new = _retention(proj3, s0, wts["ln_ret"], bt_ret, c_ret)

    m = _merge(o_a.reshape(n, -1), o_r.reshape(n, -1), proj, wts["w_br_a"], wts["w_br_r"], tm)
    x2 = _out_proj(x2d, m, wts["w_o"], tm)

    h2, scores_t = _scores(x2, wts["ln2"], wts["w_q"], wts["sub_keys"], _pick(n, (512, 256, 128)))
    eidx_t, gate_t = _topk(scores_t, _pick(n, (512, 256, 128)))
    eidx = eidx_t.reshape(N_SEL, n).T
    gate2 = jnp.repeat(gate_t.reshape(N_SEL, n).T, 2, axis=1)
    x3 = _peer(eidx, h2, gate2, x2, wts["peer_uv"], _pick(n, (64, 32, 16)))

    y = _ple(x3, pe.reshape(n, -1), wts["ln_ple"], wts["w_ple_gate"], wts["w_ple_proj"], wts["ln_final"],
             _pick(n, (256, 128)))

    k_off, v_off = COL["k_a"][0], COL["v_a"][0]
    kv_w = N_KV_A * HEAD_DIM_A
    k_new, v_new = proj3[:, :, k_off:k_off + kv_w], proj3[:, :, v_off:v_off + kv_w]
    if prompt:
        k_win, v_win = k_new[:, -WINDOW:], v_new[:, -WINDOW:]
    else:
        k_win = jnp.concatenate([ctx_k, k_new], axis=1)[:, -WINDOW:]
        v_win = jnp.concatenate([ctx_v, v_new], axis=1)[:, -WINDOW:]
    shp = (bn, WINDOW, N_KV_A, HEAD_DIM_A)
    return y.reshape(bn, t, D_MODEL), k_win.reshape(shp), v_win.reshape(shp), s_new


def kernel(x_prompt, x_sample, cache_k_win, cache_v_win, state_ret, p_prompt, p_sample, ln1, w_in, attn_sinks, ln_ret, w_branch_attn, w_branch_ret, w_out, ln2, w_peer_query, peer_sub_keys, peer_u, peer_v, ln_ple, w_ple_gate, w_ple_proj, ln_final):
    depth = ln1.shape[0]
    assert depth == 1, "single-layer step"
    i = 0
    colscale = jnp.ones((IN_COLS,), F32).at[COL["k_r"][0]:COL["k_r"][0] + COL["k_r"][1]].set(DK_R ** -0.5)
    wts = dict(
        ln1=ln1[i][None, :],
        w_in=jnp.concatenate([w_in[i][:, _REF_COL[f][0]:_REF_COL[f][0] + _REF_COL[f][1]] for f in _NEW_ORDER],
                             axis=1).astype(BF16),
        colscale=colscale[None, :],
        sinks=attn_sinks[i],
        ln_ret=ln_ret[i].reshape(N_HEADS_R, DV_R),
        w_br_a=w_branch_attn[i].astype(BF16),
        w_br_r=w_branch_ret[i].astype(BF16),
        w_o=w_out[i].astype(BF16),
        ln2=ln2[i][None, :],
        w_q=w_peer_query[i].astype(BF16),
        sub_keys=peer_sub_keys[i].astype(BF16),
        peer_uv=_pack_tables(peer_u[i], peer_v[i])[:, None, :],
        ln_ple=ln_ple[i][None, :],
        w_ple_gate=w_ple_gate[i].astype(BF16),
        w_ple_proj=w_ple_proj[i].astype(BF16),
        ln_final=ln_final[None, :],
    )
    yp, kp, vp, sp = _group(x_prompt, p_prompt[i], 0, None, None, None, wts)
    ys, ks, vs, ss = _group(x_sample, p_sample[i], PAST_LEN, cache_k_win[i], cache_v_win[i], state_ret[i], wts)
    return (yp, ys, kp[None], vp[None], sp[None], ks[None], vs[None], ss[None])
```

```python
import functools
import math

import jax
import jax.numpy as jnp
from jax import lax
from jax.experimental import pallas as pl
from jax.experimental.pallas import tpu as pltpu

F32 = jnp.float32
BF16 = jnp.bfloat16

D_MODEL = 2048
PAST_LEN = 16384
HEAD_DIM_A = 64
N_HEADS_A = 16
N_KV_A = 4
GROUP_A = 4
WINDOW = 128
ATTN_BLOCK = 128
N_HEADS_R = 8
DK_R = 64
DV_R = 128
RET_CHUNK = 128
PEER_HEADS = 8
N_KEYS = 128
PEER_TOPK = 16
D_KEY_HALF = 128
N_SEL = PEER_HEADS * PEER_TOPK
PEER_CHUNKS = 4
ROPE_THETA = 10000.0
EPS = 1e-6
NEG_INF = -1e30

LANES = 128
SUBLANES = 8
VMEM_LIMIT_BYTES = 56 * 1024 * 1024

_REF_FIELDS = (("q_a", 1024), ("k_a", 256), ("v_a", 256), ("q_r", 512), ("k_r", 512),
               ("v_r", 1024), ("g_r", 1024), ("gate_a", 2048), ("gate_r", 2048))
_NEW_ORDER = ("gate_a", "gate_r", "q_a", "v_r", "g_r", "q_r", "k_r", "k_a", "v_a")
_ROPE_FIELDS = ("q_a", "q_r", "k_r", "k_a")
PROJ_TILE = 512


def _layout():
    ref_off, o = {}, 0
    for name, w in _REF_FIELDS:
        ref_off[name] = (o, w)
        o += w
    new_off, o = {}, 0
    for name in _NEW_ORDER:
        new_off[name] = (o, ref_off[name][1])
        o += ref_off[name][1]
    return new_off, ref_off, o


COL, _REF_COL, IN_COLS = _layout()


def _blk(name, width):
    off, w = COL[name]
    assert off % width == 0 and w % width == 0
    return off // width


def _params(sem, vmem=VMEM_LIMIT_BYTES):
    return pltpu.CompilerParams(dimension_semantics=sem, vmem_limit_bytes=vmem)


def _rms(x, g):
    return x * lax.rsqrt(jnp.mean(x * x, axis=-1, keepdims=True) + EPS) * g


def _sigmoid(x):
    return 1.0 / (1.0 + jnp.exp(-x))


def _dot(a, b):
    return jnp.dot(a, b, preferred_element_type=F32)


def _dot_nt(a, b):
    return lax.dot_general(a, b, (((1,), (1,)), ((), ())), preferred_element_type=F32)


def _dot_tn(a, b):
    return lax.dot_general(a, b, (((0,), (0,)), ((), ())), preferred_element_type=F32)


def _in_proj_kernel(x_ref, g_ref, w_ref, cos_ref, sin_ref, cs_ref, o_ref, h_ref, *, rope_ranges):
    j = pl.program_id(1)

    @pl.when(j == 0)
    def _():
        h_ref[...] = _rms(x_ref[...], g_ref[...]).astype(BF16)

    acc = _dot(h_ref[...], w_ref[...])
    is_rope = functools.reduce(jnp.logical_or, [(j >= a) & (j < b) for a, b in rope_ranges])

    @pl.when(is_rope)
    def _():
        tn = acc.shape[1]
        lane = lax.broadcasted_iota(jnp.int32, acc.shape, 1)
        first_half = (lane % HEAD_DIM_A) < (HEAD_DIM_A // 2)
        partner = jnp.where(first_half, pltpu.roll(acc, tn - HEAD_DIM_A // 2, 1), pltpu.roll(acc, HEAD_DIM_A // 2, 1))
        o_ref[...] = (acc * cos_ref[...] + partner * sin_ref[...]) * cs_ref[...]

    @pl.when(jnp.logical_not(is_rope))
    def _():
        o_ref[...] = acc


def _in_proj(x2d, ln, w, cos, sin, colscale, tm):
    n = x2d.shape[0]
    rt = cos.shape[0] // tm
    rope_ranges = tuple((COL[f][0] // PROJ_TILE, -(-(COL[f][0] + COL[f][1]) // PROJ_TILE)) for f in _ROPE_FIELDS)
    ends = [COL[f][0] + COL[f][1] for f in _ROPE_FIELDS if (COL[f][0] + COL[f][1]) % PROJ_TILE]
    assert len(ends) <= 1 and cos.shape[1] == (1 + len(ends)) * PROJ_TILE
    mixed = ends[0] // PROJ_TILE if ends else -1
    table_spec = pl.BlockSpec((tm, PROJ_TILE), lambda i, j: (i % rt, jnp.where(j == mixed, 1, 0)))
    return pl.pallas_call(
        functools.partial(_in_proj_kernel, rope_ranges=rope_ranges),
        out_shape=jax.ShapeDtypeStruct((n, IN_COLS), F32),
        grid=(n // tm, IN_COLS // PROJ_TILE),
        in_specs=[
            pl.BlockSpec((tm, D_MODEL), lambda i, j: (i, 0)),
            pl.BlockSpec((1, D_MODEL), lambda i, j: (0, 0)),
            pl.BlockSpec((D_MODEL, PROJ_TILE), lambda i, j: (0, j)),
            table_spec,
            table_spec,
            pl.BlockSpec((1, PROJ_TILE), lambda i, j: (0, j)),
        ],
        out_specs=pl.BlockSpec((tm, PROJ_TILE), lambda i, j: (i, j)),
        scratch_shapes=[pltpu.VMEM((tm, D_MODEL), BF16)],
        compiler_params=_params(("arbitrary", "arbitrary")),
        name="in_proj",
    )(x2d, ln, w, cos, sin, colscale)


def _attn_kernel(sink_ref, q_ref, kc_ref, vc_ref, kx_ref, vx_ref, o_ref, *, first_ctx_invalid):
    bt, c, _ = q_ref.shape
    rows = GROUP_A * c
    qi = lax.broadcasted_iota(jnp.int32, (rows, WINDOW), 0) % c
    kj = lax.broadcasted_iota(jnp.int32, (rows, WINDOW), 1)
    ctx_ok = kj >= qi
    if first_ctx_invalid:
        ctx_ok = jnp.logical_and(ctx_ok, pl.program_id(1) > 0)
    qi_c = lax.broadcasted_iota(jnp.int32, (rows, c), 0) % c
    kj_c = lax.broadcasted_iota(jnp.int32, (rows, c), 1)
    cur_ok = kj_c <= qi_c
    grp = lax.broadcasted_iota(jnp.int32, (rows, 1), 0) // c
    scale = HEAD_DIM_A ** -0.5

    def one(b):
        q = q_ref[b]
        kc, vc, kx, vx = kc_ref[b], vc_ref[b], kx_ref[b], vx_ref[b]
        outs = [None] * N_HEADS_A
        for kh in range(N_KV_A):
            sl = slice(kh * HEAD_DIM_A, (kh + 1) * HEAD_DIM_A)
            q4 = jnp.concatenate(
                [q[:, (kh * GROUP_A + g) * HEAD_DIM_A:(kh * GROUP_A + g + 1) * HEAD_DIM_A] for g in range(GROUP_A)],
                axis=0).astype(BF16)
            s_x = jnp.where(ctx_ok, _dot_nt(q4, kx[:, sl].astype(BF16)) * scale, NEG_INF)
            s_c = jnp.where(cur_ok, _dot_nt(q4, kc[:, sl].astype(BF16)) * scale, NEG_INF)
            sink = jnp.zeros((rows, 1), F32)
            for g in range(GROUP_A):
                sink = jnp.where(grp == g, sink_ref[kh * GROUP_A + g], sink)
            m = jnp.maximum(jnp.maximum(jnp.max(s_x, axis=-1, keepdims=True), jnp.max(s_c, axis=-1, keepdims=True)), sink)
            e_x = jnp.exp(s_x - m)
            e_c = jnp.exp(s_c - m)
            den = jnp.sum(e_x, axis=-1, keepdims=True) + jnp.sum(e_c, axis=-1, keepdims=True) + jnp.exp(sink - m)
            o4 = _dot((e_x / den).astype(BF16), vx[:, sl].astype(BF16)) + _dot((e_c / den).astype(BF16), vc[:, sl].astype(BF16))
            for g in range(GROUP_A):
                outs[kh * GROUP_A + g] = o4[g * c:(g + 1) * c]
        o_ref[b] = jnp.concatenate(outs, axis=1).astype(o_ref.dtype)

    for b in range(bt):
        one(b)


def _attention(sinks, proj3, ctx_k, ctx_v, bt, c, prompt):
    bn, t, _ = proj3.shape
    qb, kb, vb = _blk("q_a", 1024), _blk("k_a", 256), _blk("v_a", 256)
    if prompt:
        ctx_specs = [pl.BlockSpec((bt, WINDOW, 256), lambda b, n: (b, jnp.maximum(n - 1, 0), kb)),
                     pl.BlockSpec((bt, WINDOW, 256), lambda b, n: (b, jnp.maximum(n - 1, 0), vb))]
    else:
        ctx_specs = [pl.BlockSpec((bt, WINDOW, 256), lambda b, n: (b, 0, 0)),
                     pl.BlockSpec((bt, WINDOW, 256), lambda b, n: (b, 0, 0))]
    return pl.pallas_call(
        functools.partial(_attn_kernel, first_ctx_invalid=prompt),
        out_shape=jax.ShapeDtypeStruct((bn, t, N_HEADS_A * HEAD_DIM_A), BF16),
        grid=(bn // bt, t // c),
        in_specs=[
            pl.BlockSpec(memory_space=pltpu.SMEM),
            pl.BlockSpec((bt, c, 1024), lambda b, n: (b, n, qb)),
            pl.BlockSpec((bt, c, 256), lambda b, n: (b, n, kb)),
            pl.BlockSpec((bt, c, 256), lambda b, n: (b, n, vb)),
        ] + ctx_specs,
        out_specs=pl.BlockSpec((bt, c, 1024), lambda b, n: (b, n, 0)),
        compiler_params=_params(("arbitrary", "arbitrary")),
        name="attn_prompt" if prompt else "attn_sample",
    )(sinks, proj3, proj3, proj3, ctx_k, ctx_v)


def _ret_kernel(cdec_ref, q_ref, k_ref, v_ref, g_ref, intra_ref, cross_ref, kdec_ref, ln_ref, *rest, has_s0):
    if has_s0:
        s0_ref, o_ref, sout_ref, s_ref = rest
    else:
        o_ref, sout_ref, s_ref = rest
    bt = q_ref.shape[0]
    ci = pl.program_id(1)

    @pl.when(ci == 0)
    def _():
        s_ref[...] = s0_ref[...] if has_s0 else jnp.zeros(s_ref.shape, F32)

    def one(b):
        q, k, v, gt = q_ref[b], k_ref[b], v_ref[b], g_ref[b]
        outs = []
        for h in range(N_HEADS_R):
            qh = q[:, h * DK_R:(h + 1) * DK_R].astype(BF16)
            kh = k[:, h * DK_R:(h + 1) * DK_R]
            vh = v[:, h * DV_R:(h + 1) * DV_R].astype(BF16)
            s = s_ref[b, h]
            att = _dot_nt(qh, kh.astype(BF16)) * intra_ref[h]
            o = _dot(att.astype(BF16), vh) + _dot(qh, s.astype(BF16)) * cross_ref[h]
            s_ref[b, h] = s * cdec_ref[h] + _dot_tn((kh * kdec_ref[h]).astype(BF16), vh)
            mu = jnp.mean(o, axis=-1, keepdims=True)
            d = o - mu
            var = jnp.mean(d * d, axis=-1, keepdims=True)
            gh = gt[:, h * DV_R:(h + 1) * DV_R]
            outs.append(d * lax.rsqrt(var + EPS) * ln_ref[h:h + 1, :] * (gh * _sigmoid(gh)))
        o_ref[b] = jnp.concatenate(outs, axis=1).astype(o_ref.dtype)

    for b in range(bt):
        one(b)

    @pl.when(ci == pl.num_programs(1) - 1)
    def _():
        sout_ref[...] = s_ref[...]


def _retention(proj3, s0, ln_ret, bt, c):
    bn, t, _ = proj3.shape
    log_g = jnp.log1p(-jnp.exp2(-5.0 - jnp.arange(N_HEADS_R, dtype=F32)))
    i = jnp.arange(c, dtype=F32)
    diff = i[:, None] - i[None, :]
    intra = jnp.where(diff[None] >= 0, jnp.exp(jnp.maximum(diff, 0.0)[None] * log_g[:, None, None]), 0.0)
    cross = jnp.exp((i + 1.0)[None, :] * log_g[:, None])
    kdec = jnp.exp((c - 1.0 - i)[None, :] * log_g[:, None])
    cdec = jnp.exp(c * log_g)
    cross_b = jnp.broadcast_to(cross[:, :, None], (N_HEADS_R, c, DV_R))
    kdec_b = jnp.broadcast_to(kdec[:, :, None], (N_HEADS_R, c, DK_R))
    has_s0 = s0 is not None
    const3 = lambda b, n: (0, 0, 0)
    state_spec = pl.BlockSpec((bt, N_HEADS_R, DK_R, DV_R), lambda b, n: (b, 0, 0, 0))
    in_specs = [
        pl.BlockSpec(memory_space=pltpu.SMEM),
        pl.BlockSpec((bt, c, 512), lambda b, n: (b, n, _blk("q_r", 512))),
        pl.BlockSpec((bt, c, 512), lambda b, n: (b, n, _blk("k_r", 512))),
        pl.BlockSpec((bt, c, 1024), lambda b, n: (b, n, _blk("v_r", 1024))),
        pl.BlockSpec((bt, c, 1024), lambda b, n: (b, n, _blk("g_r", 1024))),
        pl.BlockSpec((N_HEADS_R, c, c), const3),
        pl.BlockSpec((N_HEADS_R, c, DV_R), const3),
        pl.BlockSpec((N_HEADS_R, c, DK_R), const3),
        pl.BlockSpec((N_HEADS_R, DV_R), lambda b, n: (0, 0)),
    ]
    args = [cdec, proj3, proj3, proj3, proj3, intra, cross_b, kdec_b, ln_ret]
    if has_s0:
        in_specs.append(state_spec)
        args.append(s0)
    return pl.pallas_call(
        functools.partial(_ret_kernel, has_s0=has_s0),
        out_shape=(jax.ShapeDtypeStruct((bn, t, N_HEADS_R * DV_R), BF16),
                   jax.ShapeDtypeStruct((bn, N_HEADS_R, DK_R, DV_R), F32)),
        grid=(bn // bt, t // c),
        in_specs=in_specs,
        out_specs=(pl.BlockSpec((bt, c, 1024), lambda b, n: (b, n, 0)), state_spec),
        scratch_shapes=[pltpu.VMEM((bt, N_HEADS_R, DK_R, DV_R), F32)],
        compiler_params=_params(("arbitrary", "arbitrary")),
        name="ret_sample" if has_s0 else "ret_prompt",
    )(*args)


def _merge_kernel(oa_ref, or_ref, ga_ref, gr_ref, wa_ref, wr_ref, m_ref):
    br_a = _dot(oa_ref[...], wa_ref[...])
    br_r = _dot(or_ref[...], wr_ref[...])
    m_ref[...] = (_sigmoid(ga_ref[...]) * br_a + _sigmoid(gr_ref[...]) * br_r).astype(m_ref.dtype)


def _merge(o_a, o_r, proj, w_a, w_r, tm, tn=512):
    n = o_a.shape[0]
    ga, gr = _blk("gate_a", tn), _blk("gate_r", tn)
    return pl.pallas_call(
        _merge_kernel,
        out_shape=jax.ShapeDtypeStruct((n, D_MODEL), BF16),
        grid=(n // tm, D_MODEL // tn),
        in_specs=[
            pl.BlockSpec((tm, 1024), lambda i, j: (i, 0)),
            pl.BlockSpec((tm, 1024), lambda i, j: (i, 0)),
            pl.BlockSpec((tm, tn), lambda i, j: (i, ga + j)),
            pl.BlockSpec((tm, tn), lambda i, j: (i, gr + j)),
            pl.BlockSpec((1024, tn), lambda i, j: (0, j)),
            pl.BlockSpec((1024, tn), lambda i, j: (0, j)),
        ],
        out_specs=pl.BlockSpec((tm, tn), lambda i, j: (i, j)),
        compiler_params=_params(("arbitrary", "arbitrary")),
        name="merge",
    )(o_a, o_r, proj, proj, w_a, w_r)


def _out_proj_kernel(x_ref, m_ref, w_ref, o_ref):
    o_ref[...] = x_ref[...] + _dot(m_ref[...], w_ref[...])


def _out_proj(x2d, m, w_o, tm, tn=512):
    n = x2d.shape[0]
    return pl.pallas_call(
        _out_proj_kernel,
        out_shape=jax.ShapeDtypeStruct((n, D_MODEL), F32),
        grid=(n // tm, D_MODEL // tn),
        in_specs=[
            pl.BlockSpec((tm, tn), lambda i, j: (i, j)),
            pl.BlockSpec((tm, D_MODEL), lambda i, j: (i, 0)),
            pl.BlockSpec((D_MODEL, tn), lambda i, j: (0, j)),
        ],
        out_specs=pl.BlockSpec((tm, tn), lambda i, j: (i, j)),
        compiler_params=_params(("arbitrary", "arbitrary")),
        name="out_proj",
    )(x2d, m, w_o)


def _scores_kernel(x_ref, g_ref, w_ref, sk_ref, h_out_ref, s_ref, h_ref):
    j = pl.program_id(1)

    @pl.when(j == 0)
    def _():
        h = _rms(x_ref[...], g_ref[...])
        h_ref[...] = h.astype(BF16)
        h_out_ref[...] = h

    qry = _dot(h_ref[...], w_ref[...]).astype(BF16)
    for g in range(qry.shape[1] // D_KEY_HALF):
        s_ref[g] = _dot_nt(sk_ref[g % 2], qry[:, g * D_KEY_HALF:(g + 1) * D_KEY_HALF])


def _scores(x2d, ln, w_q, sub_keys, tm, tn=512):
    n = x2d.shape[0]
    ng = tn // D_KEY_HALF
    return pl.pallas_call(
        _scores_kernel,
        out_shape=(jax.ShapeDtypeStruct((n, D_MODEL), F32),
                   jax.ShapeDtypeStruct((2 * PEER_HEADS, N_KEYS, n), F32)),
        grid=(n // tm, D_MODEL // tn),
        in_specs=[
            pl.BlockSpec((tm, D_MODEL), lambda i, j: (i, 0)),
            pl.BlockSpec((1, D_MODEL), lambda i, j: (0, 0)),
            pl.BlockSpec((D_MODEL, tn), lambda i, j: (0, j)),
            pl.BlockSpec((2, N_KEYS, D_KEY_HALF), lambda i, j: (0, 0, 0)),
        ],
        out_specs=(pl.BlockSpec((tm, D_MODEL), lambda i, j: (i, 0)),
                   pl.BlockSpec((ng, N_KEYS, tm), lambda i, j: (j, 0, i))),
        scratch_shapes=[pltpu.VMEM((tm, D_MODEL), BF16)],
        compiler_params=_params(("arbitrary", "arbitrary")),
        name="peer_scores",
    )(x2d, ln, w_q, sub_keys)


def _take_top(vals, iota, count, fill, payload=None):
    n_rows = vals.shape[0]
    tags = (iota,) if payload is None else (iota, payload)

    def first_max(v):
        row8 = lax.broadcasted_iota(jnp.int32, (SUBLANES, v.shape[1]), 0)
        blocks = [(v[i:i + SUBLANES], row8 + i) + tuple(t[i:i + SUBLANES] for t in tags[1:])
                  for i in range(0, n_rows, SUBLANES)]
        while len(blocks) > 1:
            nxt = []
            for a, b in zip(blocks[0::2], blocks[1::2]):
                take_a = a[0] >= b[0]
                nxt.append(tuple(jnp.where(take_a, x, y) for x, y in zip(a, b)))
            blocks = nxt + ([blocks[-1]] if len(blocks) % 2 else [])
        cur = blocks[0]
        m = jnp.max(cur[0], axis=0, keepdims=True)
        pos = jnp.min(jnp.where(cur[0] == m, cur[1], n_rows), axis=0, keepdims=True)
        rest = tuple(jnp.max(jnp.where(cur[1] == pos, t, -1), axis=0, keepdims=True) for t in cur[2:])
        return (m, pos) + rest

    top_v, top_i = [], []
    for _ in range(count):
        best = first_max(vals)
        top_v.append(best[0])
        top_i.append(best[-1])
        vals = jnp.where(iota == best[1], fill, vals)
    return jnp.concatenate(top_v, axis=0), jnp.concatenate(top_i, axis=0)


_COMB_PIECES = tuple((i, PEER_TOPK // (i + 1)) for i in range(SUBLANES))
_COMB_ROWS = sum(max(nj, SUBLANES) for _, nj in _COMB_PIECES) + SUBLANES


def _topk_kernel(s_ref, e_ref, g_ref):
    tt = s_ref.shape[2]
    key_iota = lax.broadcasted_iota(jnp.int32, (N_KEYS, tt), 0)
    comb_iota = lax.broadcasted_iota(jnp.int32, (_COMB_ROWS, tt), 0)
    ninf = float("-inf")

    def head(h, carry):
        s0, i0 = _take_top(s_ref[2 * h], key_iota, PEER_TOPK, ninf)
        s1, i1 = _take_top(s_ref[2 * h + 1], key_iota, PEER_TOPK, ninf)
        comb, cidx = [], []
        for i, nj in _COMB_PIECES:
            rows = max(nj, SUBLANES)
            sv = s0[i:i + 1, :] + s1[:rows]
            iv = i0[i:i + 1, :] * N_KEYS + i1[:rows]
            if nj < rows:
                keep = lax.broadcasted_iota(jnp.int32, (rows, tt), 0) < nj
                sv, iv = jnp.where(keep, sv, ninf), jnp.where(keep, iv, -1)
            comb.append(sv)
            cidx.append(iv)
        comb.append(s0[SUBLANES:] + s1[0:1, :])
        cidx.append(i0[SUBLANES:] * N_KEYS + i1[0:1, :])
        comb, cidx = jnp.concatenate(comb, axis=0), jnp.concatenate(cidx, axis=0)
        best, eidx = _take_top(comb, comb_iota, PEER_TOPK, ninf, payload=cidx)
        ex = jnp.exp(best - jnp.max(best, axis=0, keepdims=True))
        g_ref[h] = ex / jnp.sum(ex, axis=0, keepdims=True)
        e_ref[h] = eidx
        return carry

    lax.fori_loop(0, PEER_HEADS, head, 0)


def _topk(scores_t, tt=256):
    n = scores_t.shape[2]
    out_spec = pl.BlockSpec((PEER_HEADS, PEER_TOPK, tt), lambda i: (0, 0, i))
    return pl.pallas_call(
        _topk_kernel,
        out_shape=(jax.ShapeDtypeStruct((PEER_HEADS, PEER_TOPK, n), jnp.int32),
                   jax.ShapeDtypeStruct((PEER_HEADS, PEER_TOPK, n), F32)),
        grid=(n // tt,),
        in_specs=[pl.BlockSpec((2 * PEER_HEADS, N_KEYS, tt), lambda i: (0, 0, i))],
        out_specs=(out_spec, out_spec),
        compiler_params=_params(("arbitrary",)),
        name="peer_topk",
    )(scores_t)


def _gelu(a):
    return 0.5 * a * (1.0 + lax.erf(a * (2.0 ** -0.5)))


def _peer_kernel(idx_hbm, h_ref, g2_ref, x_ref, uv_hbm, o_ref, idx_smem, buf, sem_i, sem):
    tt = h_ref.shape[0]
    half = D_MODEL // 2
    step = pl.program_id(0)
    more = step + 1 < pl.num_programs(0)
    cur = step % 2

    def idx_copy(s, islot):
        return pltpu.make_async_copy(idx_hbm.at[pl.ds(s * tt, tt), :], idx_smem.at[islot], sem_i.at[islot])

    n_groups = tt // SUBLANES
    n_parts = 2 * PEER_CHUNKS
    cw = half // PEER_CHUNKS

    def gather_start(islot, t, bank, j, part=None):
        ks = range(N_SEL) if part is None else range(part * N_SEL // n_parts, (part + 1) * N_SEL // n_parts)
        for k in ks:
            e = idx_smem[islot, t, k]
            pltpu.make_async_copy(uv_hbm.at[e], buf.at[bank, j, pl.ds(k, 1), :],
                                  sem.at[bank]).start(priority=k % 2)

    def gather_wait(bank):
        pltpu.make_async_copy(buf.at[bank], buf.at[bank], sem.at[bank]).wait()

    @pl.when(step == 0)
    def _():
        first = idx_copy(0, 0)
        first.start()
        first.wait()
        for j in range(SUBLANES):
            gather_start(0, j, 0, j)

    @pl.when(more)
    def _():
        idx_copy(step + 1, 1 - cur).start()

    row_iota = lax.broadcasted_iota(jnp.int32, (SUBLANES, 1), 0)
    even = lax.broadcasted_iota(jnp.int32, (SUBLANES, 2 * N_SEL), 1) % 2 == 0

    def group(gi, carry):
        r0 = pl.multiple_of(gi * SUBLANES, SUBLANES)
        last = gi == n_groups - 1
        bank = gi % 2

        @pl.when(jnp.logical_and(last, more))
        def _():
            idx_copy(step + 1, 1 - cur).wait()

        n_islot = jnp.where(jnp.logical_and(last, more), 1 - cur, cur)
        n_t0 = jnp.where(last, 0, r0 + SUBLANES)
        gather_wait(bank)
        h8 = h_ref[pl.ds(r0, SUBLANES), :]
        g2 = g2_ref[pl.ds(r0, SUBLANES), :]
        acc = [jnp.zeros((2 * SUBLANES, cw), F32) for _ in range(PEER_CHUNKS)]
        issued = [0]

        def request():
            q = issued[0]
            issued[0] += 1
            gather_start(n_islot, n_t0 + q // n_parts, 1 - bank, q // n_parts, part=q % n_parts)

        def u_dot(r, c, p):
            sel = row_iota == r
            hm = jnp.concatenate([jnp.where(sel, h8[:, c * cw:(c + 1) * cw], 0.0),
                                  jnp.where(sel, h8[:, half + c * cw:half + (c + 1) * cw], 0.0)],
                                 axis=0).astype(BF16)
            ub = pltpu.bitcast(buf[bank, r, :, c * cw:(c + 1) * cw], BF16)
            return p + _dot_nt(hm, ub)

        def mix_weights(p):
            part = jnp.where(even, p[:SUBLANES], p[SUBLANES:])
            a2 = part + jnp.where(even, pltpu.roll(part, 2 * N_SEL - 1, 1), pltpu.roll(part, 1, 1))
            w2 = _gelu(a2) * g2
            return jnp.concatenate([jnp.where(even, w2, 0.0), jnp.where(even, 0.0, w2)], axis=0).astype(BF16)

        def v_dot(r, c, wm):
            vb = pltpu.bitcast(buf[bank, r, :, half + c * cw:half + (c + 1) * cw], BF16)
            acc[c] = acc[c] + _dot(wm, vb)

        p_next = jnp.zeros((2 * SUBLANES, 2 * N_SEL), F32)
        for c in range(PEER_CHUNKS):
            request()
            p_next = u_dot(0, c, p_next)
        for r in range(SUBLANES):
            wm = mix_weights(p_next)
            p_next = jnp.zeros((2 * SUBLANES, 2 * N_SEL), F32)
            for c in range(PEER_CHUNKS):
                if r + 1 < SUBLANES:
                    request()
                    p_next = u_dot(r + 1, c, p_next)
                request()
                v_dot(r, c, wm)
        assert issued[0] == SUBLANES * n_parts
        peer = jnp.concatenate([a[:SUBLANES] for a in acc] + [a[SUBLANES:] for a in acc], axis=1)
        o_ref[pl.ds(r0, SUBLANES), :] = x_ref[pl.ds(r0, SUBLANES), :] + peer
        return carry

    lax.fori_loop(0, n_groups, group, 0)

    @pl.when(jnp.logical_not(more))
    def _():
        gather_wait(n_groups % 2)


def _pack_kernel(u_ref, v_ref, o_ref):
    half = D_MODEL // 2

    def words(t):
        bits = lax.bitcast_convert_type(t.astype(BF16).astype(F32), jnp.uint32)
        return (bits[:, :half] >> 16) | bits[:, half:]

    o_ref[:, :half] = words(u_ref[...])
    o_ref[:, half:] = words(v_ref[...])


def _pack_tables(tab_u, tab_v, tm=512):
    e = tab_u.shape[0]
    row = lambda i: (i, 0)
    return pl.pallas_call(
        _pack_kernel,
        out_shape=jax.ShapeDtypeStruct((e, D_MODEL), jnp.uint32),
        grid=(e // tm,),
        in_specs=[pl.BlockSpec((tm, D_MODEL), row), pl.BlockSpec((tm, D_MODEL), row)],
        out_specs=pl.BlockSpec((tm, D_MODEL), row),
        compiler_params=_params(("arbitrary",)),
        name="peer_pack",
    )(tab_u, tab_v)


def _peer(eidx, h2, gate2, x2d, uv, tt):
    n = x2d.shape[0]
    assert tt % (2 * SUBLANES) == 0
    row = lambda i: (i, 0)
    return pl.pallas_call(
        _peer_kernel,
        out_shape=jax.ShapeDtypeStruct((n, D_MODEL), F32),
        grid=(n // tt,),
        in_specs=[
            pl.BlockSpec(memory_space=pl.ANY),
            pl.BlockSpec((tt, D_MODEL), row),
            pl.BlockSpec((tt, 2 * N_SEL), row),
            pl.BlockSpec((tt, D_MODEL), row),
            pl.BlockSpec(memory_space=pl.ANY),
        ],
        out_specs=pl.BlockSpec((tt, D_MODEL), row),
        scratch_shapes=[
            pltpu.SMEM((2, tt, N_SEL), jnp.int32),
            pltpu.VMEM((2, SUBLANES, N_SEL, D_MODEL), jnp.uint32),
            pltpu.SemaphoreType.DMA((2,)),
            pltpu.SemaphoreType.DMA((2,)),
        ],
        compiler_params=_params(("arbitrary",)),
        name="peer_mix",
    )(eidx, h2, gate2, x2d, uv)


def _ple_kernel(x_ref, p_ref, lnp_ref, wg_ref, wp_ref, lnf_ref, y_ref):
    x = x_ref[...]
    gate = _sigmoid(_dot(_rms(x, lnp_ref[...]).astype(BF16), wg_ref[...]))
    x = x + gate * _dot(p_ref[...].astype(BF16), wp_ref[...])
    y_ref[...] = _rms(x, lnf_ref[...])


def _ple(x2d, p2d, ln_ple, w_gate, w_proj, ln_final, tm):
    n = x2d.shape[0]
    pd = p2d.shape[1]
    row = lambda i: (i, 0)
    const = lambda i: (0, 0)
    return pl.pallas_call(
        _ple_kernel,
        out_shape=jax.ShapeDtypeStruct((n, D_MODEL), F32),
        grid=(n // tm,),
        in_specs=[
            pl.BlockSpec((tm, D_MODEL), row),
            pl.BlockSpec((tm, pd), row),
            pl.BlockSpec((1, D_MODEL), const),
            pl.BlockSpec((D_MODEL, D_MODEL), const),
            pl.BlockSpec((pd, D_MODEL), const),
            pl.BlockSpec((1, D_MODEL), const),
        ],
        out_specs=pl.BlockSpec((tm, D_MODEL), row),
        compiler_params=_params(("arbitrary",)),
        name="ple_final",
    )(x2d, p2d, ln_ple, w_gate, w_proj, ln_final)


def _rope_tables(pos):
    inv = ROPE_THETA ** (-jnp.arange(0, HEAD_DIM_A, 2, dtype=F32) / HEAD_DIM_A)
    ang = pos.astype(F32)[:, None] * inv[None, :]
    cos, sin = jnp.cos(ang), jnp.sin(ang)
    reps = PROJ_TILE // HEAD_DIM_A
    cos = jnp.tile(jnp.concatenate([cos, cos], axis=1), (1, reps))
    sin = jnp.tile(jnp.concatenate([-sin, sin], axis=1), (1, reps))
    k_end = (COL["k_a"][0] + COL["k_a"][1]) % PROJ_TILE
    if k_end:
        keep = jnp.arange(PROJ_TILE) < k_end
        cos = jnp.concatenate([cos, jnp.where(keep, cos, 1.0)], axis=1)
        sin = jnp.concatenate([sin, jnp.where(keep, sin, 0.0)], axis=1)
    return cos, sin


def _pick(n, prefs):
    for c in prefs:
        if n % c == 0:
            return c
    raise ValueError(f"no tile of {prefs} divides {n}")


def _group(x, pe, pos0, cache_k, cache_v, s0, wts):
    bn, t, _ = x.shape
    n = bn * t
    prompt = cache_k is None
    x2d = x.reshape(n, D_MODEL)
    tm = _pick(n, (1024, 512, 256, 128))

    pos = pos0 + jnp.arange(t, dtype=jnp.int32)
    cos, sin = _rope_tables(pos)
    if t < tm:
        cos, sin = jnp.tile(cos, (tm // t, 1)), jnp.tile(sin, (tm // t, 1))
    proj = _in_proj(x2d, wts["ln1"], wts["w_in"], cos, sin, wts["colscale"], tm)
    proj3 = proj.reshape(bn, t, IN_COLS)

    if prompt:
        c_att, bt_att = ATTN_BLOCK, 1
        ctx_k = ctx_v = proj3
    else:
        c_att, bt_att = t, _pick(bn, (8, 4, 2, 1))
        ctx_k = cache_k.reshape(bn, WINDOW, N_KV_A * HEAD_DIM_A)
        ctx_v = cache_v.reshape(bn, WINDOW, N_KV_A * HEAD_DIM_A)
    o_a = _attention(wts["sinks"], proj3, ctx_k, ctx_v, bt_att, c_att, prompt)

    c_ret = math.gcd(t, RET_CHUNK)
    bt_ret = 1 if prompt else _pick(bn, (8, 4, 2, 1))
    o_r, s_new = _retention(proj3, s0, wts["ln_ret"], bt_ret, c_ret)

    m = _merge(o_a.reshape(n, -1), o_r.reshape(n, -1), proj, wts["w_br_a"], wts["w_br_r"], tm)
    x2 = _out_proj(x2d, m, wts["w_o"], tm)

    h2, scores_t = _scores(x2, wts["ln2"], wts["w_q"], wts["sub_keys"], _pick(n, (1024, 512, 256, 128)))
    eidx_t, gate_t = _topk(scores_t, _pick(n, (512, 256, 128)))
    eidx = eidx_t.reshape(N_SEL, n).T
    gate2 = jnp.repeat(gate_t.reshape(N_SEL, n).T, 2, axis=1)
    x3 = _peer(eidx, h2, gate2, x2, wts["peer_uv"], _pick(n, (128, 64, 32, 16)))

    y = _ple(x3, pe.reshape(n, -1), wts["ln_ple"], wts["w_ple_gate"], wts["w_ple_proj"], wts["ln_final"],
             _pick(n, (512, 256, 128)))

    k_off, v_off = COL["k_a"][0], COL["v_a"][0]
    kv_w = N_KV_A * HEAD_DIM_A
    k_new, v_new = proj3[:, :, k_off:k_off + kv_w], proj3[:, :, v_off:v_off + kv_w]
    if prompt:
        k_win, v_win = k_new[:, -WINDOW:], v_new[:, -WINDOW:]
    else:
        k_win = jnp.concatenate([ctx_k, k_new], axis=1)[:, -WINDOW:]
        v_win = jnp.concatenate([ctx_v, v_new], axis=1)[:, -WINDOW:]
    shp = (bn, WINDOW, N_KV_A, HEAD_DIM_A)
    return y.reshape(bn, t, D_MODEL), k_win.reshape(shp), v_win.reshape(shp), s_new


def kernel(x_prompt, x_sample, cache_k_win, cache_v_win, state_ret, p_prompt, p_sample, ln1, w_in, attn_sinks, ln_ret, w_branch_attn, w_branch_ret, w_out, ln2, w_peer_query, peer_sub_keys, peer_u, peer_v, ln_ple, w_ple_gate, w_ple_proj, ln_final):
    depth = ln1.shape[0]
    assert depth == 1, "single-layer step"
    i = 0
    colscale = jnp.ones((IN_COLS,), F32).at[COL["k_r"][0]:COL["k_r"][0] + COL["k_r"][1]].set(DK_R ** -0.5)
    wts = dict(
        ln1=ln1[i][None, :],
        w_in=jnp.concatenate([w_in[i][:, _REF_COL[f][0]:_REF_COL[f][0] + _REF_COL[f][1]] for f in _NEW_ORDER],
                             axis=1).astype(BF16),
        colscale=colscale[None, :],
        sinks=attn_sinks[i],
        ln_ret=ln_ret[i].reshape(N_HEADS_R, DV_R),
        w_br_a=w_branch_attn[i].astype(BF16),
        w_br_r=w_branch_ret[i].astype(BF16),
        w_o=w_out[i].astype(BF16),
        ln2=ln2[i][None, :],
        w_q=w_peer_query[i].astype(BF16),
        sub_keys=peer_sub_keys[i].astype(BF16),
        peer_uv=_pack_tables(peer_u[i], peer_v[i])[:, None, :],
        ln_ple=ln_ple[i][None, :],
        w_ple_gate=w_ple_gate[i].astype(BF16),
        w_ple_proj=w_ple_proj[i].astype(BF16),
        ln_final=ln_final[None, :],
    )
    yp, kp, vp, sp = _group(x_prompt, p_prompt[i], 0, None, None, None, wts)
    ys, ks, vs, ss = _group(x_sample, p_sample[i], PAST_LEN, cache_k_win[i], cache_v_win[i], state_ret[i], wts)
    return (yp, ys, kp[None], vp[None], sp[None], ks[None], vs[None], ss[None])
```

```python
import functools
import math

import jax
import jax.numpy as jnp
from jax import lax
from jax.experimental import pallas as pl
from jax.experimental.pallas import tpu as pltpu

F32 = jnp.float32
BF16 = jnp.bfloat16

D_MODEL = 2048
PAST_LEN = 16384
HEAD_DIM_A = 64
N_HEADS_A = 16
N_KV_A = 4
GROUP_A = 4
WINDOW = 128
ATTN_BLOCK = 128
N_HEADS_R = 8
DK_R = 64
DV_R = 128
RET_CHUNK = 128
PEER_HEADS = 8
N_KEYS = 128
PEER_TOPK = 16
D_KEY_HALF = 128
N_SEL = PEER_HEADS * PEER_TOPK
PEER_CHUNKS = 4
ROPE_THETA = 10000.0
EPS = 1e-6
NEG_INF = -1e30

LANES = 128
SUBLANES = 8
VMEM_LIMIT_BYTES = 56 * 1024 * 1024

_REF_FIELDS = (("q_a", 1024), ("k_a", 256), ("v_a", 256), ("q_r", 512), ("k_r", 512),
               ("v_r", 1024), ("g_r", 1024), ("gate_a", 2048), ("gate_r", 2048))
_NEW_ORDER = ("gate_a", "gate_r", "q_a", "v_r", "g_r", "q_r", "k_r", "k_a", "v_a")
_ROPE_FIELDS = ("q_a", "q_r", "k_r", "k_a")
PROJ_TILE = 512


def _layout():
    ref_off, o = {}, 0
    for name, w in _REF_FIELDS:
        ref_off[name] = (o, w)
        o += w
    new_off, o = {}, 0
    for name in _NEW_ORDER:
        new_off[name] = (o, ref_off[name][1])
        o += ref_off[name][1]
    return new_off, ref_off, o


COL, _REF_COL, IN_COLS = _layout()


def _blk(name, width):
    off, w = COL[name]
    assert off % width == 0 and w % width == 0
    return off // width


def _params(sem, vmem=VMEM_LIMIT_BYTES):
    return pltpu.CompilerParams(dimension_semantics=sem, vmem_limit_bytes=vmem)


def _rms(x, g):
    return x * lax.rsqrt(jnp.mean(x * x, axis=-1, keepdims=True) + EPS) * g


def _sigmoid(x):
    return 1.0 / (1.0 + jnp.exp(-x))


def _dot(a, b):
    return jnp.dot(a, b, preferred_element_type=F32)


def _dot_nt(a, b):
    return lax.dot_general(a, b, (((1,), (1,)), ((), ())), preferred_element_type=F32)


def _dot_tn(a, b):
    return lax.dot_general(a, b, (((0,), (0,)), ((), ())), preferred_element_type=F32)


def _in_proj_kernel(x_ref, g_ref, w_ref, cos_ref, sin_ref, cs_ref, o_ref, h_ref, *, rope_ranges):
    j = pl.program_id(1)

    @pl.when(j == 0)
    def _():
        h_ref[...] = _rms(x_ref[...], g_ref[...]).astype(BF16)

    is_rope = functools.reduce(jnp.logical_or, [(j >= a) & (j < b) for a, b in rope_ranges])

    @pl.when(is_rope)
    def _():
        acc = _dot(h_ref[...], w_ref[...])
        tn = acc.shape[1]
        lane = lax.broadcasted_iota(jnp.int32, acc.shape, 1)
        first_half = (lane % HEAD_DIM_A) < (HEAD_DIM_A // 2)
        partner = jnp.where(first_half, pltpu.roll(acc, tn - HEAD_DIM_A // 2, 1), pltpu.roll(acc, HEAD_DIM_A // 2, 1))
        o_ref[...] = (acc * cos_ref[...] + partner * sin_ref[...]) * cs_ref[...]

    @pl.when(jnp.logical_not(is_rope))
    def _():
        o_ref[...] = _dot(h_ref[...], w_ref[...])


def _in_proj(x2d, ln, w, cos, sin, colscale, tm):
    n = x2d.shape[0]
    rt = cos.shape[0] // tm
    rope_ranges = tuple((COL[f][0] // PROJ_TILE, -(-(COL[f][0] + COL[f][1]) // PROJ_TILE)) for f in _ROPE_FIELDS)
    ends = [COL[f][0] + COL[f][1] for f in _ROPE_FIELDS if (COL[f][0] + COL[f][1]) % PROJ_TILE]
    assert len(ends) <= 1 and cos.shape[1] == (1 + len(ends)) * PROJ_TILE
    mixed = ends[0] // PROJ_TILE if ends else -1
    table_spec = pl.BlockSpec((tm, PROJ_TILE), lambda i, j: (i % rt, jnp.where(j == mixed, 1, 0)))
    return pl.pallas_call(
        functools.partial(_in_proj_kernel, rope_ranges=rope_ranges),
        out_shape=jax.ShapeDtypeStruct((n, IN_COLS), F32),
        grid=(n // tm, IN_COLS // PROJ_TILE),
        in_specs=[
            pl.BlockSpec((tm, D_MODEL), lambda i, j: (i, 0)),
            pl.BlockSpec((1, D_MODEL), lambda i, j: (0, 0)),
            pl.BlockSpec((D_MODEL, PROJ_TILE), lambda i, j: (0, j)),
            table_spec,
            table_spec,
            pl.BlockSpec((1, PROJ_TILE), lambda i, j: (0, j)),
        ],
        out_specs=pl.BlockSpec((tm, PROJ_TILE), lambda i, j: (i, j)),
        scratch_shapes=[pltpu.VMEM((tm, D_MODEL), BF16)],
        compiler_params=_params(("arbitrary", "arbitrary")),
        name="in_proj",
    )(x2d, ln, w, cos, sin, colscale)


def _attn_kernel(sink_ref, q_ref, kc_ref, vc_ref, kx_ref, vx_ref, o_ref, *, first_ctx_invalid):
    bt, c, _ = q_ref.shape
    rows = GROUP_A * c
    qi = lax.broadcasted_iota(jnp.int32, (rows, WINDOW), 0) % c
    kj = lax.broadcasted_iota(jnp.int32, (rows, WINDOW), 1)
    ctx_ok = kj >= qi
    if first_ctx_invalid:
        ctx_ok = jnp.logical_and(ctx_ok, pl.program_id(1) > 0)
    qi_c = lax.broadcasted_iota(jnp.int32, (rows, c), 0) % c
    kj_c = lax.broadcasted_iota(jnp.int32, (rows, c), 1)
    cur_ok = kj_c <= qi_c
    grp = lax.broadcasted_iota(jnp.int32, (rows, 1), 0) // c
    scale = HEAD_DIM_A ** -0.5

    def one(b):
        q = q_ref[b]
        kc, vc, kx, vx = kc_ref[b], vc_ref[b], kx_ref[b], vx_ref[b]
        outs = [None] * N_HEADS_A
        for kh in range(N_KV_A):
            sl = slice(kh * HEAD_DIM_A, (kh + 1) * HEAD_DIM_A)
            q4 = jnp.concatenate(
                [q[:, (kh * GROUP_A + g) * HEAD_DIM_A:(kh * GROUP_A + g + 1) * HEAD_DIM_A] for g in range(GROUP_A)],
                axis=0).astype(BF16)
            s_x = jnp.where(ctx_ok, _dot_nt(q4, kx[:, sl].astype(BF16)) * scale, NEG_INF)
            s_c = jnp.where(cur_ok, _dot_nt(q4, kc[:, sl].astype(BF16)) * scale, NEG_INF)
            sink = jnp.zeros((rows, 1), F32)
            for g in range(GROUP_A):
                sink = jnp.where(grp == g, sink_ref[kh * GROUP_A + g], sink)
            m = jnp.maximum(jnp.maximum(jnp.max(s_x, axis=-1, keepdims=True), jnp.max(s_c, axis=-1, keepdims=True)), sink)
            e_x = jnp.exp(s_x - m)
            e_c = jnp.exp(s_c - m)
            den = jnp.sum(e_x, axis=-1, keepdims=True) + jnp.sum(e_c, axis=-1, keepdims=True) + jnp.exp(sink - m)
            o4 = _dot((e_x / den).astype(BF16), vx[:, sl].astype(BF16)) + _dot((e_c / den).astype(BF16), vc[:, sl].astype(BF16))
            for g in range(GROUP_A):
                outs[kh * GROUP_A + g] = o4[g * c:(g + 1) * c]
        o_ref[b] = jnp.concatenate(outs, axis=1).astype(o_ref.dtype)

    for b in range(bt):
        one(b)


def _attention(sinks, proj3, ctx_k, ctx_v, bt, c, prompt):
    bn, t, _ = proj3.shape
    qb, kb, vb = _blk("q_a", 1024), _blk("k_a", 256), _blk("v_a", 256)
    if prompt:
        ctx_specs = [pl.BlockSpec((bt, WINDOW, 256), lambda b, n: (b, jnp.maximum(n - 1, 0), kb)),
                     pl.BlockSpec((bt, WINDOW, 256), lambda b, n: (b, jnp.maximum(n - 1, 0), vb))]
    else:
        ctx_specs = [pl.BlockSpec((bt, WINDOW, 256), lambda b, n: (b, 0, 0)),
                     pl.BlockSpec((bt, WINDOW, 256), lambda b, n: (b, 0, 0))]
    return pl.pallas_call(
        functools.partial(_attn_kernel, first_ctx_invalid=prompt),
        out_shape=jax.ShapeDtypeStruct((bn, t, N_HEADS_A * HEAD_DIM_A), BF16),
        grid=(bn // bt, t // c),
        in_specs=[
            pl.BlockSpec(memory_space=pltpu.SMEM),
            pl.BlockSpec((bt, c, 1024), lambda b, n: (b, n, qb)),
            pl.BlockSpec((bt, c, 256), lambda b, n: (b, n, kb)),
            pl.BlockSpec((bt, c, 256), lambda b, n: (b, n, vb)),
        ] + ctx_specs,
        out_specs=pl.BlockSpec((bt, c, 1024), lambda b, n: (b, n, 0)),
        compiler_params=_params(("arbitrary", "arbitrary")),
        name="attn_prompt" if prompt else "attn_sample",
    )(sinks, proj3, proj3, proj3, ctx_k, ctx_v)


def _ret_kernel(cdec_ref, q_ref, k_ref, v_ref, g_ref, intra_ref, cross_ref, kdec_ref, ln_ref, *rest, has_s0):
    if has_s0:
        s0_ref, o_ref, sout_ref, s_ref = rest
    else:
        o_ref, sout_ref, s_ref = rest
    bt = q_ref.shape[0]
    ci = pl.program_id(1)

    @pl.when(ci == 0)
    def _():
        s_ref[...] = s0_ref[...] if has_s0 else jnp.zeros(s_ref.shape, F32)

    def one(b):
        q, k, v, gt = q_ref[b], k_ref[b], v_ref[b], g_ref[b]
        outs = []
        for h in range(N_HEADS_R):
            qh = q[:, h * DK_R:(h + 1) * DK_R].astype(BF16)
            kh = k[:, h * DK_R:(h + 1) * DK_R]
            vh = v[:, h * DV_R:(h + 1) * DV_R].astype(BF16)
            s = s_ref[b, h]
            att = _dot_nt(qh, kh.astype(BF16)) * intra_ref[h]
            o = _dot(att.astype(BF16), vh) + _dot(qh, s.astype(BF16)) * cross_ref[h]
            s_ref[b, h] = s * cdec_ref[h] + _dot_tn((kh * kdec_ref[h]).astype(BF16), vh)
            mu = jnp.mean(o, axis=-1, keepdims=True)
            d = o - mu
            var = jnp.mean(d * d, axis=-1, keepdims=True)
            gh = gt[:, h * DV_R:(h + 1) * DV_R]
            outs.append(d * lax.rsqrt(var + EPS) * ln_ref[h:h + 1, :] * (gh * _sigmoid(gh)))
        o_ref[b] = jnp.concatenate(outs, axis=1).astype(o_ref.dtype)

    for b in range(bt):
        one(b)

    @pl.when(ci == pl.num_programs(1) - 1)
    def _():
        sout_ref[...] = s_ref[...]


def _retention(proj3, s0, ln_ret, bt, c):
    bn, t, _ = proj3.shape
    log_g = jnp.log1p(-jnp.exp2(-5.0 - jnp.arange(N_HEADS_R, dtype=F32)))
    i = jnp.arange(c, dtype=F32)
    diff = i[:, None] - i[None, :]
    intra = jnp.where(diff[None] >= 0, jnp.exp(jnp.maximum(diff, 0.0)[None] * log_g[:, None, None]), 0.0)
    cross = jnp.exp((i + 1.0)[None, :] * log_g[:, None])
    kdec = jnp.exp((c - 1.0 - i)[None, :] * log_g[:, None])
    cdec = jnp.exp(c * log_g)
    cross_b = jnp.broadcast_to(cross[:, :, None], (N_HEADS_R, c, DV_R))
    kdec_b = jnp.broadcast_to(kdec[:, :, None], (N_HEADS_R, c, DK_R))
    has_s0 = s0 is not None
    const3 = lambda b, n: (0, 0, 0)
    state_spec = pl.BlockSpec((bt, N_HEADS_R, DK_R, DV_R), lambda b, n: (b, 0, 0, 0))
    in_specs = [
        pl.BlockSpec(memory_space=pltpu.SMEM),
        pl.BlockSpec((bt, c, 512), lambda b, n: (b, n, _blk("q_r", 512))),
        pl.BlockSpec((bt, c, 512), lambda b, n: (b, n, _blk("k_r", 512))),
        pl.BlockSpec((bt, c, 1024), lambda b, n: (b, n, _blk("v_r", 1024))),
        pl.BlockSpec((bt, c, 1024), lambda b, n: (b, n, _blk("g_r", 1024))),
        pl.BlockSpec((N_HEADS_R, c, c), const3),
        pl.BlockSpec((N_HEADS_R, c, DV_R), const3),
        pl.BlockSpec((N_HEADS_R, c, DK_R), const3),
        pl.BlockSpec((N_HEADS_R, DV_R), lambda b, n: (0, 0)),
    ]
    args = [cdec, proj3, proj3, proj3, proj3, intra, cross_b, kdec_b, ln_ret]
    if has_s0:
        in_specs.append(state_spec)
        args.append(s0)
    return pl.pallas_call(
        functools.partial(_ret_kernel, has_s0=has_s0),
        out_shape=(jax.ShapeDtypeStruct((bn, t, N_HEADS_R * DV_R), BF16),
                   jax.ShapeDtypeStruct((bn, N_HEADS_R, DK_R, DV_R), F32)),
        grid=(bn // bt, t // c),
        in_specs=in_specs,
        out_specs=(pl.BlockSpec((bt, c, 1024), lambda b, n: (b, n, 0)), state_spec),
        scratch_shapes=[pltpu.VMEM((bt, N_HEADS_R, DK_R, DV_R), F32)],
        compiler_params=_params(("arbitrary", "arbitrary")),
        name="ret_sample" if has_s0 else "ret_prompt",
    )(*args)


def _merge_kernel(oa_ref, or_ref, ga_ref, gr_ref, wa_ref, wr_ref, m_ref):
    br_a = _dot(oa_ref[...], wa_ref[...])
    br_r = _dot(or_ref[...], wr_ref[...])
    m_ref[...] = (_sigmoid(ga_ref[...]) * br_a + _sigmoid(gr_ref[...]) * br_r).astype(m_ref.dtype)


def _merge(o_a, o_r, proj, w_a, w_r, tm, tn=512):
    n = o_a.shape[0]
    ga, gr = _blk("gate_a", tn), _blk("gate_r", tn)
    return pl.pallas_call(
        _merge_kernel,
        out_shape=jax.ShapeDtypeStruct((n, D_MODEL), BF16),
        grid=(n // tm, D_MODEL // tn),
        in_specs=[
            pl.BlockSpec((tm, 1024), lambda i, j: (i, 0)),
            pl.BlockSpec((tm, 1024), lambda i, j: (i, 0)),
            pl.BlockSpec((tm, tn), lambda i, j: (i, ga + j)),
            pl.BlockSpec((tm, tn), lambda i, j: (i, gr + j)),
            pl.BlockSpec((1024, tn), lambda i, j: (0, j)),
            pl.BlockSpec((1024, tn), lambda i, j: (0, j)),
        ],
        out_specs=pl.BlockSpec((tm, tn), lambda i, j: (i, j)),
        compiler_params=_params(("arbitrary", "arbitrary")),
        name="merge",
    )(o_a, o_r, proj, proj, w_a, w_r)


def _out_proj_kernel(x_ref, m_ref, w_ref, o_ref):
    o_ref[...] = x_ref[...] + _dot(m_ref[...], w_ref[...])


def _out_proj(x2d, m, w_o, tm, tn=512):
    n = x2d.shape[0]
    return pl.pallas_call(
        _out_proj_kernel,
        out_shape=jax.ShapeDtypeStruct((n, D_MODEL), F32),
        grid=(n // tm, D_MODEL // tn),
        in_specs=[
            pl.BlockSpec((tm, tn), lambda i, j: (i, j)),
            pl.BlockSpec((tm, D_MODEL), lambda i, j: (i, 0)),
            pl.BlockSpec((D_MODEL, tn), lambda i, j: (0, j)),
        ],
        out_specs=pl.BlockSpec((tm, tn), lambda i, j: (i, j)),
        compiler_params=_params(("arbitrary", "arbitrary")),
        name="out_proj",
    )(x2d, m, w_o)


def _scores_kernel(x_ref, g_ref, w_ref, sk_ref, h_out_ref, s_ref, h_ref):
    j = pl.program_id(1)

    @pl.when(j == 0)
    def _():
        h = _rms(x_ref[...], g_ref[...])
        h_ref[...] = h.astype(BF16)
        h_out_ref[...] = h

    qry = _dot(h_ref[...], w_ref[...]).astype(BF16)
    for g in range(qry.shape[1] // D_KEY_HALF):
        s_ref[g] = _dot_nt(sk_ref[g % 2], qry[:, g * D_KEY_HALF:(g + 1) * D_KEY_HALF])


def _scores(x2d, ln, w_q, sub_keys, tm, tn=512):
    n = x2d.shape[0]
    ng = tn // D_KEY_HALF
    return pl.pallas_call(
        _scores_kernel,
        out_shape=(jax.ShapeDtypeStruct((n, D_MODEL), F32),
                   jax.ShapeDtypeStruct((2 * PEER_HEADS, N_KEYS, n), F32)),
        grid=(n // tm, D_MODEL // tn),
        in_specs=[
            pl.BlockSpec((tm, D_MODEL), lambda i, j: (i, 0)),
            pl.BlockSpec((1, D_MODEL), lambda i, j: (0, 0)),
            pl.BlockSpec((D_MODEL, tn), lambda i, j: (0, j)),
            pl.BlockSpec((2, N_KEYS, D_KEY_HALF), lambda i, j: (0, 0, 0)),
        ],
        out_specs=(pl.BlockSpec((tm, D_MODEL), lambda i, j: (i, 0)),
                   pl.BlockSpec((ng, N_KEYS, tm), lambda i, j: (j, 0, i))),
        scratch_shapes=[pltpu.VMEM((tm, D_MODEL), BF16)],
        compiler_params=_params(("arbitrary", "arbitrary")),
        name="peer_scores",
    )(x2d, ln, w_q, sub_keys)


def _take_top(vals, iota, count, fill, payload=None):
    n_rows = vals.shape[0]
    tags = (iota,) if payload is None else (iota, payload)

    def first_max(v):
        row8 = lax.broadcasted_iota(jnp.int32, (SUBLANES, v.shape[1]), 0)
        blocks = [(v[i:i + SUBLANES], row8 + i) + tuple(t[i:i + SUBLANES] for t in tags[1:])
                  for i in range(0, n_rows, SUBLANES)]
        while len(blocks) > 1:
            nxt = []
            for a, b in zip(blocks[0::2], blocks[1::2]):
                take_a = a[0] >= b[0]
                nxt.append(tuple(jnp.where(take_a, x, y) for x, y in zip(a, b)))
            blocks = nxt + ([blocks[-1]] if len(blocks) % 2 else [])
        cur = blocks[0]
        m = jnp.max(cur[0], axis=0, keepdims=True)
        pos = jnp.min(jnp.where(cur[0] == m, cur[1], n_rows), axis=0, keepdims=True)
        rest = tuple(jnp.max(jnp.where(cur[1] == pos, t, -1), axis=0, keepdims=True) for t in cur[2:])
        return (m, pos) + rest

    top_v, top_i = [], []
    for _ in range(count):
        best = first_max(vals)
        top_v.append(best[0])
        top_i.append(best[-1])
        vals = jnp.where(iota == best[1], fill, vals)
    return jnp.concatenate(top_v, axis=0), jnp.concatenate(top_i, axis=0)


_COMB_PIECES = tuple((i, PEER_TOPK // (i + 1)) for i in range(SUBLANES))
_COMB_ROWS = sum(max(nj, SUBLANES) for _, nj in _COMB_PIECES) + SUBLANES


def _topk_kernel(s_ref, e_ref, g_ref):
    tt = s_ref.shape[2]
    key_iota = lax.broadcasted_iota(jnp.int32, (N_KEYS, tt), 0)
    comb_iota = lax.broadcasted_iota(jnp.int32, (_COMB_ROWS, tt), 0)
    ninf = float("-inf")

    def head(h, carry):
        s0, i0 = _take_top(s_ref[2 * h], key_iota, PEER_TOPK, ninf)
        s1, i1 = _take_top(s_ref[2 * h + 1], key_iota, PEER_TOPK, ninf)
        comb, cidx = [], []
        for i, nj in _COMB_PIECES:
            rows = max(nj, SUBLANES)
            sv = s0[i:i + 1, :] + s1[:rows]
            iv = i0[i:i + 1, :] * N_KEYS + i1[:rows]
            if nj < rows:
                keep = lax.broadcasted_iota(jnp.int32, (rows, tt), 0) < nj
                sv, iv = jnp.where(keep, sv, ninf), jnp.where(keep, iv, -1)
            comb.append(sv)
            cidx.append(iv)
        comb.append(s0[SUBLANES:] + s1[0:1, :])
        cidx.append(i0[SUBLANES:] * N_KEYS + i1[0:1, :])
        comb, cidx = jnp.concatenate(comb, axis=0), jnp.concatenate(cidx, axis=0)
        best, eidx = _take_top(comb, comb_iota, PEER_TOPK, ninf, payload=cidx)
        ex = jnp.exp(best - jnp.max(best, axis=0, keepdims=True))
        g_ref[h] = ex / jnp.sum(ex, axis=0, keepdims=True)
        e_ref[h] = eidx
        return carry

    lax.fori_loop(0, PEER_HEADS, head, 0)


def _topk(scores_t, tt=256):
    n = scores_t.shape[2]
    out_spec = pl.BlockSpec((PEER_HEADS, PEER_TOPK, tt), lambda i: (0, 0, i))
    return pl.pallas_call(
        _topk_kernel,
        out_shape=(jax.ShapeDtypeStruct((PEER_HEADS, PEER_TOPK, n), jnp.int32),
                   jax.ShapeDtypeStruct((PEER_HEADS, PEER_TOPK, n), F32)),
        grid=(n // tt,),
        in_specs=[pl.BlockSpec((2 * PEER_HEADS, N_KEYS, tt), lambda i: (0, 0, i))],
        out_specs=(out_spec, out_spec),
        compiler_params=_params(("arbitrary",)),
        name="peer_topk",
    )(scores_t)


def _gelu(a):
    return 0.5 * a * (1.0 + lax.erf(a * (2.0 ** -0.5)))


def _peer_kernel(idx_hbm, h_ref, g_ref, x_ref, uv_hbm, o_ref, idx_smem, buf, sem_i, sem):
    tt = h_ref.shape[0]
    step = pl.program_id(0)
    more = step + 1 < pl.num_programs(0)
    cur = step % 2

    def idx_copy(s, islot):
        return pltpu.make_async_copy(idx_hbm.at[pl.ds(s * tt, tt), :], idx_smem.at[islot], sem_i.at[islot])

    n_groups = tt // SUBLANES
    n_parts = 2 * PEER_CHUNKS
    cw = D_MODEL // PEER_CHUNKS

    def gather_start(islot, t, bank, j, part=None):
        ks = range(N_SEL) if part is None else range(part * N_SEL // n_parts, (part + 1) * N_SEL // n_parts)
        for k in ks:
            e = idx_smem[islot, t, k]
            pltpu.make_async_copy(uv_hbm.at[e], buf.at[bank, j, pl.ds(k, 1), :],
                                  sem.at[bank]).start(priority=k % 2)

    def gather_wait(bank):
        pltpu.make_async_copy(buf.at[bank], buf.at[bank], sem.at[bank]).wait()

    @pl.when(step == 0)
    def _():
        first = idx_copy(0, 0)
        first.start()
        first.wait()
        for j in range(SUBLANES):
            gather_start(0, j, 0, j)

    @pl.when(more)
    def _():
        idx_copy(step + 1, 1 - cur).start()

    row_iota = lax.broadcasted_iota(jnp.int32, (SUBLANES, 1), 0)

    def group(gi, carry):
        r0 = pl.multiple_of(gi * SUBLANES, SUBLANES)
        last = gi == n_groups - 1
        bank = gi % 2

        @pl.when(jnp.logical_and(last, more))
        def _():
            idx_copy(step + 1, 1 - cur).wait()

        n_islot = jnp.where(jnp.logical_and(last, more), 1 - cur, cur)
        n_t0 = jnp.where(last, 0, r0 + SUBLANES)
        gather_wait(bank)
        h8 = h_ref[pl.ds(r0, SUBLANES), :]
        g8 = g_ref[pl.ds(r0, SUBLANES), :]
        acc = [jnp.zeros((SUBLANES, cw), F32) for _ in range(PEER_CHUNKS)]
        issued = [0]

        def request():
            q = issued[0]
            issued[0] += 1
            gather_start(n_islot, n_t0 + q // n_parts, 1 - bank, q // n_parts, part=q % n_parts)

        def u_dot(r, c, p):
            sel = row_iota == r
            hm = jnp.where(sel, h8[:, c * cw:(c + 1) * cw], 0.0).astype(BF16)
            ub = buf[bank, r, :, c * cw:(c + 1) * cw].astype(BF16)
            return p + _dot_nt(hm, ub)

        def mix_weights(p):
            return (_gelu(p) * g8).astype(BF16)

        def v_dot(r, c, wm):
            vb = buf[bank, r, :, D_MODEL + c * cw:D_MODEL + (c + 1) * cw].astype(BF16)
            acc[c] = acc[c] + _dot(wm, vb)

        p_next = jnp.zeros((SUBLANES, N_SEL), F32)
        for c in range(PEER_CHUNKS):
            request()
            p_next = u_dot(0, c, p_next)
        for r in range(SUBLANES):
            wm = mix_weights(p_next)
            p_next = jnp.zeros((SUBLANES, N_SEL), F32)
            for c in range(PEER_CHUNKS):
                if r + 1 < SUBLANES:
                    request()
                    p_next = u_dot(r + 1, c, p_next)
                request()
                v_dot(r, c, wm)
        assert issued[0] == SUBLANES * n_parts
        o_ref[pl.ds(r0, SUBLANES), :] = x_ref[pl.ds(r0, SUBLANES), :] + jnp.concatenate(acc, axis=1)
        return carry

    lax.fori_loop(0, n_groups, group, 0)

    @pl.when(jnp.logical_not(more))
    def _():
        gather_wait(n_groups % 2)


def _peer(eidx, h2, gate, x2d, uv, tt):
    n = x2d.shape[0]
    assert tt % (2 * SUBLANES) == 0
    row = lambda i: (i, 0)
    return pl.pallas_call(
        _peer_kernel,
        out_shape=jax.ShapeDtypeStruct((n, D_MODEL), F32),
        grid=(n // tt,),
        in_specs=[
            pl.BlockSpec(memory_space=pl.ANY),
            pl.BlockSpec((tt, D_MODEL), row),
            pl.BlockSpec((tt, N_SEL), row),
            pl.BlockSpec((tt, D_MODEL), row),
            pl.BlockSpec(memory_space=pl.ANY),
        ],
        out_specs=pl.BlockSpec((tt, D_MODEL), row),
        scratch_shapes=[
            pltpu.SMEM((2, tt, N_SEL), jnp.int32),
            pltpu.VMEM((2, SUBLANES, N_SEL, 2 * D_MODEL), F32),
            pltpu.SemaphoreType.DMA((2,)),
            pltpu.SemaphoreType.DMA((2,)),
        ],
        compiler_params=_params(("arbitrary",)),
        name="peer_mix",
    )(eidx, h2, gate, x2d, uv)


def _ple_kernel(x_ref, p_ref, lnp_ref, wg_ref, wp_ref, lnf_ref, y_ref):
    x = x_ref[...]
    gate = _sigmoid(_dot(_rms(x, lnp_ref[...]).astype(BF16), wg_ref[...]))
    x = x + gate * _dot(p_ref[...].astype(BF16), wp_ref[...])
    y_ref[...] = _rms(x, lnf_ref[...])


def _ple(x2d, p2d, ln_ple, w_gate, w_proj, ln_final, tm):
    n = x2d.shape[0]
    pd = p2d.shape[1]
    row = lambda i: (i, 0)
    const = lambda i: (0, 0)
    return pl.pallas_call(
        _ple_kernel,
        out_shape=jax.ShapeDtypeStruct((n, D_MODEL), F32),
        grid=(n // tm,),
        in_specs=[
            pl.BlockSpec((tm, D_MODEL), row),
            pl.BlockSpec((tm, pd), row),
            pl.BlockSpec((1, D_MODEL), const),
            pl.BlockSpec((D_MODEL, D_MODEL), const),
            pl.BlockSpec((pd, D_MODEL), const),
            pl.BlockSpec((1, D_MODEL), const),
        ],
        out_specs=pl.BlockSpec((tm, D_MODEL), row),
        compiler_params=_params(("arbitrary",)),
        name="ple_final",
    )(x2d, p2d, ln_ple, w_gate, w_proj, ln_final)


def _rope_tables(pos):
    inv = ROPE_THETA ** (-jnp.arange(0, HEAD_DIM_A, 2, dtype=F32) / HEAD_DIM_A)
    ang = pos.astype(F32)[:, None] * inv[None, :]
    cos, sin = jnp.cos(ang), jnp.sin(ang)
    reps = PROJ_TILE // HEAD_DIM_A
    cos = jnp.tile(jnp.concatenate([cos, cos], axis=1), (1, reps))
    sin = jnp.tile(jnp.concatenate([-sin, sin], axis=1), (1, reps))
    k_end = (COL["k_a"][0] + COL["k_a"][1]) % PROJ_TILE
    if k_end:
        keep = jnp.arange(PROJ_TILE) < k_end
        cos = jnp.concatenate([cos, jnp.where(keep, cos, 1.0)], axis=1)
        sin = jnp.concatenate([sin, jnp.where(keep, sin, 0.0)], axis=1)
    return cos, sin


def _pick(n, prefs):
    for c in prefs:
        if n % c == 0:
            return c
    raise ValueError(f"no tile of {prefs} divides {n}")


def _group(x, pe, pos0, cache_k, cache_v, s0, wts):
    bn, t, _ = x.shape
    n = bn * t
    prompt = cache_k is None
    x2d = x.reshape(n, D_MODEL)
    tm = _pick(n, (1024, 512, 256, 128))

    pos = pos0 + jnp.arange(t, dtype=jnp.int32)
    cos, sin = _rope_tables(pos)
    if t < tm:
        cos, sin = jnp.tile(cos, (tm // t, 1)), jnp.tile(sin, (tm // t, 1))
    proj = _in_proj(x2d, wts["ln1"], wts["w_in"], cos, sin, wts["colscale"], tm)
    proj3 = proj.reshape(bn, t, IN_COLS)

    if prompt:
        c_att, bt_att = ATTN_BLOCK, 1
        ctx_k = ctx_v = proj3
    else:
        c_att, bt_att = t, _pick(bn, (8, 4, 2, 1))
        ctx_k = cache_k.reshape(bn, WINDOW, N_KV_A * HEAD_DIM_A)
        ctx_v = cache_v.reshape(bn, WINDOW, N_KV_A * HEAD_DIM_A)
    o_a = _attention(wts["sinks"], proj3, ctx_k, ctx_v, bt_att, c_att, prompt)

    c_ret = math.gcd(t, RET_CHUNK)
    bt_ret = 1 if prompt else _pick(bn, (8, 4, 2, 1))
    o_r, s_new = _retention(proj3, s0, wts["ln_ret"], bt_ret, c_ret)

    m = _merge(o_a.reshape(n, -1), o_r.reshape(n, -1), proj, wts["w_br_a"], wts["w_br_r"], tm)
    x2 = _out_proj(x2d, m, wts["w_o"], tm)

    h2, scores_t = _scores(x2, wts["ln2"], wts["w_q"], wts["sub_keys"], _pick(n, (1024, 512, 256, 128)))
    eidx_t, gate_t = _topk(scores_t, _pick(n, (512, 256, 128)))
    eidx = eidx_t.reshape(N_SEL, n).T
    gate = gate_t.reshape(N_SEL, n).T
    x3 = _peer(eidx, h2, gate, x2, wts["peer_uv"], _pick(n, (128, 64, 32, 16)))

    y = _ple(x3, pe.reshape(n, -1), wts["ln_ple"], wts["w_ple_gate"], wts["w_ple_proj"], wts["ln_final"],
             _pick(n, (512, 256, 128)))

    k_off, v_off = COL["k_a"][0], COL["v_a"][0]
    kv_w = N_KV_A * HEAD_DIM_A
    k_new, v_new = proj3[:, :, k_off:k_off + kv_w], proj3[:, :, v_off:v_off + kv_w]
    if prompt:
        k_win, v_win = k_new[:, -WINDOW:], v_new[:, -WINDOW:]
    else:
        k_win = jnp.concatenate([ctx_k, k_new], axis=1)[:, -WINDOW:]
        v_win = jnp.concatenate([ctx_v, v_new], axis=1)[:, -WINDOW:]
    shp = (bn, WINDOW, N_KV_A, HEAD_DIM_A)
    return y.reshape(bn, t, D_MODEL), k_win.reshape(shp), v_win.reshape(shp), s_new


def kernel(x_prompt, x_sample, cache_k_win, cache_v_win, state_ret, p_prompt, p_sample, ln1, w_in, attn_sinks, ln_ret, w_branch_attn, w_branch_ret, w_out, ln2, w_peer_query, peer_sub_keys, peer_u, peer_v, ln_ple, w_ple_gate, w_ple_proj, ln_final):
    depth = ln1.shape[0]
    assert depth == 1, "single-layer step"
    i = 0
    colscale = jnp.ones((IN_COLS,), F32).at[COL["k_r"][0]:COL["k_r"][0] + COL["k_r"][1]].set(DK_R ** -0.5)
    wts = dict(
        ln1=ln1[i][None, :],
        w_in=jnp.concatenate([w_in[i][:, _REF_COL[f][0]:_REF_COL[f][0] + _REF_COL[f][1]] for f in _NEW_ORDER],
                             axis=1).astype(BF16),
        colscale=colscale[None, :],
        sinks=attn_sinks[i],
        ln_ret=ln_ret[i].reshape(N_HEADS_R, DV_R),
        w_br_a=w_branch_attn[i].astype(BF16),
        w_br_r=w_branch_ret[i].astype(BF16),
        w_o=w_out[i].astype(BF16),
        ln2=ln2[i][None, :],
        w_q=w_peer_query[i].astype(BF16),
        sub_keys=peer_sub_keys[i].astype(BF16),
        peer_uv=jnp.concatenate([peer_u[i], peer_v[i]], axis=1)[:, None, :],
        ln_ple=ln_ple[i][None, :],
        w_ple_gate=w_ple_gate[i].astype(BF16),
        w_ple_proj=w_ple_proj[i].astype(BF16),
        ln_final=ln_final[None, :],
    )
    yp, kp, vp, sp = _group(x_prompt, p_prompt[i], 0, None, None, None, wts)
    ys, ks, vs, ss = _group(x_sample, p_sample[i], PAST_LEN, cache_k_win[i], cache_v_win[i], state_ret[i], wts)
    return (yp, ys, kp[None], vp[None], sp[None], ks[None], vs[None], ss[None])
```

```python
import functools
import math

import jax
import jax.numpy as jnp
from jax import lax
from jax.experimental import pallas as pl
from jax.experimental.pallas import tpu as pltpu

F32 = jnp.float32
BF16 = jnp.bfloat16

D_MODEL = 2048
PAST_LEN = 16384
HEAD_DIM_A = 64
N_HEADS_A = 16
N_KV_A = 4
GROUP_A = 4
WINDOW = 128
ATTN_BLOCK = 128
N_HEADS_R = 8
DK_R = 64
DV_R = 128
RET_CHUNK = 128
PEER_HEADS = 8
N_KEYS = 128
PEER_TOPK = 16
D_KEY_HALF = 128
N_SEL = PEER_HEADS * PEER_TOPK
PEER_CHUNKS = 4
ROPE_THETA = 10000.0
EPS = 1e-6
NEG_INF = -1e30

LANES = 128
SUBLANES = 8
VMEM_LIMIT_BYTES = 56 * 1024 * 1024

_REF_FIELDS = (("q_a", 1024), ("k_a", 256), ("v_a", 256), ("q_r", 512), ("k_r", 512),
               ("v_r", 1024), ("g_r", 1024), ("gate_a", 2048), ("gate_r", 2048))
_NEW_ORDER = ("gate_a", "gate_r", "q_a", "v_r", "g_r", "q_r", "k_r", "k_a", "v_a")
_ROPE_FIELDS = ("q_a", "q_r", "k_r", "k_a")
PROJ_TILE = 512


def _layout():
    ref_off, o = {}, 0
    for name, w in _REF_FIELDS:
        ref_off[name] = (o, w)
        o += w
    new_off, o = {}, 0
    for name in _NEW_ORDER:
        new_off[name] = (o, ref_off[name][1])
        o += ref_off[name][1]
    return new_off, ref_off, o


COL, _REF_COL, IN_COLS = _layout()


def _blk(name, width):
    off, w = COL[name]
    assert off % width == 0 and w % width == 0
    return off // width


def _params(sem, vmem=VMEM_LIMIT_BYTES):
    return pltpu.CompilerParams(dimension_semantics=sem, vmem_limit_bytes=vmem)


def _rms(x, g):
    return x * lax.rsqrt(jnp.mean(x * x, axis=-1, keepdims=True) + EPS) * g


def _sigmoid(x):
    return 1.0 / (1.0 + jnp.exp(-x))


def _dot(a, b):
    return jnp.dot(a, b, preferred_element_type=F32)


def _dot_nt(a, b):
    return lax.dot_general(a, b, (((1,), (1,)), ((), ())), preferred_element_type=F32)


def _dot_tn(a, b):
    return lax.dot_general(a, b, (((0,), (0,)), ((), ())), preferred_element_type=F32)


def _in_proj_kernel(x_ref, g_ref, w_ref, cos_ref, sin_ref, cs_ref, o_ref, h_ref, *, rope_ranges):
    j = pl.program_id(1)

    @pl.when(j == 0)
    def _():
        h_ref[...] = _rms(x_ref[...], g_ref[...]).astype(BF16)

    is_rope = functools.reduce(jnp.logical_or, [(j >= a) & (j < b) for a, b in rope_ranges])

    @pl.when(is_rope)
    def _():
        acc = _dot(h_ref[...], w_ref[...])
        tn = acc.shape[1]
        lane = lax.broadcasted_iota(jnp.int32, acc.shape, 1)
        first_half = (lane % HEAD_DIM_A) < (HEAD_DIM_A // 2)
        partner = jnp.where(first_half, pltpu.roll(acc, tn - HEAD_DIM_A // 2, 1), pltpu.roll(acc, HEAD_DIM_A // 2, 1))
        o_ref[...] = (acc * cos_ref[...] + partner * sin_ref[...]) * cs_ref[...]

    @pl.when(jnp.logical_not(is_rope))
    def _():
        o_ref[...] = _dot(h_ref[...], w_ref[...])


def _in_proj(x2d, ln, w, cos, sin, colscale, tm):
    n = x2d.shape[0]
    rt = cos.shape[0] // tm
    rope_ranges = tuple((COL[f][0] // PROJ_TILE, -(-(COL[f][0] + COL[f][1]) // PROJ_TILE)) for f in _ROPE_FIELDS)
    ends = [COL[f][0] + COL[f][1] for f in _ROPE_FIELDS if (COL[f][0] + COL[f][1]) % PROJ_TILE]
    assert len(ends) <= 1 and cos.shape[1] == (1 + len(ends)) * PROJ_TILE
    mixed = ends[0] // PROJ_TILE if ends else -1
    table_spec = pl.BlockSpec((tm, PROJ_TILE), lambda i, j: (i % rt, jnp.where(j == mixed, 1, 0)))
    return pl.pallas_call(
        functools.partial(_in_proj_kernel, rope_ranges=rope_ranges),
        out_shape=jax.ShapeDtypeStruct((n, IN_COLS), F32),
        grid=(n // tm, IN_COLS // PROJ_TILE),
        in_specs=[
            pl.BlockSpec((tm, D_MODEL), lambda i, j: (i, 0)),
            pl.BlockSpec((1, D_MODEL), lambda i, j: (0, 0)),
            pl.BlockSpec((D_MODEL, PROJ_TILE), lambda i, j: (0, j)),
            table_spec,
            table_spec,
            pl.BlockSpec((1, PROJ_TILE), lambda i, j: (0, j)),
        ],
        out_specs=pl.BlockSpec((tm, PROJ_TILE), lambda i, j: (i, j)),
        scratch_shapes=[pltpu.VMEM((tm, D_MODEL), BF16)],
        compiler_params=_params(("arbitrary", "arbitrary")),
        name="in_proj",
    )(x2d, ln, w, cos, sin, colscale)


def _attn_kernel(sink_ref, q_ref, kc_ref, vc_ref, kx_ref, vx_ref, o_ref, *, first_ctx_invalid):
    bt, c, _ = q_ref.shape
    rows = GROUP_A * c
    qi = lax.broadcasted_iota(jnp.int32, (rows, WINDOW), 0) % c
    kj = lax.broadcasted_iota(jnp.int32, (rows, WINDOW), 1)
    ctx_ok = kj >= qi
    if first_ctx_invalid:
        ctx_ok = jnp.logical_and(ctx_ok, pl.program_id(1) > 0)
    qi_c = lax.broadcasted_iota(jnp.int32, (rows, c), 0) % c
    kj_c = lax.broadcasted_iota(jnp.int32, (rows, c), 1)
    cur_ok = kj_c <= qi_c
    grp = lax.broadcasted_iota(jnp.int32, (rows, 1), 0) // c
    scale = HEAD_DIM_A ** -0.5

    def one(b):
        q = q_ref[b]
        kc, vc, kx, vx = kc_ref[b], vc_ref[b], kx_ref[b], vx_ref[b]
        outs = [None] * N_HEADS_A
        for kh in range(N_KV_A):
            sl = slice(kh * HEAD_DIM_A, (kh + 1) * HEAD_DIM_A)
            q4 = jnp.concatenate(
                [q[:, (kh * GROUP_A + g) * HEAD_DIM_A:(kh * GROUP_A + g + 1) * HEAD_DIM_A] for g in range(GROUP_A)],
                axis=0).astype(BF16)
            s_x = jnp.where(ctx_ok, _dot_nt(q4, kx[:, sl].astype(BF16)) * scale, NEG_INF)
            s_c = jnp.where(cur_ok, _dot_nt(q4, kc[:, sl].astype(BF16)) * scale, NEG_INF)
            sink = jnp.zeros((rows, 1), F32)
            for g in range(GROUP_A):
                sink = jnp.where(grp == g, sink_ref[kh * GROUP_A + g], sink)
            m = jnp.maximum(jnp.maximum(jnp.max(s_x, axis=-1, keepdims=True), jnp.max(s_c, axis=-1, keepdims=True)), sink)
            e_x = jnp.exp(s_x - m)
            e_c = jnp.exp(s_c - m)
            den = jnp.sum(e_x, axis=-1, keepdims=True) + jnp.sum(e_c, axis=-1, keepdims=True) + jnp.exp(sink - m)
            o4 = _dot((e_x / den).astype(BF16), vx[:, sl].astype(BF16)) + _dot((e_c / den).astype(BF16), vc[:, sl].astype(BF16))
            for g in range(GROUP_A):
                outs[kh * GROUP_A + g] = o4[g * c:(g + 1) * c]
        o_ref[b] = jnp.concatenate(outs, axis=1).astype(o_ref.dtype)

    for b in range(bt):
        one(b)


def _attention(sinks, proj3, ctx_k, ctx_v, bt, c, prompt):
    bn, t, _ = proj3.shape
    qb, kb, vb = _blk("q_a", 1024), _blk("k_a", 256), _blk("v_a", 256)
    if prompt:
        ctx_specs = [pl.BlockSpec((bt, WINDOW, 256), lambda b, n: (b, jnp.maximum(n - 1, 0), kb)),
                     pl.BlockSpec((bt, WINDOW, 256), lambda b, n: (b, jnp.maximum(n - 1, 0), vb))]
    else:
        ctx_specs = [pl.BlockSpec((bt, WINDOW, 256), lambda b, n: (b, 0, 0)),
                     pl.BlockSpec((bt, WINDOW, 256), lambda b, n: (b, 0, 0))]
    return pl.pallas_call(
        functools.partial(_attn_kernel, first_ctx_invalid=prompt),
        out_shape=jax.ShapeDtypeStruct((bn, t, N_HEADS_A * HEAD_DIM_A), BF16),
        grid=(bn // bt, t // c),
        in_specs=[
            pl.BlockSpec(memory_space=pltpu.SMEM),
            pl.BlockSpec((bt, c, 1024), lambda b, n: (b, n, qb)),
            pl.BlockSpec((bt, c, 256), lambda b, n: (b, n, kb)),
            pl.BlockSpec((bt, c, 256), lambda b, n: (b, n, vb)),
        ] + ctx_specs,
        out_specs=pl.BlockSpec((bt, c, 1024), lambda b, n: (b, n, 0)),
        compiler_params=_params(("arbitrary", "arbitrary")),
        name="attn_prompt" if prompt else "attn_sample",
    )(sinks, proj3, proj3, proj3, ctx_k, ctx_v)


def _ret_kernel(cdec_ref, q_ref, k_ref, v_ref, g_ref, intra_ref, cross_ref, kdec_ref, ln_ref, *rest, has_s0):
    if has_s0:
        s0_ref, o_ref, sout_ref, s_ref = rest
    else:
        o_ref, sout_ref, s_ref = rest
    bt = q_ref.shape[0]
    ci = pl.program_id(1)

    @pl.when(ci == 0)
    def _():
        s_ref[...] = s0_ref[...] if has_s0 else jnp.zeros(s_ref.shape, F32)

    def one(b):
        q, k, v, gt = q_ref[b], k_ref[b], v_ref[b], g_ref[b]
        outs = []
        for h in range(N_HEADS_R):
            qh = q[:, h * DK_R:(h + 1) * DK_R].astype(BF16)
            kh = k[:, h * DK_R:(h + 1) * DK_R]
            vh = v[:, h * DV_R:(h + 1) * DV_R].astype(BF16)
            s = s_ref[b, h]
            att = _dot_nt(qh, kh.astype(BF16)) * intra_ref[h]
            o = _dot(att.astype(BF16), vh) + _dot(qh, s.astype(BF16)) * cross_ref[h]
            s_ref[b, h] = s * cdec_ref[h] + _dot_tn((kh * kdec_ref[h]).astype(BF16), vh)
            mu = jnp.mean(o, axis=-1, keepdims=True)
            d = o - mu
            var = jnp.mean(d * d, axis=-1, keepdims=True)
            gh = gt[:, h * DV_R:(h + 1) * DV_R]
            outs.append(d * lax.rsqrt(var + EPS) * ln_ref[h:h + 1, :] * (gh * _sigmoid(gh)))
        o_ref[b] = jnp.concatenate(outs, axis=1).astype(o_ref.dtype)

    for b in range(bt):
        one(b)

    @pl.when(ci == pl.num_programs(1) - 1)
    def _():
        sout_ref[...] = s_ref[...]


def _retention(proj3, s0, ln_ret, bt, c):
    bn, t, _ = proj3.shape
    log_g = jnp.log1p(-jnp.exp2(-5.0 - jnp.arange(N_HEADS_R, dtype=F32)))
    i = jnp.arange(c, dtype=F32)
    diff = i[:, None] - i[None, :]
    intra = jnp.where(diff[None] >= 0, jnp.exp(jnp.maximum(diff, 0.0)[None] * log_g[:, None, None]), 0.0)
    cross = jnp.exp((i + 1.0)[None, :] * log_g[:, None])
    kdec = jnp.exp((c - 1.0 - i)[None, :] * log_g[:, None])
    cdec = jnp.exp(c * log_g)
    cross_b = jnp.broadcast_to(cross[:, :, None], (N_HEADS_R, c, DV_R))
    kdec_b = jnp.broadcast_to(kdec[:, :, None], (N_HEADS_R, c, DK_R))
    has_s0 = s0 is not None
    const3 = lambda b, n: (0, 0, 0)
    state_spec = pl.BlockSpec((bt, N_HEADS_R, DK_R, DV_R), lambda b, n: (b, 0, 0, 0))
    in_specs = [
        pl.BlockSpec(memory_space=pltpu.SMEM),
        pl.BlockSpec((bt, c, 512), lambda b, n: (b, n, _blk("q_r", 512))),
        pl.BlockSpec((bt, c, 512), lambda b, n: (b, n, _blk("k_r", 512))),
        pl.BlockSpec((bt, c, 1024), lambda b, n: (b, n, _blk("v_r", 1024))),
        pl.BlockSpec((bt, c, 1024), lambda b, n: (b, n, _blk("g_r", 1024))),
        pl.BlockSpec((N_HEADS_R, c, c), const3),
        pl.BlockSpec((N_HEADS_R, c, DV_R), const3),
        pl.BlockSpec((N_HEADS_R, c, DK_R), const3),
        pl.BlockSpec((N_HEADS_R, DV_R), lambda b, n: (0, 0)),
    ]
    args = [cdec, proj3, proj3, proj3, proj3, intra, cross_b, kdec_b, ln_ret]
    if has_s0:
        in_specs.append(state_spec)
        args.append(s0)
    return pl.pallas_call(
        functools.partial(_ret_kernel, has_s0=has_s0),
        out_shape=(jax.ShapeDtypeStruct((bn, t, N_HEADS_R * DV_R), BF16),
                   jax.ShapeDtypeStruct((bn, N_HEADS_R, DK_R, DV_R), F32)),
        grid=(bn // bt, t // c),
        in_specs=in_specs,
        out_specs=(pl.BlockSpec((bt, c, 1024), lambda b, n: (b, n, 0)), state_spec),
        scratch_shapes=[pltpu.VMEM((bt, N_HEADS_R, DK_R, DV_R), F32)],
        compiler_params=_params(("arbitrary", "arbitrary")),
        name="ret_sample" if has_s0 else "ret_prompt",
    )(*args)


def _merge_kernel(oa_ref, or_ref, ga_ref, gr_ref, wa_ref, wr_ref, m_ref):
    br_a = _dot(oa_ref[...], wa_ref[...])
    br_r = _dot(or_ref[...], wr_ref[...])
    m_ref[...] = (_sigmoid(ga_ref[...]) * br_a + _sigmoid(gr_ref[...]) * br_r).astype(m_ref.dtype)


def _merge(o_a, o_r, proj, w_a, w_r, tm, tn=512):
    n = o_a.shape[0]
    ga, gr = _blk("gate_a", tn), _blk("gate_r", tn)
    return pl.pallas_call(
        _merge_kernel,
        out_shape=jax.ShapeDtypeStruct((n, D_MODEL), BF16),
        grid=(n // tm, D_MODEL // tn),
        in_specs=[
            pl.BlockSpec((tm, 1024), lambda i, j: (i, 0)),
            pl.BlockSpec((tm, 1024), lambda i, j: (i, 0)),
            pl.BlockSpec((tm, tn), lambda i, j: (i, ga + j)),
            pl.BlockSpec((tm, tn), lambda i, j: (i, gr + j)),
            pl.BlockSpec((1024, tn), lambda i, j: (0, j)),
            pl.BlockSpec((1024, tn), lambda i, j: (0, j)),
        ],
        out_specs=pl.BlockSpec((tm, tn), lambda i, j: (i, j)),
        compiler_params=_params(("arbitrary", "arbitrary")),
        name="merge",
    )(o_a, o_r, proj, proj, w_a, w_r)


def _out_proj_kernel(x_ref, m_ref, w_ref, o_ref):
    o_ref[...] = x_ref[...] + _dot(m_ref[...], w_ref[...])


def _out_proj(x2d, m, w_o, tm, tn=512):
    n = x2d.shape[0]
    return pl.pallas_call(
        _out_proj_kernel,
        out_shape=jax.ShapeDtypeStruct((n, D_MODEL), F32),
        grid=(n // tm, D_MODEL // tn),
        in_specs=[
            pl.BlockSpec((tm, tn), lambda i, j: (i, j)),
            pl.BlockSpec((tm, D_MODEL), lambda i, j: (i, 0)),
            pl.BlockSpec((D_MODEL, tn), lambda i, j: (0, j)),
        ],
        out_specs=pl.BlockSpec((tm, tn), lambda i, j: (i, j)),
        compiler_params=_params(("arbitrary", "arbitrary")),
        name="out_proj",
    )(x2d, m, w_o)


def _scores_kernel(x_ref, g_ref, w_ref, sk_ref, h_out_ref, s_ref, h_ref):
    j = pl.program_id(1)

    @pl.when(j == 0)
    def _():
        h = _rms(x_ref[...], g_ref[...])
        h_ref[...] = h.astype(BF16)
        h_out_ref[...] = h

    qry = _dot(h_ref[...], w_ref[...]).astype(BF16)
    for g in range(qry.shape[1] // D_KEY_HALF):
        s_ref[g] = _dot_nt(sk_ref[g % 2], qry[:, g * D_KEY_HALF:(g + 1) * D_KEY_HALF])


def _scores(x2d, ln, w_q, sub_keys, tm, tn=512):
    n = x2d.shape[0]
    ng = tn // D_KEY_HALF
    return pl.pallas_call(
        _scores_kernel,
        out_shape=(jax.ShapeDtypeStruct((n, D_MODEL), F32),
                   jax.ShapeDtypeStruct((2 * PEER_HEADS, N_KEYS, n), F32)),
        grid=(n // tm, D_MODEL // tn),
        in_specs=[
            pl.BlockSpec((tm, D_MODEL), lambda i, j: (i, 0)),
            pl.BlockSpec((1, D_MODEL), lambda i, j: (0, 0)),
            pl.BlockSpec((D_MODEL, tn), lambda i, j: (0, j)),
            pl.BlockSpec((2, N_KEYS, D_KEY_HALF), lambda i, j: (0, 0, 0)),
        ],
        out_specs=(pl.BlockSpec((tm, D_MODEL), lambda i, j: (i, 0)),
                   pl.BlockSpec((ng, N_KEYS, tm), lambda i, j: (j, 0, i))),
        scratch_shapes=[pltpu.VMEM((tm, D_MODEL), BF16)],
        compiler_params=_params(("arbitrary", "arbitrary")),
        name="peer_scores",
    )(x2d, ln, w_q, sub_keys)


def _take_top(vals, iota, count, fill, payload=None):
    n_rows = vals.shape[0]
    tags = (iota,) if payload is None else (iota, payload)

    def first_max(v):
        row8 = lax.broadcasted_iota(jnp.int32, (SUBLANES, v.shape[1]), 0)
        blocks = [(v[i:i + SUBLANES], row8 + i) + tuple(t[i:i + SUBLANES] for t in tags[1:])
                  for i in range(0, n_rows, SUBLANES)]
        while len(blocks) > 1:
            nxt = []
            for a, b in zip(blocks[0::2], blocks[1::2]):
                take_a = a[0] >= b[0]
                nxt.append(tuple(jnp.where(take_a, x, y) for x, y in zip(a, b)))
            blocks = nxt + ([blocks[-1]] if len(blocks) % 2 else [])
        cur = blocks[0]
        m = jnp.max(cur[0], axis=0, keepdims=True)
        pos = jnp.min(jnp.where(cur[0] == m, cur[1], n_rows), axis=0, keepdims=True)
        rest = tuple(jnp.max(jnp.where(cur[1] == pos, t, -1), axis=0, keepdims=True) for t in cur[2:])
        return (m, pos) + rest

    top_v, top_i = [], []
    for _ in range(count):
        best = first_max(vals)
        top_v.append(best[0])
        top_i.append(best[-1])
        vals = jnp.where(iota == best[1], fill, vals)
    return jnp.concatenate(top_v, axis=0), jnp.concatenate(top_i, axis=0)


_COMB_PIECES = tuple((i, PEER_TOPK // (i + 1)) for i in range(SUBLANES))
_COMB_ROWS = sum(max(nj, SUBLANES) for _, nj in _COMB_PIECES) + SUBLANES


def _topk_kernel(s_ref, e_ref, g_ref):
    tt = s_ref.shape[2]
    key_iota = lax.broadcasted_iota(jnp.int32, (N_KEYS, tt), 0)
    comb_iota = lax.broadcasted_iota(jnp.int32, (_COMB_ROWS, tt), 0)
    ninf = float("-inf")

    def head(h, carry):
        s0, i0 = _take_top(s_ref[2 * h], key_iota, PEER_TOPK, ninf)
        s1, i1 = _take_top(s_ref[2 * h + 1], key_iota, PEER_TOPK, ninf)
        comb, cidx = [], []
        for i, nj in _COMB_PIECES:
            rows = max(nj, SUBLANES)
            sv = s0[i:i + 1, :] + s1[:rows]
            iv = i0[i:i + 1, :] * N_KEYS + i1[:rows]
            if nj < rows:
                keep = lax.broadcasted_iota(jnp.int32, (rows, tt), 0) < nj
                sv, iv = jnp.where(keep, sv, ninf), jnp.where(keep, iv, -1)
            comb.append(sv)
            cidx.append(iv)
        comb.append(s0[SUBLANES:] + s1[0:1, :])
        cidx.append(i0[SUBLANES:] * N_KEYS + i1[0:1, :])
        comb, cidx = jnp.concatenate(comb, axis=0), jnp.concatenate(cidx, axis=0)
        best, eidx = _take_top(comb, comb_iota, PEER_TOPK, ninf, payload=cidx)
        ex = jnp.exp(best - jnp.max(best, axis=0, keepdims=True))
        g_ref[h] = ex / jnp.sum(ex, axis=0, keepdims=True)
        e_ref[h] = eidx
        return carry

    lax.fori_loop(0, PEER_HEADS, head, 0)


def _topk(scores_t, tt=256):
    n = scores_t.shape[2]
    out_spec = pl.BlockSpec((PEER_HEADS, PEER_TOPK, tt), lambda i: (0, 0, i))
    return pl.pallas_call(
        _topk_kernel,
        out_shape=(jax.ShapeDtypeStruct((PEER_HEADS, PEER_TOPK, n), jnp.int32),
                   jax.ShapeDtypeStruct((PEER_HEADS, PEER_TOPK, n), F32)),
        grid=(n // tt,),
        in_specs=[pl.BlockSpec((2 * PEER_HEADS, N_KEYS, tt), lambda i: (0, 0, i))],
        out_specs=(out_spec, out_spec),
        compiler_params=_params(("arbitrary",)),
        name="peer_topk",
    )(scores_t)


def _gelu(a):
    return 0.5 * a * (1.0 + lax.erf(a * (2.0 ** -0.5)))


def _peer_kernel(idx_hbm, h_ref, g2_ref, x_ref, uv_hbm, o_ref, idx_smem, buf, sem_i, sem):
    tt = h_ref.shape[0]
    half = D_MODEL // 2
    step = pl.program_id(0)
    more = step + 1 < pl.num_programs(0)
    cur = step % 2

    def idx_copy(s, islot):
        return pltpu.make_async_copy(idx_hbm.at[pl.ds(s * tt, tt), :], idx_smem.at[islot], sem_i.at[islot])

    n_groups = tt // SUBLANES
    n_parts = 2 * PEER_CHUNKS
    cw = half // PEER_CHUNKS

    def gather_start(islot, t, bank, j, part=None):
        ks = range(N_SEL) if part is None else range(part * N_SEL // n_parts, (part + 1) * N_SEL // n_parts)
        for k in ks:
            e = idx_smem[islot, t, k]
            pltpu.make_async_copy(uv_hbm.at[e], buf.at[bank, j, pl.ds(2 * k, 2), :],
                                  sem.at[bank]).start(priority=k % 2)

    def gather_wait(bank):
        pltpu.make_async_copy(buf.at[bank], buf.at[bank], sem.at[bank]).wait()

    @pl.when(step == 0)
    def _():
        first = idx_copy(0, 0)
        first.start()
        first.wait()
        for j in range(SUBLANES):
            gather_start(0, j, 0, j)

    @pl.when(more)
    def _():
        idx_copy(step + 1, 1 - cur).start()

    row_iota = lax.broadcasted_iota(jnp.int32, (SUBLANES, 1), 0)
    even = lax.broadcasted_iota(jnp.int32, (SUBLANES, 2 * N_SEL), 1) % 2 == 0

    def group(gi, carry):
        r0 = pl.multiple_of(gi * SUBLANES, SUBLANES)
        last = gi == n_groups - 1
        bank = gi % 2

        @pl.when(jnp.logical_and(last, more))
        def _():
            idx_copy(step + 1, 1 - cur).wait()

        n_islot = jnp.where(jnp.logical_and(last, more), 1 - cur, cur)
        n_t0 = jnp.where(last, 0, r0 + SUBLANES)
        gather_wait(bank)
        h8 = h_ref[pl.ds(r0, SUBLANES), :]
        g2 = g2_ref[pl.ds(r0, SUBLANES), :]
        acc = [jnp.zeros((2 * SUBLANES, cw), F32) for _ in range(PEER_CHUNKS)]
        issued = [0]

        def request():
            q = issued[0]
            issued[0] += 1
            gather_start(n_islot, n_t0 + q // n_parts, 1 - bank, q // n_parts, part=q % n_parts)

        def u_dot(r, c, p):
            sel = row_iota == r
            hm = jnp.concatenate([jnp.where(sel, h8[:, c * cw:(c + 1) * cw], 0.0),
                                  jnp.where(sel, h8[:, half + c * cw:half + (c + 1) * cw], 0.0)],
                                 axis=0).astype(BF16)
            ub = buf[bank, r, :, c * cw:(c + 1) * cw]
            return p + _dot_nt(hm, ub)

        def mix_weights(p):
            part = jnp.where(even, p[:SUBLANES], p[SUBLANES:])
            a2 = part + jnp.where(even, pltpu.roll(part, 2 * N_SEL - 1, 1), pltpu.roll(part, 1, 1))
            w2 = _gelu(a2) * g2
            return jnp.concatenate([jnp.where(even, w2, 0.0), jnp.where(even, 0.0, w2)], axis=0).astype(BF16)

        def v_dot(r, c, wm):
            vb = buf[bank, r, :, half + c * cw:half + (c + 1) * cw]
            acc[c] = acc[c] + _dot(wm, vb)

        p_next = jnp.zeros((2 * SUBLANES, 2 * N_SEL), F32)
        for c in range(PEER_CHUNKS):
            request()
            p_next = u_dot(0, c, p_next)
        for r in range(SUBLANES):
            wm = mix_weights(p_next)
            p_next = jnp.zeros((2 * SUBLANES, 2 * N_SEL), F32)
            for c in range(PEER_CHUNKS):
                if r + 1 < SUBLANES:
                    request()
                    p_next = u_dot(r + 1, c, p_next)
                request()
                v_dot(r, c, wm)
        assert issued[0] == SUBLANES * n_parts
        peer = jnp.concatenate([a[:SUBLANES] for a in acc] + [a[SUBLANES:] for a in acc], axis=1)
        o_ref[pl.ds(r0, SUBLANES), :] = x_ref[pl.ds(r0, SUBLANES), :] + peer
        return carry

    lax.fori_loop(0, n_groups, group, 0)

    @pl.when(jnp.logical_not(more))
    def _():
        gather_wait(n_groups % 2)


def _pair_rows(u, v):
    half = u.shape[1] // 2
    return jnp.stack([jnp.concatenate([u[:, :half], v[:, :half]], axis=1),
                      jnp.concatenate([u[:, half:], v[:, half:]], axis=1)], axis=1)


def _peer(eidx, h2, gate2, x2d, uv, tt):
    n = x2d.shape[0]
    assert tt % (2 * SUBLANES) == 0
    row = lambda i: (i, 0)
    return pl.pallas_call(
        _peer_kernel,
        out_shape=jax.ShapeDtypeStruct((n, D_MODEL), F32),
        grid=(n // tt,),
        in_specs=[
            pl.BlockSpec(memory_space=pl.ANY),
            pl.BlockSpec((tt, D_MODEL), row),
            pl.BlockSpec((tt, 2 * N_SEL), row),
            pl.BlockSpec((tt, D_MODEL), row),
            pl.BlockSpec(memory_space=pl.ANY),
        ],
        out_specs=pl.BlockSpec((tt, D_MODEL), row),
        scratch_shapes=[
            pltpu.SMEM((2, tt, N_SEL), jnp.int32),
            pltpu.VMEM((2, SUBLANES, 2 * N_SEL, D_MODEL), BF16),
            pltpu.SemaphoreType.DMA((2,)),
            pltpu.SemaphoreType.DMA((2,)),
        ],
        compiler_params=_params(("arbitrary",)),
        name="peer_mix",
    )(eidx, h2, gate2, x2d, uv)


def _ple_kernel(x_ref, p_ref, lnp_ref, wg_ref, wp_ref, lnf_ref, y_ref):
    x = x_ref[...]
    gate = _sigmoid(_dot(_rms(x, lnp_ref[...]).astype(BF16), wg_ref[...]))
    x = x + gate * _dot(p_ref[...].astype(BF16), wp_ref[...])
    y_ref[...] = _rms(x, lnf_ref[...])


def _ple(x2d, p2d, ln_ple, w_gate, w_proj, ln_final, tm):
    n = x2d.shape[0]
    pd = p2d.shape[1]
    row = lambda i: (i, 0)
    const = lambda i: (0, 0)
    return pl.pallas_call(
        _ple_kernel,
        out_shape=jax.ShapeDtypeStruct((n, D_MODEL), F32),
        grid=(n // tm,),
        in_specs=[
            pl.BlockSpec((tm, D_MODEL), row),
            pl.BlockSpec((tm, pd), row),
            pl.BlockSpec((1, D_MODEL), const),
            pl.BlockSpec((D_MODEL, D_MODEL), const),
            pl.BlockSpec((pd, D_MODEL), const),
            pl.BlockSpec((1, D_MODEL), const),
        ],
        out_specs=pl.BlockSpec((tm, D_MODEL), row),
        compiler_params=_params(("arbitrary",)),
        name="ple_final",
    )(x2d, p2d, ln_ple, w_gate, w_proj, ln_final)


def _rope_tables(pos):
    inv = ROPE_THETA ** (-jnp.arange(0, HEAD_DIM_A, 2, dtype=F32) / HEAD_DIM_A)
    ang = pos.astype(F32)[:, None] * inv[None, :]
    cos, sin = jnp.cos(ang), jnp.sin(ang)
    reps = PROJ_TILE // HEAD_DIM_A
    cos = jnp.tile(jnp.concatenate([cos, cos], axis=1), (1, reps))
    sin = jnp.tile(jnp.concatenate([-sin, sin], axis=1), (1, reps))
    k_end = (COL["k_a"][0] + COL["k_a"][1]) % PROJ_TILE
    if k_end:
        keep = jnp.arange(PROJ_TILE) < k_end
        cos = jnp.concatenate([cos, jnp.where(keep, cos, 1.0)], axis=1)
        sin = jnp.concatenate([sin, jnp.where(keep, sin, 0.0)], axis=1)
    return cos, sin


def _pick(n, prefs):
    for c in prefs:
        if n % c == 0:
            return c
    raise ValueError(f"no tile of {prefs} divides {n}")


def _group(x, pe, pos0, cache_k, cache_v, s0, wts):
    bn, t, _ = x.shape
    n = bn * t
    prompt = cache_k is None
    x2d = x.reshape(n, D_MODEL)
    tm = _pick(n, (1024, 512, 256, 128))

    pos = pos0 + jnp.arange(t, dtype=jnp.int32)
    cos, sin = _rope_tables(pos)
    if t < tm:
        cos, sin = jnp.tile(cos, (tm // t, 1)), jnp.tile(sin, (tm // t, 1))
    proj = _in_proj(x2d, wts["ln1"], wts["w_in"], cos, sin, wts["colscale"], tm)
    proj3 = proj.reshape(bn, t, IN_COLS)

    if prompt:
        c_att, bt_att = ATTN_BLOCK, 1
        ctx_k = ctx_v = proj3
    else:
        c_att, bt_att = t, _pick(bn, (8, 4, 2, 1))
        ctx_k = cache_k.reshape(bn, WINDOW, N_KV_A * HEAD_DIM_A)
        ctx_v = cache_v.reshape(bn, WINDOW, N_KV_A * HEAD_DIM_A)
    o_a = _attention(wts["sinks"], proj3, ctx_k, ctx_v, bt_att, c_att, prompt)

    c_ret = math.gcd(t, RET_CHUNK)
    bt_ret = 1 if prompt else _pick(bn, (8, 4, 2, 1))
    o_r, s_new = _retention(proj3, s0, wts["ln_ret"], bt_ret, c_ret)

    m = _merge(o_a.reshape(n, -1), o_r.reshape(n, -1), proj, wts["w_br_a"], wts["w_br_r"], tm)
    x2 = _out_proj(x2d, m, wts["w_o"], tm)

    h2, scores_t = _scores(x2, wts["ln2"], wts["w_q"], wts["sub_keys"], _pick(n, (1024, 512, 256, 128)))
    eidx_t, gate_t = _topk(scores_t, _pick(n, (512, 256, 128)))
    eidx = eidx_t.reshape(N_SEL, n).T
    gate2 = jnp.repeat(gate_t.reshape(N_SEL, n).T, 2, axis=1)
    x3 = _peer(eidx, h2, gate2, x2, wts["peer_uv"], _pick(n, (128, 64, 32, 16)))

    y = _ple(x3, pe.reshape(n, -1), wts["ln_ple"], wts["w_ple_gate"], wts["w_ple_proj"], wts["ln_final"],
             _pick(n, (512, 256, 128)))

    k_off, v_off = COL["k_a"][0], COL["v_a"][0]
    kv_w = N_KV_A * HEAD_DIM_A
    k_new, v_new = proj3[:, :, k_off:k_off + kv_w], proj3[:, :, v_off:v_off + kv_w]
    if prompt:
        k_win, v_win = k_new[:, -WINDOW:], v_new[:, -WINDOW:]
    else:
        k_win = jnp.concatenate([ctx_k, k_new], axis=1)[:, -WINDOW:]
        v_win = jnp.concatenate([ctx_v, v_new], axis=1)[:, -WINDOW:]
    shp = (bn, WINDOW, N_KV_A, HEAD_DIM_A)
    return y.reshape(bn, t, D_MODEL), k_win.reshape(shp), v_win.reshape(shp), s_new


def kernel(x_prompt, x_sample, cache_k_win, cache_v_win, state_ret, p_prompt, p_sample, ln1, w_in, attn_sinks, ln_ret, w_branch_attn, w_branch_ret, w_out, ln2, w_peer_query, peer_sub_keys, peer_u, peer_v, ln_ple, w_ple_gate, w_ple_proj, ln_final):
    depth = ln1.shape[0]
    assert depth == 1, "single-layer step"
    i = 0
    colscale = jnp.ones((IN_COLS,), F32).at[COL["k_r"][0]:COL["k_r"][0] + COL["k_r"][1]].set(DK_R ** -0.5)
    wts = dict(
        ln1=ln1[i][None, :],
        w_in=jnp.concatenate([w_in[i][:, _REF_COL[f][0]:_REF_COL[f][0] + _REF_COL[f][1]] for f in _NEW_ORDER],
                             axis=1).astype(BF16),
        colscale=colscale[None, :],
        sinks=attn_sinks[i],
        ln_ret=ln_ret[i].reshape(N_HEADS_R, DV_R),
        w_br_a=w_branch_attn[i].astype(BF16),
        w_br_r=w_branch_ret[i].astype(BF16),
        w_o=w_out[i].astype(BF16),
        ln2=ln2[i][None, :],
        w_q=w_peer_query[i].astype(BF16),
        sub_keys=peer_sub_keys[i].astype(BF16),
        peer_uv=_pair_rows(peer_u[i].astype(BF16), peer_v[i].astype(BF16)),
        ln_ple=ln_ple[i][None, :],
        w_ple_gate=w_ple_gate[i].astype(BF16),
        w_ple_proj=w_ple_proj[i].astype(BF16),
        ln_final=ln_final[None, :],
    )
    yp, kp, vp, sp = _group(x_prompt, p_prompt[i], 0, None, None, None, wts)
    ys, ks, vs, ss = _group(x_sample, p_sample[i], PAST_LEN, cache_k_win[i], cache_v_win[i], state_ret[i], wts)
    return (yp, ys, kp[None], vp[None], sp[None], ks[None], vs[None], ss[None])
```
